```python
import jax, jax.numpy as jnp
from jax import lax
import numpy as np

D_MODEL = 2048
BATCH = 8
SEQ = 2048
DEPTH = 1
DEC_BATCH = 32
DEC_SEQ = 1
PAST_LEN = 16384
PAGE_SIZE = 128

GDN_HEADS = 8
GDN_DK = 128
GDN_DV = 128
GDN_CONV = 4
GDN_CHUNK = 64
GDN_QK = GDN_HEADS * GDN_DK
GDN_VW = GDN_HEADS * GDN_DV
CONV_CH = 2 * GDN_QK + GDN_VW
DWA_GROUPS = ((128, 1), (512, 4), (2048, 16))
N_DWA_GROUPS = 3
DWA_HEADS = 4
DWA_DH = 128
DWA_WIDTH = N_DWA_GROUPS * DWA_HEADS * DWA_DH
DWA_BLOCK = 128
ROPE_THETA = 10000.0
D_FF = 5632
N_MOD = 9
NORM_EPS = 1e-6

kernel_name = "hybrid_gdn_dilated_window_decoder_step"

F32 = jnp.float32


def split_cols(x, sizes):
    idx = np.cumsum(sizes)[:-1].tolist()
    return jnp.split(x, idx, axis=-1)


def rmsnorm(x, g):
    xf = x.astype(F32)
    xf = xf * lax.rsqrt(jnp.mean(xf * xf, axis=-1, keepdims=True) + NORM_EPS)
    return (xf * g.astype(F32)).astype(x.dtype)


def modulate(x, gain, shift, scale):
    return rmsnorm(x, gain) * (1.0 + scale[:, None, :]) + shift[:, None, :]


def swiglu(x, w_up, w_down):
    g, u = jnp.split(x @ w_up, 2, axis=-1)
    return (jax.nn.silu(g) * u) @ w_down


def l2norm(x):
    return x * lax.rsqrt(jnp.sum(x * x, axis=-1, keepdims=True) + NORM_EPS)


def rope(x, pos):
    half = DWA_DH // 2
    inv = jnp.power(ROPE_THETA, -jnp.arange(half, dtype=F32) / half)
    ang = pos.astype(F32)[:, None] * inv[None, :]
    cos = jnp.cos(ang)[None, :, None, None, :]
    sin = jnp.sin(ang)[None, :, None, None, :]
    xf = x.astype(F32)
    x1, x2 = xf[..., :half], xf[..., half:]
    return jnp.concatenate([x1 * cos - x2 * sin, x2 * cos + x1 * sin], axis=-1).astype(x.dtype)


def causal_conv(full, w):
    T = full.shape[1] - (GDN_CONV - 1)
    acc = full[:, 0:T] * w[0]
    for j in range(1, GDN_CONV):
        acc = acc + full[:, j:j + T] * w[j]
    return jax.nn.silu(acc)


def gdn_chunked(q, k, v, beta, g, s0):
    B, T, H, dk = q.shape
    C = GDN_CHUNK
    nc = T // C
    to_c = lambda x: x.reshape(B, nc, C, H, x.shape[-1]).transpose(1, 0, 3, 2, 4)
    to_s = lambda x: x.reshape(B, nc, C, H).transpose(1, 0, 3, 2)
    qc, kc, vc = to_c(q), to_c(k), to_c(v)
    bc, G = to_s(beta), jnp.cumsum(to_s(g), axis=-1)
    ii = jnp.arange(C)
    strict = ii[:, None] > ii[None, :]
    incl = ii[:, None] >= ii[None, :]
    diff = G[..., :, None] - G[..., None, :]
    dec_strict = jnp.where(strict, jnp.exp(jnp.where(strict, diff, 0.0)), 0.0)
    dec_incl = jnp.where(incl, jnp.exp(jnp.where(incl, diff, 0.0)), 0.0)
    kb = kc * bc[..., None]
    vb = vc * bc[..., None]
    a_mat = jnp.einsum('nbhid,nbhjd->nbhij', kb, kc) * dec_strict
    rhs = jnp.concatenate([kb * jnp.exp(G)[..., None], vb], axis=-1)
    sol = lax.linalg.triangular_solve(jnp.eye(C, dtype=F32) + a_mat, rhs,
                                      left_side=True, lower=True, unit_diagonal=True)
    w_c, u_c = sol[..., :dk], sol[..., dk:]
    qk = jnp.einsum('nbhid,nbhjd->nbhij', qc, kc) * dec_incl
    q_dec = qc * jnp.exp(G)[..., None]
    k_dec = kc * jnp.exp(G[..., -1:] - G)[..., None]
    g_last = jnp.exp(G[..., -1])

    def step(S, xs):
        wc, uc, qkc, qdc, kdc, glc = xs
        v_new = uc - jnp.einsum('bhcd,bhde->bhce', wc, S)
        o = jnp.einsum('bhcd,bhde->bhce', qdc, S) + jnp.einsum('bhij,bhje->bhie', qkc, v_new)
        S = S * glc[..., None, None] + jnp.einsum('bhcd,bhce->bhde', kdc, v_new)
        return S, o

    S, o = lax.scan(step, s0, (w_c, u_c, qk, q_dec, k_dec, g_last))
    o = o.transpose(1, 0, 3, 2, 4).reshape(B, T, H, v.shape[-1])
    return o, S


def gdn_recurrent(q, k, v, beta, g, s0):
    def step(S, xs):
        qt, kt, vt, bt, gt = xs
        S = S * jnp.exp(gt)[..., None, None]
        kv = jnp.einsum('bhd,bhde->bhe', kt, S)
        delta = bt[..., None] * (vt - kv)
        S = S + jnp.einsum('bhd,bhe->bhde', kt, delta)
        return S, jnp.einsum('bhd,bhde->bhe', qt, S)

    xs = tuple(jnp.moveaxis(a, 1, 0) for a in (q, k, v, beta, g))
    S, o = lax.scan(step, s0, xs)
    return jnp.moveaxis(o, 0, 1), S


def softmax_stats(s, vals_einsum, vals):
    m = jnp.max(s, axis=-1, keepdims=True)
    p = jnp.exp(s - m)
    l = jnp.sum(p, axis=-1, keepdims=True)
    o = jnp.einsum(vals_einsum, p, vals) / l
    return o, (m + jnp.log(l))[..., 0]


def dilated_prompt(q, k, v, dil, span):
    B, T, H, dh = q.shape
    Ls = T // dil
    nb = -(-Ls // DWA_BLOCK)
    Lp = nb * DWA_BLOCK
    streams = lambda x: x.reshape(B, Ls, dil, H, dh).transpose(0, 2, 3, 1, 4).astype(F32)
    qs = jnp.pad(streams(q), ((0, 0), (0, 0), (0, 0), (0, Lp - Ls), (0, 0)))
    qs = qs.reshape(B, dil, H, nb, DWA_BLOCK, dh)

    def kblocks(x):
        xs = jnp.pad(streams(x), ((0, 0), (0, 0), (0, 0), (DWA_BLOCK, Lp - Ls), (0, 0)))
        xs = xs.reshape(B, dil, H, nb + 1, DWA_BLOCK, dh)
        return jnp.concatenate([xs[:, :, :, :-1], xs[:, :, :, 1:]], axis=-2)

    kb, vb = kblocks(k), kblocks(v)
    s = jnp.einsum('bdhnqe,bdhnke->bdhnqk', qs, kb) * (DWA_DH ** -0.5)
    r = jnp.arange(DWA_BLOCK)[:, None]
    c = jnp.arange(2 * DWA_BLOCK)[None, :]
    dist = DWA_BLOCK + r - c
    kpos = jnp.arange(nb)[:, None, None] * DWA_BLOCK - DWA_BLOCK + c[None]
    valid = (dist >= 0)[None] & (dist <= span)[None] & (kpos >= 0)
    s = jnp.where(valid, s, -jnp.inf)
    o, lse = softmax_stats(s, 'bdhnqk,bdhnke->bdhnqe', vb)
    o = o.reshape(B, dil, H, Lp, dh)[:, :, :, :Ls].transpose(0, 3, 1, 2, 4).reshape(B, T, H, dh)
    lse = lse.reshape(B, dil, H, Lp)[..., :Ls].transpose(0, 3, 1, 2).reshape(B, T, H)
    return o, lse


def dilated_sample(q, k_new, v_new, kv_buf, dil, span):
    S = q.shape[1]
    L = kv_buf.shape[1]
    keys = jnp.concatenate([kv_buf[:, :, 0].astype(F32), k_new.astype(F32)], axis=1)
    vals = jnp.concatenate([kv_buf[:, :, 1].astype(F32), v_new.astype(F32)], axis=1)
    idx = L + jnp.arange(S)[:, None] - jnp.arange(span + 1)[None, :] * dil
    valid = idx >= 0
    idx = jnp.maximum(idx, 0)
    kg, vg = keys[:, idx], vals[:, idx]
    s = jnp.einsum('bshe,bsjhe->bshj', q.astype(F32), kg) * (DWA_DH ** -0.5)
    s = jnp.where(valid[None, :, None, :], s, -jnp.inf)
    return softmax_stats(s, 'bshj,bsjhe->bshe', vg)


def mixer(xn, pos, conv_hist, s0, kv_bufs, p, l):
    B, T, _ = xn.shape
    qkv_a, z, b_raw, a_raw, q_b, k_b, v_b, gate_a, gate_b = split_cols(
        xn @ p['w_in'][l],
        [CONV_CH, GDN_VW, GDN_HEADS, GDN_HEADS, DWA_WIDTH, DWA_WIDTH, DWA_WIDTH, D_MODEL, D_MODEL])
    full = jnp.concatenate([conv_hist.astype(xn.dtype), qkv_a], axis=1)
    new_hist = full[:, -(GDN_CONV - 1):]
    qa, ka, va = split_cols(causal_conv(full, p['conv_w'][l]).astype(F32), [GDN_QK, GDN_QK, GDN_VW])
    qa = l2norm(qa.reshape(B, T, GDN_HEADS, GDN_DK)) * (GDN_DK ** -0.5)
    ka = l2norm(ka.reshape(B, T, GDN_HEADS, GDN_DK))
    va = va.reshape(B, T, GDN_HEADS, GDN_DV)
    beta = jax.nn.sigmoid(b_raw.astype(F32))
    g = -jnp.exp(p['a_log'][l].astype(F32)) * jax.nn.softplus(a_raw.astype(F32) + p['dt_bias'][l].astype(F32))
    if kv_bufs is None:
        o_a, s_new = gdn_chunked(qa, ka, va, beta, g, s0.astype(F32))
    else:
        o_a, s_new = gdn_recurrent(qa, ka, va, beta, g, s0.astype(F32))
    o_a = rmsnorm(o_a, p['gdn_norm'][l]) * jax.nn.silu(z.astype(F32).reshape(B, T, GDN_HEADS, GDN_DV))
    o_a = o_a.reshape(B, T, GDN_VW).astype(xn.dtype)
    shp = (B, T, N_DWA_GROUPS, DWA_HEADS, DWA_DH)
    qb = rope(q_b.reshape(shp), pos)
    kb = rope(k_b.reshape(shp), pos)
    vb = v_b.reshape(shp)
    outs, lses, new_kv = [], [], []
    for gi, (win, dil) in enumerate(DWA_GROUPS):
        span = win // dil
        qg, kg, vg = qb[:, :, gi], kb[:, :, gi], vb[:, :, gi]
        if kv_bufs is None:
            o, lse = dilated_prompt(qg, kg, vg, dil, span)
            keep = min(win, T)
            new_kv.append(jnp.stack([kg[:, T - keep:], vg[:, T - keep:]], axis=2))
        else:
            o, lse = dilated_sample(qg, kg, vg, kv_bufs[gi], dil, span)
            new_kv.append(jnp.stack([kg, vg], axis=2))
        outs.append(o)
        lses.append(lse)
    wts = jax.nn.softmax(jnp.stack(lses, axis=0), axis=0)
    o_b = sum(wts[gi][..., None] * outs[gi] for gi in range(N_DWA_GROUPS))
    o_b = o_b.reshape(B, T, DWA_HEADS * DWA_DH).astype(xn.dtype)
    merged = jax.nn.sigmoid(gate_a) * (o_a @ p['w_proj_a'][l]) + jax.nn.sigmoid(gate_b) * (o_b @ p['w_proj_b'][l])
    return merged @ p['w_out'][l], new_hist, s_new, new_kv


def trunk(x, c, pos, states, p):
    h = x
    new_states = []
    for l in range(DEPTH):
        mod = jax.nn.silu(c) @ p['w_ada'][l] + p['b_ada'][l]
        sh1, sc1, gt1, shm, scm, gtm, sh2, sc2, gt2 = jnp.split(mod, N_MOD, axis=-1)
        h = h + 0.5 * gt1[:, None] * swiglu(modulate(h, p['norm_ffn1'][l], sh1, sc1),
                                            p['w_ffn1_up'][l], p['w_ffn1_down'][l])
        conv_hist, s0, kv_bufs = states[l]
        y, nh, ns, nkv = mixer(modulate(h, p['norm_mix'][l], shm, scm), pos, conv_hist, s0, kv_bufs, p, l)
        h = h + gtm[:, None] * y
        h = h + 0.5 * gt2[:, None] * swiglu(modulate(h, p['norm_ffn2'][l], sh2, sc2),
                                            p['w_ffn2_up'][l], p['w_ffn2_down'][l])
        new_states.append((nh, ns, nkv))
    return rmsnorm(h, p['norm_final']), new_states


def setup_inputs(seed: int = 0) -> dict:
    key = jax.random.key(seed)
    ks = jax.random.split(key, 32)
    nrm = lambda k, shape, s: jax.random.normal(k, shape, F32) * s
    in_cols = CONV_CH + GDN_VW + 2 * GDN_HEADS + 3 * DWA_WIDTH + 2 * D_MODEL
    dt = jax.random.uniform(ks[20], (DEPTH, GDN_HEADS), F32, 0.001, 0.1)
    return {
        'x_prompt': nrm(ks[0], (BATCH, SEQ, D_MODEL), 1.0),
        'x_sample': nrm(ks[1], (DEC_BATCH, DEC_SEQ, D_MODEL), 1.0),
        'state_conv': nrm(ks[2], (DEPTH, DEC_BATCH, GDN_CONV - 1, CONV_CH), 1.0),
        'state_delta': nrm(ks[3], (DEPTH, DEC_BATCH, GDN_HEADS, GDN_DK, GDN_DV), 0.05),
        'cache_kv_w128': nrm(ks[4], (DEPTH, DEC_BATCH, min(DWA_GROUPS[0][0], PAST_LEN), 2, DWA_HEADS, DWA_DH), 1.0),
        'cache_kv_w512': nrm(ks[5], (DEPTH, DEC_BATCH, min(DWA_GROUPS[1][0], PAST_LEN), 2, DWA_HEADS, DWA_DH), 1.0),
        'cache_kv_w2048': nrm(ks[6], (DEPTH, DEC_BATCH, min(DWA_GROUPS[2][0], PAST_LEN), 2, DWA_HEADS, DWA_DH), 1.0),
        'c_prompt': nrm(ks[7], (BATCH, D_MODEL), 1.0),
        'c_sample': nrm(ks[8], (DEC_BATCH, D_MODEL), 1.0),
        'w_ada': nrm(ks[9], (DEPTH, D_MODEL, N_MOD * D_MODEL), 0.5 * D_MODEL ** -0.5),
        'b_ada': nrm(ks[10], (DEPTH, N_MOD * D_MODEL), 0.02),
        'norm_ffn1': 1.0 + nrm(ks[11], (DEPTH, D_MODEL), 0.05),
        'w_ffn1_up': nrm(ks[12], (DEPTH, D_MODEL, 2 * D_FF), D_MODEL ** -0.5),
        'w_ffn1_down': nrm(ks[13], (DEPTH, D_FF, D_MODEL), D_FF ** -0.5),
        'norm_mix': 1.0 + nrm(ks[14], (DEPTH, D_MODEL), 0.05),
        'w_in': nrm(ks[15], (DEPTH, D_MODEL, in_cols), D_MODEL ** -0.5),
        'conv_w': nrm(ks[16], (DEPTH, GDN_CONV, CONV_CH), GDN_CONV ** -0.5),
        'a_log': jnp.log(jax.random.uniform(ks[17], (DEPTH, GDN_HEADS), F32, 1.0, 16.0)),
        'dt_bias': jnp.log(jnp.expm1(dt)),
        'gdn_norm': 1.0 + nrm(ks[18], (DEPTH, GDN_DV), 0.05),
        'w_proj_a': nrm(ks[19], (DEPTH, GDN_VW, D_MODEL), GDN_VW ** -0.5),
        'w_proj_b': nrm(ks[21], (DEPTH, DWA_HEADS * DWA_DH, D_MODEL), (DWA_HEADS * DWA_DH) ** -0.5),
        'w_out': nrm(ks[22], (DEPTH, D_MODEL, D_MODEL), D_MODEL ** -0.5),
        'norm_ffn2': 1.0 + nrm(ks[23], (DEPTH, D_MODEL), 0.05),
        'w_ffn2_up': nrm(ks[24], (DEPTH, D_MODEL, 2 * D_FF), D_MODEL ** -0.5),
        'w_ffn2_down': nrm(ks[25], (DEPTH, D_FF, D_MODEL), D_FF ** -0.5),
        'norm_final': 1.0 + nrm(ks[26], (D_MODEL,), 0.05),
    }


def reference(x_prompt, x_sample, state_conv, state_delta, cache_kv_w128, cache_kv_w512, cache_kv_w2048,
              c_prompt, c_sample, w_ada, b_ada, norm_ffn1, w_ffn1_up, w_ffn1_down, norm_mix, w_in, conv_w,
              a_log, dt_bias, gdn_norm, w_proj_a, w_proj_b, w_out, norm_ffn2, w_ffn2_up, w_ffn2_down,
              norm_final):
    p = dict(w_ada=w_ada, b_ada=b_ada, norm_ffn1=norm_ffn1, w_ffn1_up=w_ffn1_up, w_ffn1_down=w_ffn1_down,
             norm_mix=norm_mix, w_in=w_in, conv_w=conv_w, a_log=a_log, dt_bias=dt_bias, gdn_norm=gdn_norm,
             w_proj_a=w_proj_a, w_proj_b=w_proj_b, w_out=w_out, norm_ffn2=norm_ffn2, w_ffn2_up=w_ffn2_up,
             w_ffn2_down=w_ffn2_down, norm_final=norm_final)
    B, T, _ = x_prompt.shape
    zero_hist = jnp.zeros((B, GDN_CONV - 1, CONV_CH), x_prompt.dtype)
    zero_s = jnp.zeros((B, GDN_HEADS, GDN_DK, GDN_DV), F32)
    prompt_states = [(zero_hist, zero_s, None) for _ in range(DEPTH)]
    sample_states = [(state_conv[l], state_delta[l], (cache_kv_w128[l], cache_kv_w512[l], cache_kv_w2048[l]))
                     for l in range(DEPTH)]
    pos_p = jnp.arange(T, dtype=jnp.int32)
    pos_s = PAST_LEN + jnp.arange(x_sample.shape[1], dtype=jnp.int32)
    y_prompt, ns_p = trunk(x_prompt, c_prompt, pos_p, prompt_states, p)
    y_sample, ns_s = trunk(x_sample, c_sample, pos_s, sample_states, p)
    conv_p = jnp.stack([s[0] for s in ns_p])
    delta_p = jnp.stack([s[1] for s in ns_p])
    kv128_p = jnp.stack([s[2][0] for s in ns_p])
    kv512_p = jnp.stack([s[2][1] for s in ns_p])
    kv2048_p = jnp.stack([s[2][2] for s in ns_p])
    conv_s = jnp.stack([s[0] for s in ns_s])
    delta_s = jnp.stack([s[1] for s in ns_s])
    kv128_s = jnp.stack([s[2][0] for s in ns_s])
    kv512_s = jnp.stack([s[2][1] for s in ns_s])
    kv2048_s = jnp.stack([s[2][2] for s in ns_s])
    return (y_prompt, y_sample, conv_p, delta_p, kv128_p, kv512_p, kv2048_p,
            conv_s, delta_s, kv128_s, kv512_s, kv2048_s)
```

```python
import functools

import jax
import jax.numpy as jnp
import numpy as np
from jax import lax
from jax.experimental import pallas as pl
from jax.experimental.pallas import tpu as pltpu

F32 = jnp.float32
BF16 = jnp.bfloat16

D_MODEL = 2048
D_FF = 5632
N_MOD = 9
NORM_EPS = 1e-6
PAST_LEN = 16384

GDN_HEADS = 8
GDN_DK = 128
GDN_CONV = 4
GDN_CHUNK = 64
GDN_QK = GDN_HEADS * GDN_DK
CONV_CH = 3 * GDN_QK

DWA_GROUPS = ((128, 1), (512, 4), (2048, 16))
DWA_HEADS = 4
DWA_DH = 128
DWA_GW = DWA_HEADS * DWA_DH
DWA_WIDTH = len(DWA_GROUPS) * DWA_GW
DWA_BLOCK = 128
ROPE_THETA = 10000.0

LANES = 128
VMEM_LIMIT = 56 * 1024 * 1024

_C_Z = CONV_CH
_C_B = _C_Z + GDN_QK
_C_A = _C_B + GDN_HEADS
_C_Q = _C_A + GDN_HEADS
_C_K = _C_Q + DWA_WIDTH
_C_V = _C_K + DWA_WIDTH
_C_GA = _C_V + DWA_WIDTH
_C_GB = _C_GA + D_MODEL

_HEAD_ORDER = (0, 2, 4, 6, 1, 3, 5, 7)
_N_PAIRS = GDN_HEADS // 2


def _sigmoid(x):
    return 1.0 / (1.0 + jnp.exp(-x))


def _silu(x):
    return x * _sigmoid(x)


def _dot(a, b):
    return jnp.dot(a, b, preferred_element_type=F32)


def _dot_nt(a, b):
    return lax.dot_general(a, b, (((1,), (1,)), ((), ())), preferred_element_type=F32)


def _split(x):
    hi = x.astype(BF16)
    lo = (x - hi.astype(F32)).astype(BF16)
    return hi, lo


def _dot3(a, b):
    ah, al = _split(a)
    bh, bl = _split(b)
    return _dot(ah, bh) + (_dot(al, bh) + _dot(ah, bl))


def _params(*sem):
    return pltpu.CompilerParams(dimension_semantics=sem, vmem_limit_bytes=VMEM_LIMIT)


def _ada_kernel(c_ref, w_ref, b_ref, o_ref):
    a = _silu(c_ref[...]).astype(BF16)
    o_ref[...] = _dot(a, w_ref[...].astype(BF16)) + b_ref[...]


def _ada(c, w, b):
    m, n = c.shape[0], w.shape[1]
    tn = 1024
    return pl.pallas_call(
        _ada_kernel,
        out_shape=jax.ShapeDtypeStruct((m, n), F32),
        grid=(n // tn,),
        in_specs=[pl.BlockSpec((m, D_MODEL), lambda j: (0, 0)),
                  pl.BlockSpec((D_MODEL, tn), lambda j: (0, j)),
                  pl.BlockSpec((1, tn), lambda j: (0, j))],
        out_specs=pl.BlockSpec((m, tn), lambda j: (0, j)),
        compiler_params=_params("arbitrary"),
        name="ada_mod",
    )(c, w, b)


def _rms(x):
    return x * lax.rsqrt(jnp.mean(x * x, axis=-1, keepdims=True) + NORM_EPS)


def _ffn_kernel(x_ref, sh_ref, sc_ref, gt_ref, gain_ref, wg_ref, wu_ref, wd_ref, *rest, next_mod, final_norm):
    if next_mod:
        nsh_ref, nsc_ref, ngain_ref, out_ref, nxt_ref, xm_sc, acc_sc = rest
    elif final_norm:
        ngain_ref, out_ref, xm_sc, acc_sc = rest
    else:
        out_ref, xm_sc, acc_sc = rest
    j = pl.program_id(2)

    @pl.when(j == 0)
    def _():
        xn = _rms(x_ref[0]) * gain_ref[...]
        xm_sc[...] = (xn * (1.0 + sc_ref[0]) + sh_ref[0]).astype(BF16)
        acc_sc[...] = jnp.zeros_like(acc_sc)

    xm = xm_sc[...]
    g = _dot(xm, wg_ref[...])
    u = _dot(xm, wu_ref[...])
    acc_sc[...] += _dot((_silu(g) * u).astype(BF16), wd_ref[...])

    @pl.when(j == pl.num_programs(2) - 1)
    def _():
        h = x_ref[0] + 0.5 * gt_ref[0] * acc_sc[...]
        if final_norm:
            out_ref[0] = _rms(h) * ngain_ref[...]
        else:
            out_ref[0] = h
        if next_mod:
            hn = _rms(h) * ngain_ref[...]
            nxt_ref[0] = (hn * (1.0 + nsc_ref[0]) + nsh_ref[0]).astype(BF16)


def _ffn(x3, mod3, k_sh, gain, w_up, w_down, *, next_k=None, next_gain=None, final_gain=None):
    G, R, _ = x3.shape
    rm = mod3.shape[1]
    tm = min(512, R)
    tf = 512
    nff = D_FF // tf
    tmm = tm if rm == R else 1
    row = (lambda g, i, j: (g, i, 0)) if rm == R else (lambda g, i, j: (g, 0, 0))

    def mod_spec(k):
        if rm == R:
            return pl.BlockSpec((1, tmm, D_MODEL), lambda g, i, j: (g, i, k))
        return pl.BlockSpec((1, 1, D_MODEL), lambda g, i, j: (g, 0, k))

    vec = pl.BlockSpec((1, D_MODEL), lambda g, i, j: (0, 0))
    xspec = pl.BlockSpec((1, tm, D_MODEL), lambda g, i, j: (g, i, 0))
    in_specs = [xspec, mod_spec(k_sh), mod_spec(k_sh + 1), mod_spec(k_sh + 2), vec,
                pl.BlockSpec((D_MODEL, tf), lambda g, i, j: (0, j)),
                pl.BlockSpec((D_MODEL, tf), lambda g, i, j: (0, j + nff)),
                pl.BlockSpec((tf, D_MODEL), lambda g, i, j: (j, 0))]
    args = [x3, mod3, mod3, mod3, gain.reshape(1, D_MODEL), w_up, w_up, w_down]
    out_shape = [jax.ShapeDtypeStruct((G, R, D_MODEL), F32)]
    out_specs = [xspec]
    next_mod = next_k is not None
    if next_mod:
        in_specs += [mod_spec(next_k), mod_spec(next_k + 1), vec]
        args += [mod3, mod3, next_gain.reshape(1, D_MODEL)]
        out_shape.append(jax.ShapeDtypeStruct((G, R, D_MODEL), BF16))
        out_specs.append(xspec)
    elif final_gain is not None:
        in_specs.append(vec)
        args.append(final_gain.reshape(1, D_MODEL))
    del row
    res = pl.pallas_call(
        functools.partial(_ffn_kernel, next_mod=next_mod, final_norm=final_gain is not None),
        out_shape=out_shape,
        grid=(G, R // tm, nff),
        in_specs=in_specs,
        out_specs=out_specs,
        scratch_shapes=[pltpu.VMEM((tm, D_MODEL), BF16), pltpu.VMEM((tm, D_MODEL), F32)],
        compiler_params=_params("parallel", "parallel", "arbitrary"),
        name="ffn_next" if next_mod else "ffn_final",
    )(*args)
    return res if next_mod else res[0]


def _rope_tile(x, cosf, sinf):
    return x * cosf + pltpu.roll(x, DWA_DH // 2, axis=1) * sinf


def _proj_kernel(a_ref, w_ref, *rest, rope, tn):
    if rope == "none":
        (o_ref,) = rest
        o_ref[0] = _dot(a_ref[0], w_ref[...]).astype(o_ref.dtype)
        return
    cos_ref, sin_ref, o_ref = rest
    acc = _dot(a_ref[0], w_ref[...])

    def roped():
        cosf, sinf = cos_ref[...], sin_ref[...]
        for h in range(tn // DWA_DH):
            sl = slice(h * DWA_DH, (h + 1) * DWA_DH)
            o_ref[0, :, sl] = _rope_tile(acc[:, sl], cosf, sinf).astype(o_ref.dtype)

    if rope == "all":
        roped()
    else:
        j = pl.program_id(2)
        pl.when(j == 0)(roped)

        @pl.when(j != 0)
        def _():
            o_ref[0] = acc.astype(o_ref.dtype)


def _proj(xm3, w, out_dtype, *, rope="none", tables=None, tn=512, name="proj"):
    G, R, _ = xm3.shape
    n = w.shape[1]
    tm = min(1024, R)
    in_specs = [pl.BlockSpec((1, tm, D_MODEL), lambda g, i, j: (g, i, 0)),
                pl.BlockSpec((D_MODEL, tn), lambda g, i, j: (0, j))]
    args = [xm3, w]
    if rope != "none":
        tab = pl.BlockSpec((tm, DWA_DH), lambda g, i, j: (i, 0))
        in_specs += [tab, tab]
        args += list(tables)
    return pl.pallas_call(
        functools.partial(_proj_kernel, rope=rope, tn=tn),
        out_shape=jax.ShapeDtypeStruct((G, R, n), out_dtype),
        grid=(G, R // tm, n // tn),
        in_specs=in_specs,
        out_specs=pl.BlockSpec((1, tm, tn), lambda g, i, j: (g, i, j)),
        compiler_params=_params("parallel", "parallel", "arbitrary"),
        name=name,
    )(*args)


def _rope_tables(pos):
    half = DWA_DH // 2
    inv = jnp.power(ROPE_THETA, -jnp.arange(half, dtype=F32) / half)
    ang = pos.astype(F32)[:, None] * inv[None, :]
    cos, sin = jnp.cos(ang), jnp.sin(ang)
    return jnp.concatenate([cos, cos], axis=-1), jnp.concatenate([-sin, sin], axis=-1)


def _gdn_gates(ba, par):
    beta = _sigmoid(ba)
    x = ba + par[1:2]
    softplus = jnp.maximum(x, 0.0) + jnp.log1p(jnp.exp(-jnp.abs(x)))
    return beta, -jnp.exp(par[0:1]) * softplus


def _gated_out(o, z, gn):
    return (_rms(o) * gn) * _silu(z)


def _gdn_chunk_kernel(qkv_ref, z_ref, ba_ref, cw_ref, par_ref, gn_ref, o_ref, s_ref, buf_sc, x_sc, S_sc):
    C = GDN_CHUNK
    c = pl.program_id(1)

    @pl.when(c == 0)
    def _():
        buf_sc[0:8, :] = jnp.zeros((8, CONV_CH), F32)
        S_sc[...] = jnp.zeros_like(S_sc)

    buf_sc[8:8 + C, :] = qkv_ref[0]
    acc = buf_sc[5:5 + C, :] * cw_ref[0:1, :]
    for j in range(1, GDN_CONV):
        acc = acc + buf_sc[5 + j:5 + j + C, :] * cw_ref[j:j + 1, :]
    x_sc[...] = _silu(acc)
    buf_sc[0:8, :] = buf_sc[C:C + 8, :]

    beta_t, g_t = _gdn_gates(ba_ref[0], par_ref[...])
    row = lax.broadcasted_iota(jnp.int32, (C, LANES), 0)
    G = g_t
    s = 1
    while s < C:
        G = G + jnp.where(row >= s, pltpu.roll(G, s, axis=0), 0.0)
        s *= 2
    shift = LANES - _N_PAIRS
    Gs = jnp.concatenate([G, pltpu.roll(G, shift, axis=1)], axis=0)
    Bs = jnp.concatenate([beta_t, pltpu.roll(beta_t, shift, axis=1)], axis=0)
    GT = Gs.T
    r2 = lax.broadcasted_iota(jnp.int32, (2 * C, LANES), 0)
    Glast = jnp.where(r2 < C, Gs[C - 1:C, :], Gs[2 * C - 1:2 * C, :])
    eG = jnp.exp(Gs)
    eGl = jnp.exp(Glast - Gs)
    egl = jnp.exp(Glast)

    ii = lax.broadcasted_iota(jnp.int32, (2 * C, 2 * C), 0)
    jj = lax.broadcasted_iota(jnp.int32, (2 * C, 2 * C), 1)
    same = (ii // C) == (jj // C)
    strict = same & (ii > jj)
    diag = ii == jj
    blk = (ii // 16) == (jj // 16)
    top = r2 < C
    gn = gn_ref[...]

    for p in range(_N_PAIRS):
        def pair(off):
            a = x_sc[:, off + (2 * p) * GDN_DK: off + (2 * p + 1) * GDN_DK]
            b = x_sc[:, off + (2 * p + 1) * GDN_DK: off + (2 * p + 2) * GDN_DK]
            return jnp.concatenate([a, b], axis=0)

        q2 = pair(0)
        k2 = pair(GDN_QK)
        v2 = pair(2 * GDN_QK)
        q2 = q2 * lax.rsqrt(jnp.sum(q2 * q2, axis=-1, keepdims=True) + NORM_EPS) * (GDN_DK ** -0.5)
        k2 = k2 * lax.rsqrt(jnp.sum(k2 * k2, axis=-1, keepdims=True) + NORM_EPS)
        beta = Bs[:, p:p + 1]
        gc = Gs[:, 8 + p:9 + p]
        gr = GT[8 + p:9 + p, :]
        dec = jnp.where(strict, jnp.exp(jnp.where(strict, gc - gr, 0.0)), 0.0)
        kb = k2 * beta
        vb = v2 * beta
        k2b = k2.astype(BF16)
        A = _dot_nt(kb.astype(BF16), k2b) * dec
        qk = _dot_nt(q2.astype(BF16), k2b) * (dec + jnp.where(diag, 1.0, 0.0))

        Dg = jnp.where(blk, A, 0.0)
        E = A - Dg
        Q = -Dg
        Dp = Dg
        for _ in range(3):
            Dp = _dot3(Dp, Dp)
            Q = Q + Dp + _dot3(Q, Dp)
        N = E + _dot3(Q, E)
        N2 = _dot3(N, N)
        M = N2 - N - _dot3(N, N2)
        Tp = M + Q + _dot3(M, Q)
        rhs = jnp.concatenate([kb * eG[:, 8 + p:9 + p], vb], axis=1)
        sol = rhs + _dot3(Tp, rhs)
        w2 = sol[:, :GDN_DK]
        u2 = sol[:, GDN_DK:]

        qd2 = q2 * eG[:, 8 + p:9 + p]
        kd2 = k2 * eGl[:, 8 + p:9 + p]
        wq = jnp.concatenate([w2, qd2], axis=0).astype(BF16)
        S0 = S_sc[2 * p]
        S1 = S_sc[2 * p + 1]
        r0 = _dot(wq, S0.astype(BF16))
        r1 = _dot(wq, S1.astype(BF16))
        ws = jnp.where(top, r0[:2 * C], r1[:2 * C])
        qs = jnp.where(top, r0[2 * C:], r1[2 * C:])
        vn = u2 - ws
        vnb = vn.astype(BF16)
        o2 = qs + _dot(qk.astype(BF16), vnb)
        kdT = kd2.T.astype(BF16)
        S_sc[2 * p] = S0 * egl[0:1, 8 + p:9 + p] + _dot(kdT, jnp.where(top, vnb, jnp.zeros_like(vnb)))
        S_sc[2 * p + 1] = S1 * egl[C:C + 1, 8 + p:9 + p] + _dot(kdT, jnp.where(top, jnp.zeros_like(vnb), vnb))

        for e in range(2):
            h = 2 * p + e
            sl = slice(h * GDN_DK, (h + 1) * GDN_DK)
            o_ref[0, :, sl] = _gated_out(o2[e * C:(e + 1) * C], z_ref[0, :, sl], gn).astype(o_ref.dtype)

    @pl.when(c == pl.num_programs(1) - 1)
    def _():
        s_ref[0] = S_sc[...]


def _gdn_prompt(az, ba, conv_w, par, gn):
    B, T, _ = az.shape
    C = GDN_CHUNK
    return pl.pallas_call(
        _gdn_chunk_kernel,
        out_shape=[jax.ShapeDtypeStruct((B, T, GDN_QK), BF16),
                   jax.ShapeDtypeStruct((B, GDN_HEADS, GDN_DK, GDN_DK), F32)],
        grid=(B, T // C),
        in_specs=[pl.BlockSpec((1, C, CONV_CH), lambda b, c: (b, c, 0)),
                  pl.BlockSpec((1, C, GDN_QK), lambda b, c: (b, c, CONV_CH // GDN_QK)),
                  pl.BlockSpec((1, C, LANES), lambda b, c: (b, c, 0)),
                  pl.BlockSpec((GDN_CONV, CONV_CH), lambda b, c: (0, 0)),
                  pl.BlockSpec((8, LANES), lambda b, c: (0, 0)),
                  pl.BlockSpec((1, GDN_DK), lambda b, c: (0, 0))],
        out_specs=[pl.BlockSpec((1, C, GDN_QK), lambda b, c: (b, c, 0)),
                   pl.BlockSpec((1, GDN_HEADS, GDN_DK, GDN_DK), lambda b, c: (b, 0, 0, 0))],
        scratch_shapes=[pltpu.VMEM((C + 8, CONV_CH), F32), pltpu.VMEM((C, CONV_CH), F32),
                        pltpu.VMEM((GDN_HEADS, GDN_DK, GDN_DK), F32)],
        compiler_params=_params("parallel", "arbitrary"),
        name="gdn_chunked",
    )(az, az, ba, conv_w, par, gn)


def _gdn_step_kernel(qkv_ref, hist_ref, z_ref, ba_ref, s0_ref, cw_ref, par_ref, gn_ref, o_ref, s_ref):
    acc = qkv_ref[0] * cw_ref[GDN_CONV - 1:GDN_CONV, :]
    for j in range(GDN_CONV - 1):
        acc = acc + hist_ref[0, j:j + 1, :] * cw_ref[j:j + 1, :]
    x = _silu(acc)
    beta_t, g_t = _gdn_gates(ba_ref[0], par_ref[...])
    eg_t = jnp.exp(g_t)

    def head(off, h):
        return x[:, off + h * GDN_DK: off + (h + 1) * GDN_DK]

    rows = []
    for off, scale in ((GDN_QK, 1.0), (0, GDN_DK ** -0.5)):
        for h in range(GDN_HEADS):
            t = head(off, h)
            rows.append(t * lax.rsqrt(jnp.sum(t * t, axis=-1, keepdims=True) + NORM_EPS) * scale)
    ri = lax.broadcasted_iota(jnp.int32, (LANES, LANES), 0)
    tile = jnp.zeros((LANES, LANES), F32)
    for r, t in enumerate(rows):
        tile = jnp.where(ri == r, t, tile)
    cols = tile.T
    gn = gn_ref[...]
    for h in range(GDN_HEADS):
        lane = _HEAD_ORDER.index(h)
        kcol = cols[:, h:h + 1]
        qcol = cols[:, GDN_HEADS + h:GDN_HEADS + h + 1]
        S = s0_ref[0, h] * eg_t[:, 8 + lane:9 + lane]
        kv = jnp.sum(kcol * S, axis=0, keepdims=True)
        delta = beta_t[:, lane:lane + 1] * (head(2 * GDN_QK, h) - kv)
        S = S + kcol * delta
        s_ref[0, h] = S
        o = jnp.sum(qcol * S, axis=0, keepdims=True)
        sl = slice(h * GDN_DK, (h + 1) * GDN_DK)
        o_ref[0, :, sl] = _gated_out(o, z_ref[0, :, sl], gn).astype(o_ref.dtype)


def _gdn_sample(az, ba, hist, s0, conv_w, par, gn):
    N = az.shape[0]
    return pl.pallas_call(
        _gdn_step_kernel,
        out_shape=[jax.ShapeDtypeStruct((N, 1, GDN_QK), BF16),
                   jax.ShapeDtypeStruct((N, GDN_HEADS, GDN_DK, GDN_DK), F32)],
        grid=(N,),
        in_specs=[pl.BlockSpec((1, 1, CONV_CH), lambda b: (b, 0, 0)),
                  pl.BlockSpec((1, GDN_CONV - 1, CONV_CH), lambda b: (b, 0, 0)),
                  pl.BlockSpec((1, 1, GDN_QK), lambda b: (b, 0, CONV_CH // GDN_QK)),
                  pl.BlockSpec((1, 1, LANES), lambda b: (b, 0, 0)),
                  pl.BlockSpec((1, GDN_HEADS, GDN_DK, GDN_DK), lambda b: (b, 0, 0, 0)),
                  pl.BlockSpec((GDN_CONV, CONV_CH), lambda b: (0, 0)),
                  pl.BlockSpec((8, LANES), lambda b: (0, 0)),
                  pl.BlockSpec((1, GDN_DK), lambda b: (0, 0))],
        out_specs=[pl.BlockSpec((1, 1, GDN_QK), lambda b: (b, 0, 0)),
                   pl.BlockSpec((1, GDN_HEADS, GDN_DK, GDN_DK), lambda b: (b, 0, 0, 0))],
        compiler_params=_params("parallel"),
        name="gdn_step",
    )(az, hist, az, ba, s0, conv_w, par, gn)


def _dwa_kernel(q_ref, kc_ref, vc_ref, *rest, has_prev):
    if has_prev:
        kp_ref, vp_ref, o_ref, lse_ref = rest
    else:
        o_ref, lse_ref = rest
    BL = DWA_BLOCK
    n = pl.program_id(2)
    ri = lax.broadcasted_iota(jnp.int32, (BL, BL), 0)
    ci = lax.broadcasted_iota(jnp.int32, (BL, BL), 1)
    cur_ok = ri >= ci
    if has_prev:
        prev_ok = (ci >= ri) & (n > 0)
    lane = lax.broadcasted_iota(jnp.int32, (BL, LANES), 1)
    lse_tile = jnp.zeros((BL, LANES), F32)
    scale = DWA_DH ** -0.5
    for h in range(DWA_HEADS):
        sl = slice(h * DWA_DH, (h + 1) * DWA_DH)
        q = q_ref[0, :, sl]
        sc = jnp.where(cur_ok, _dot_nt(q, kc_ref[0, :, sl].astype(BF16)) * scale, -jnp.inf)
        m = jnp.max(sc, axis=-1, keepdims=True)
        if has_prev:
            sp = jnp.where(prev_ok, _dot_nt(q, kp_ref[0, :, sl].astype(BF16)) * scale, -jnp.inf)
            m = jnp.maximum(m, jnp.max(sp, axis=-1, keepdims=True))
        pc = jnp.exp(sc - m)
        l = jnp.sum(pc, axis=-1, keepdims=True)
        o = _dot(pc.astype(BF16), vc_ref[0, :, sl].astype(BF16))
        if has_prev:
            pp = jnp.exp(sp - m)
            l = l + jnp.sum(pp, axis=-1, keepdims=True)
            o = o + _dot(pp.astype(BF16), vp_ref[0, :, sl].astype(BF16))
        o_ref[0, :, sl] = o / l
        lse_tile = jnp.where(lane == h, m + jnp.log(l), lse_tile)
    lse_ref[0] = lse_tile


def _dwa_prompt(q_all, kv, gi):
    B, T, _ = q_all.shape
    dil = DWA_GROUPS[gi][1]
    ls = T // dil
    nb = ls // DWA_BLOCK
    ng = len(DWA_GROUPS)
    qv = q_all.reshape(B, ls, dil * DWA_WIDTH)
    kvv = kv.reshape(B, ls, dil * 2 * DWA_GW)
    BL = DWA_BLOCK
    has_prev = nb > 1
    in_specs = [pl.BlockSpec((1, BL, DWA_GW), lambda b, r, n: (b, n, r * ng + gi)),
                pl.BlockSpec((1, BL, DWA_GW), lambda b, r, n: (b, n, 2 * r)),
                pl.BlockSpec((1, BL, DWA_GW), lambda b, r, n: (b, n, 2 * r + 1))]
    args = [qv, kvv, kvv]
    if has_prev:
        in_specs += [pl.BlockSpec((1, BL, DWA_GW), lambda b, r, n: (b, jnp.maximum(n - 1, 0), 2 * r)),
                     pl.BlockSpec((1, BL, DWA_GW), lambda b, r, n: (b, jnp.maximum(n - 1, 0), 2 * r + 1))]
        args += [kvv, kvv]
    o, lse = pl.pallas_call(
        functools.partial(_dwa_kernel, has_prev=has_prev),
        out_shape=[jax.ShapeDtypeStruct((B, ls, dil * DWA_GW), F32),
                   jax.ShapeDtypeStruct((B, ls, dil * LANES), F32)],
        grid=(B, dil, nb),
        in_specs=in_specs,
        out_specs=[pl.BlockSpec((1, BL, DWA_GW), lambda b, r, n: (b, n, r)),
                   pl.BlockSpec((1, BL, LANES), lambda b, r, n: (b, n, r))],
        compiler_params=_params("parallel", "parallel", "arbitrary"),
        name=f"dwa_g{gi}",
    )(*args)
    return o.reshape(B, T, DWA_GW), lse.reshape(B, T, LANES)


def _dwa_merge_kernel(o0_ref, o1_ref, o2_ref, l0_ref, l1_ref, l2_ref, out_ref):
    l0, l1, l2 = l0_ref[0], l1_ref[0], l2_ref[0]
    m = jnp.maximum(jnp.maximum(l0, l1), l2)
    e0, e1, e2 = jnp.exp(l0 - m), jnp.exp(l1 - m), jnp.exp(l2 - m)
    den = e0 + e1 + e2
    w0, w1, w2 = e0 / den, e1 / den, e2 / den
    for h in range(DWA_HEADS):
        sl = slice(h * DWA_DH, (h + 1) * DWA_DH)
        c = slice(h, h + 1)
        out_ref[0, :, sl] = (w0[:, c] * o0_ref[0, :, sl] + w1[:, c] * o1_ref[0, :, sl]
                             + w2[:, c] * o2_ref[0, :, sl]).astype(out_ref.dtype)


def _dwa_merge(outs, lses):
    B, T, _ = outs[0].shape
    tm = min(512, T)
    ospec = pl.BlockSpec((1, tm, DWA_GW), lambda b, i: (b, i, 0))
    lspec = pl.BlockSpec((1, tm, LANES), lambda b, i: (b, i, 0))
    return pl.pallas_call(
        _dwa_merge_kernel,
        out_shape=jax.ShapeDtypeStruct((B, T, DWA_GW), BF16),
        grid=(B, T // tm),
        in_specs=[ospec] * 3 + [lspec] * 3,
        out_specs=ospec,
        compiler_params=_params("parallel", "parallel"),
        name="dwa_merge",
    )(*outs, *lses)


def _dwa_step_kernel(q_ref, n0_ref, n1_ref, n2_ref, c0_ref, c1_ref, c2_ref, o_ref):
    new_refs = (n0_ref, n1_ref, n2_ref)
    cache_refs = (c0_ref, c1_ref, c2_ref)
    scale = DWA_DH ** -0.5
    for h in range(DWA_HEADS):
        sl = slice(h * DWA_DH, (h + 1) * DWA_DH)
        vsl = slice(DWA_GW + h * DWA_DH, DWA_GW + (h + 1) * DWA_DH)
        parts = []
        for gi in range(len(DWA_GROUPS)):
            q = q_ref[0, :, gi * DWA_GW + h * DWA_DH: gi * DWA_GW + (h + 1) * DWA_DH]
            s = jnp.sum(cache_refs[gi][0, :, sl] * q, axis=-1, keepdims=True) * scale
            s_new = jnp.sum(new_refs[gi][0, :, sl] * q, axis=-1, keepdims=True) * scale
            m = jnp.maximum(jnp.max(s, axis=0, keepdims=True), s_new)
            p = jnp.exp(s - m)
            p_new = jnp.exp(s_new - m)
            l = jnp.sum(p, axis=0, keepdims=True) + p_new
            acc = jnp.sum(p * cache_refs[gi][0, :, vsl], axis=0, keepdims=True) + p_new * new_refs[gi][0, :, vsl]
            parts.append((m, l, acc))
        mm = jnp.maximum(jnp.maximum(parts[0][0], parts[1][0]), parts[2][0])
        num = jnp.zeros((1, DWA_DH), F32)
        den = jnp.zeros((1, 1), F32)
        for m, l, acc in parts:
            e = jnp.exp(m - mm)
            num = num + e * acc
            den = den + e * l
        o_ref[0, :, sl] = (num / den).astype(o_ref.dtype)


def _dwa_sample(q, kv_new, caches):
    N = q.shape[0]
    span = DWA_BLOCK
    in_specs = [pl.BlockSpec((1, 1, DWA_WIDTH), lambda b: (b, 0, 0))]
    in_specs += [pl.BlockSpec((1, 1, 2 * DWA_GW), lambda b: (b, 0, 0))] * 3
    views = []
    for (win, dil), cache in zip(DWA_GROUPS, caches):
        L = cache.shape[1]
        assert L == win and L // dil == span
        views.append(cache.reshape(N, L // dil, dil * 2 * DWA_GW))
        in_specs.append(pl.BlockSpec((1, span, 2 * DWA_GW), lambda b: (b, 0, 0)))
    return pl.pallas_call(
        _dwa_step_kernel,
        out_shape=jax.ShapeDtypeStruct((N, 1, DWA_GW), BF16),
        grid=(N,),
        in_specs=in_specs,
        out_specs=pl.BlockSpec((1, 1, DWA_GW), lambda b: (b, 0, 0)),
        compiler_params=_params("parallel"),
        name="dwa_step",
    )(q, *kv_new, *views)


def _mix_out_kernel(oa_ref, ob_ref, ga_ref, gb_ref, h_ref, gt_ref, wa_ref, wb_ref, wo_ref, out_ref):
    ta = _dot(oa_ref[0], wa_ref[...])
    tb = _dot(ob_ref[0], wb_ref[...])
    merged = _sigmoid(ga_ref[0]) * ta + _sigmoid(gb_ref[0]) * tb
    out_ref[0] = h_ref[0] + gt_ref[0] * _dot(merged.astype(BF16), wo_ref[...])


def _mix_out(oa, ob, gates, h, mod3, k_gt, wa, wb, wo):
    G, R, _ = h.shape
    rm = mod3.shape[1]
    tm = min(256, R)
    if rm == R:
        gt_spec = pl.BlockSpec((1, tm, D_MODEL), lambda g, i: (g, i, k_gt))
    else:
        gt_spec = pl.BlockSpec((1, 1, D_MODEL), lambda g, i: (g, 0, k_gt))

    def rows(width, col=0):
        return pl.BlockSpec((1, tm, width), lambda g, i: (g, i, col))

    def whole(shape):
        return pl.BlockSpec(shape, lambda g, i: (0, 0))

    return pl.pallas_call(
        _mix_out_kernel,
        out_shape=jax.ShapeDtypeStruct((G, R, D_MODEL), F32),
        grid=(G, R // tm),
        in_specs=[rows(GDN_QK), rows(DWA_GW), rows(D_MODEL, 0), rows(D_MODEL, 1), rows(D_MODEL), gt_spec,
                  whole(wa.shape), whole(wb.shape), whole(wo.shape)],
        out_specs=rows(D_MODEL),
        compiler_params=_params("parallel", "parallel"),
        name="mix_out",
    )(oa, ob, gates, gates, h, mod3, wa, wb, wo)


def _prep_weights(w_in, conv_w, a_log, dt_bias, gdn_norm, w_proj_a, w_proj_b, w_out,
                  w_ffn1_up, w_ffn1_down, w_ffn2_up, w_ffn2_down):
    w = w_in[0]
    order = np.array(_HEAD_ORDER)
    ba_cols = np.concatenate([_C_B + order, _C_A + order])
    par = jnp.zeros((8, LANES), F32)
    par = par.at[0, 8:16].set(a_log[0][order]).at[1, 8:16].set(dt_bias[0][order])
    return dict(
        az=w[:, :_C_B].astype(BF16),
        ba=jnp.pad(w[:, ba_cols], ((0, 0), (0, LANES - 2 * GDN_HEADS))).astype(BF16),
        q=w[:, _C_Q:_C_K].astype(BF16),
        kv=[jnp.concatenate([w[:, _C_K + g * DWA_GW:_C_K + (g + 1) * DWA_GW],
                             w[:, _C_V + g * DWA_GW:_C_V + (g + 1) * DWA_GW]], axis=1).astype(BF16)
            for g in range(len(DWA_GROUPS))],
        gates=w[:, _C_GA:].astype(BF16),
        conv=conv_w[0], par=par, gn=gdn_norm,
        wa=w_proj_a[0].astype(BF16), wb=w_proj_b[0].astype(BF16), wo=w_out[0].astype(BF16),
        up1=w_ffn1_up[0].astype(BF16), down1=w_ffn1_down[0].astype(BF16),
        up2=w_ffn2_up[0].astype(BF16), down2=w_ffn2_down[0].astype(BF16),
    )


def _trunk(x3, mod3, pos, W, norms, *, sample_state=None):
    G, R, _ = x3.shape
    h1, xm = _ffn(x3, mod3, 0, norms["ffn1"], W["up1"], W["down1"], next_k=3, next_gain=norms["mix"])
    tables = _rope_tables(pos)
    az = _proj(xm, W["az"], F32, name="proj_az")
    ba = _proj(xm, W["ba"], F32, tn=LANES, name="proj_ba")
    gates = _proj(xm, W["gates"], F32, name="proj_gates")
    kvs = [_proj(xm, W["kv"][g], F32, rope="first", tables=tables, name=f"proj_kv{g}")
           for g in range(len(DWA_GROUPS))]
    if sample_state is None:
        q = _proj(xm, W["q"], BF16, rope="all", tables=tables, name="proj_q")
        o_a, s_new = _gdn_prompt(az, ba, W["conv"], W["par"], W["gn"])
        parts = [_dwa_prompt(q, kvs[g], g) for g in range(len(DWA_GROUPS))]
        o_b = _dwa_merge([p[0] for p in parts], [p[1] for p in parts])
    else:
        hist, s0, caches = sample_state
        q = _proj(xm, W["q"], F32, rope="all", tables=tables, name="proj_q")
        o_a, s_new = _gdn_sample(az.reshape(R, 1, -1), ba.reshape(R, 1, LANES), hist, s0,
                                 W["conv"], W["par"], W["gn"])
        o_b = _dwa_sample(q.reshape(R, 1, -1), [kv.reshape(R, 1, -1) for kv in kvs], caches)
        o_a = o_a.reshape(1, R, -1)
        o_b = o_b.reshape(1, R, -1)
    h2 = _mix_out(o_a, o_b, gates, h1, mod3, 5, W["wa"], W["wb"], W["wo"])
    y = _ffn(h2, mod3, 6, norms["ffn2"], W["up2"], W["down2"], final_gain=norms["final"])
    return y, az, s_new, kvs


def kernel(x_prompt, x_sample, state_conv, state_delta, cache_kv_w128, cache_kv_w512, cache_kv_w2048, c_prompt, c_sample, w_ada, b_ada, norm_ffn1, w_ffn1_up, w_ffn1_down, norm_mix, w_in, conv_w, a_log, dt_bias, gdn_norm, w_proj_a, w_proj_b, w_out, norm_ffn2, w_ffn2_up, w_ffn2_down, norm_final):
    B, T, _ = x_prompt.shape
    N, S, _ = x_sample.shape
    assert S == 1 and T % (DWA_BLOCK * DWA_GROUPS[-1][1]) == 0
    W = _prep_weights(w_in, conv_w, a_log, dt_bias, gdn_norm, w_proj_a, w_proj_b, w_out,
                      w_ffn1_up, w_ffn1_down, w_ffn2_up, w_ffn2_down)
    norms = dict(ffn1=norm_ffn1[0], mix=norm_mix[0], ffn2=norm_ffn2[0], final=norm_final)

    mod = _ada(jnp.concatenate([c_prompt, c_sample], axis=0), w_ada[0], b_ada)
    mod_p = mod[:B].reshape(B, 1, N_MOD * D_MODEL)
    mod_s = mod[B:].reshape(1, N, N_MOD * D_MODEL)

    y_p, az_p, s_p, kv_p = _trunk(x_prompt, mod_p, jnp.arange(T, dtype=jnp.int32), W, norms)
    caches = (cache_kv_w128[0], cache_kv_w512[0], cache_kv_w2048[0])
    y_s, az_s, s_s, kv_s = _trunk(x_sample.reshape(1, N, D_MODEL), mod_s,
                                  jnp.full((N,), PAST_LEN, dtype=jnp.int32), W, norms,
                                  sample_state=(state_conv[0], state_delta[0], caches))

    keep = GDN_CONV - 1
    conv_p = az_p[:, T - keep:, :CONV_CH][None]
    conv_s = jnp.concatenate([state_conv[0], az_s.reshape(N, 1, -1)[:, :, :CONV_CH]], axis=1)[:, -keep:][None]
    kv_out_p = []
    for (win, _), kv in zip(DWA_GROUPS, kv_p):
        k = min(win, T)
        kv_out_p.append(kv[:, T - k:].reshape(1, B, k, 2, DWA_HEADS, DWA_DH))
    kv_out_s = [kv.reshape(1, N, 1, 2, DWA_HEADS, DWA_DH) for kv in kv_s]
    return (y_p, y_s.reshape(N, 1, D_MODEL), conv_p, s_p[None], *kv_out_p,
            conv_s, s_s[None], *kv_out_s)
```

```python
import functools

import jax
import jax.numpy as jnp
import numpy as np
from jax import lax
from jax.experimental import pallas as pl
from jax.experimental.pallas import tpu as pltpu

F32 = jnp.float32
BF16 = jnp.bfloat16

D_MODEL = 2048
D_FF = 5632
N_MOD = 9
NORM_EPS = 1e-6
PAST_LEN = 16384

GDN_HEADS = 8
GDN_DK = 128
GDN_CONV = 4
GDN_CHUNK = 64
GDN_QK = GDN_HEADS * GDN_DK
CONV_CH = 3 * GDN_QK

DWA_GROUPS = ((128, 1), (512, 4), (2048, 16))
DWA_HEADS = 4
DWA_DH = 128
DWA_GW = DWA_HEADS * DWA_DH
DWA_WIDTH = len(DWA_GROUPS) * DWA_GW
DWA_BLOCK = 128
ROPE_THETA = 10000.0

LANES = 128
VMEM_LIMIT = 56 * 1024 * 1024

_C_Z = CONV_CH
_C_B = _C_Z + GDN_QK
_C_A = _C_B + GDN_HEADS
_C_Q = _C_A + GDN_HEADS
_C_K = _C_Q + DWA_WIDTH
_C_V = _C_K + DWA_WIDTH
_C_GA = _C_V + DWA_WIDTH
_C_GB = _C_GA + D_MODEL

_HEAD_ORDER = (0, 2, 4, 6, 1, 3, 5, 7)
_N_PAIRS = GDN_HEADS // 2


def _sigmoid(x):
    return 1.0 / (1.0 + jnp.exp(-x))


def _silu(x):
    return x * _sigmoid(x)


def _dot(a, b):
    return jnp.dot(a, b, preferred_element_type=F32)


def _dot_nt(a, b):
    return lax.dot_general(a, b, (((1,), (1,)), ((), ())), preferred_element_type=F32)


def _params(*sem):
    return pltpu.CompilerParams(dimension_semantics=sem, vmem_limit_bytes=VMEM_LIMIT)


def _ada_kernel(c_ref, w_ref, b_ref, o_ref):
    a = _silu(c_ref[...]).astype(BF16)
    o_ref[...] = _dot(a, w_ref[...].astype(BF16)) + b_ref[...]


def _ada(c, w, b):
    m, n = c.shape[0], w.shape[1]
    tn = 1024
    return pl.pallas_call(
        _ada_kernel,
        out_shape=jax.ShapeDtypeStruct((m, n), F32),
        grid=(n // tn,),
        in_specs=[pl.BlockSpec((m, D_MODEL), lambda j: (0, 0)),
                  pl.BlockSpec((D_MODEL, tn), lambda j: (0, j)),
                  pl.BlockSpec((1, tn), lambda j: (0, j))],
        out_specs=pl.BlockSpec((m, tn), lambda j: (0, j)),
        compiler_params=_params("arbitrary"),
        name="ada_mod",
    )(c, w, b)


def _rms(x):
    return x * lax.rsqrt(jnp.mean(x * x, axis=-1, keepdims=True) + NORM_EPS)


def _ffn_kernel(x_ref, sh_ref, sc_ref, gt_ref, gain_ref, wg_ref, wu_ref, wd_ref, *rest, next_mod, final_norm):
    if next_mod:
        nsh_ref, nsc_ref, ngain_ref, out_ref, nxt_ref, xm_sc, acc_sc = rest
    elif final_norm:
        ngain_ref, out_ref, xm_sc, acc_sc = rest
    else:
        out_ref, xm_sc, acc_sc = rest
    j = pl.program_id(2)

    @pl.when(j == 0)
    def _():
        xn = _rms(x_ref[0]) * gain_ref[...]
        xm_sc[...] = (xn * (1.0 + sc_ref[0]) + sh_ref[0]).astype(BF16)
        acc_sc[...] = jnp.zeros_like(acc_sc)

    xm = xm_sc[...]
    g = _dot(xm, wg_ref[...])
    u = _dot(xm, wu_ref[...])
    acc_sc[...] += _dot((_silu(g) * u).astype(BF16), wd_ref[...])

    @pl.when(j == pl.num_programs(2) - 1)
    def _():
        h = x_ref[0] + 0.5 * gt_ref[0] * acc_sc[...]
        if final_norm:
            out_ref[0] = _rms(h) * ngain_ref[...]
        else:
            out_ref[0] = h
        if next_mod:
            hn = _rms(h) * ngain_ref[...]
            nxt_ref[0] = (hn * (1.0 + nsc_ref[0]) + nsh_ref[0]).astype(BF16)


def _ffn(x3, mod3, k_sh, gain, w_up, w_down, *, next_k=None, next_gain=None, final_gain=None):
    G, R, _ = x3.shape
    rm = mod3.shape[1]
    tm = min(512, R)
    tf = 512
    nff = D_FF // tf

    def mod_spec(k):
        if rm == R:
            return pl.BlockSpec((1, tm, D_MODEL), lambda g, i, j: (g, i, k))
        return pl.BlockSpec((1, 1, D_MODEL), lambda g, i, j: (g, 0, k))

    vec = pl.BlockSpec((1, D_MODEL), lambda g, i, j: (0, 0))
    xspec = pl.BlockSpec((1, tm, D_MODEL), lambda g, i, j: (g, i, 0))
    in_specs = [xspec, mod_spec(k_sh), mod_spec(k_sh + 1), mod_spec(k_sh + 2), vec,
                pl.BlockSpec((D_MODEL, tf), lambda g, i, j: (0, j)),
                pl.BlockSpec((D_MODEL, tf), lambda g, i, j: (0, j + nff)),
                pl.BlockSpec((tf, D_MODEL), lambda g, i, j: (j, 0))]
    args = [x3, mod3, mod3, mod3, gain.reshape(1, D_MODEL), w_up, w_up, w_down]
    out_shape = [jax.ShapeDtypeStruct((G, R, D_MODEL), F32)]
    out_specs = [xspec]
    next_mod = next_k is not None
    if next_mod:
        in_specs += [mod_spec(next_k), mod_spec(next_k + 1), vec]
        args += [mod3, mod3, next_gain.reshape(1, D_MODEL)]
        out_shape.append(jax.ShapeDtypeStruct((G, R, D_MODEL), BF16))
        out_specs.append(xspec)
    elif final_gain is not None:
        in_specs.append(vec)
        args.append(final_gain.reshape(1, D_MODEL))
    res = pl.pallas_call(
        functools.partial(_ffn_kernel, next_mod=next_mod, final_norm=final_gain is not None),
        out_shape=out_shape,
        grid=(G, R // tm, nff),
        in_specs=in_specs,
        out_specs=out_specs,
        scratch_shapes=[pltpu.VMEM((tm, D_MODEL), BF16), pltpu.VMEM((tm, D_MODEL), F32)],
        compiler_params=_params("parallel", "parallel", "arbitrary"),
        name="ffn_next" if next_mod else "ffn_final",
    )(*args)
    return res if next_mod else res[0]


def _rope_tile(x, cosf, sinf):
    return x * cosf + pltpu.roll(x, DWA_DH // 2, axis=1) * sinf


def _proj_kernel(a_ref, w_ref, *rest, rope, tn):
    if rope == "none":
        (o_ref,) = rest
        o_ref[0] = _dot(a_ref[0], w_ref[...]).astype(o_ref.dtype)
        return
    cos_ref, sin_ref, o_ref = rest
    acc = _dot(a_ref[0], w_ref[...])

    def roped():
        cosf, sinf = cos_ref[...], sin_ref[...]
        for h in range(tn // DWA_DH):
            sl = slice(h * DWA_DH, (h + 1) * DWA_DH)
            o_ref[0, :, sl] = _rope_tile(acc[:, sl], cosf, sinf).astype(o_ref.dtype)

    if rope == "all":
        roped()
    else:
        j = pl.program_id(2)
        pl.when(j == 0)(roped)

        @pl.when(j != 0)
        def _():
            o_ref[0] = acc.astype(o_ref.dtype)


def _proj(xm3, w, out_dtype, *, rope="none", tables=None, tn=512, name="proj"):
    G, R, _ = xm3.shape
    n = w.shape[1]
    tm = min(1024, R)
    in_specs = [pl.BlockSpec((1, tm, D_MODEL), lambda g, i, j: (g, i, 0)),
                pl.BlockSpec((D_MODEL, tn), lambda g, i, j: (0, j))]
    args = [xm3, w]
    if rope != "none":
        tab = pl.BlockSpec((tm, DWA_DH), lambda g, i, j: (i, 0))
        in_specs += [tab, tab]
        args += list(tables)
    return pl.pallas_call(
        functools.partial(_proj_kernel, rope=rope, tn=tn),
        out_shape=jax.ShapeDtypeStruct((G, R, n), out_dtype),
        grid=(G, R // tm, n // tn),
        in_specs=in_specs,
        out_specs=pl.BlockSpec((1, tm, tn), lambda g, i, j: (g, i, j)),
        compiler_params=_params("parallel", "parallel", "arbitrary"),
        name=name,
    )(*args)


def _rope_tables(pos):
    half = DWA_DH // 2
    inv = jnp.power(ROPE_THETA, -jnp.arange(half, dtype=F32) / half)
    ang = pos.astype(F32)[:, None] * inv[None, :]
    cos, sin = jnp.cos(ang), jnp.sin(ang)
    return jnp.concatenate([cos, cos], axis=-1), jnp.concatenate([-sin, sin], axis=-1)


def _gdn_gates(ba, par):
    beta = _sigmoid(ba)
    x = ba + par[1:2]
    softplus = jnp.maximum(x, 0.0) + jnp.log1p(jnp.exp(-jnp.abs(x)))
    return beta, -jnp.exp(par[0:1]) * softplus


def _gated_out(o, z, gn):
    return (_rms(o) * gn) * _silu(z)


def _gdn_chunk_kernel(qkv_ref, z_ref, ba_ref, cw_ref, par_ref, gn_ref, o_ref, s_ref, buf_sc, x_sc, S_sc):
    C = GDN_CHUNK
    c = pl.program_id(1)

    @pl.when(c == 0)
    def _():
        buf_sc[0:8, :] = jnp.zeros((8, CONV_CH), F32)
        S_sc[...] = jnp.zeros_like(S_sc)

    buf_sc[8:8 + C, :] = qkv_ref[0]
    acc = buf_sc[5:5 + C, :] * cw_ref[0:1, :]
    for j in range(1, GDN_CONV):
        acc = acc + buf_sc[5 + j:5 + j + C, :] * cw_ref[j:j + 1, :]
    x_sc[...] = _silu(acc)
    buf_sc[0:8, :] = buf_sc[C:C + 8, :]

    beta_t, g_t = _gdn_gates(ba_ref[0], par_ref[...])
    row = lax.broadcasted_iota(jnp.int32, (C, LANES), 0)
    G = g_t
    s = 1
    while s < C:
        G = G + jnp.where(row >= s, pltpu.roll(G, s, axis=0), 0.0)
        s *= 2
    shift = LANES - _N_PAIRS
    Gs = jnp.concatenate([G, pltpu.roll(G, shift, axis=1)], axis=0)
    Bs = jnp.concatenate([beta_t, pltpu.roll(beta_t, shift, axis=1)], axis=0)
    GT = Gs.T
    r2 = lax.broadcasted_iota(jnp.int32, (2 * C, LANES), 0)
    Glast = jnp.where(r2 < C, Gs[C - 1:C, :], Gs[2 * C - 1:2 * C, :])
    eG = jnp.exp(Gs)
    eGl = jnp.exp(Glast - Gs)
    egl = jnp.exp(Glast)

    ii = lax.broadcasted_iota(jnp.int32, (2 * C, 2 * C), 0)
    jj = lax.broadcasted_iota(jnp.int32, (2 * C, 2 * C), 1)
    same = (ii // C) == (jj // C)
    strict = same & (ii > jj)
    diag = ii == jj
    blk = (ii // 16) == (jj // 16)
    top = r2 < C
    gn = gn_ref[...]

    P = range(_N_PAIRS)
    C2 = 2 * C

    def pair(p, off):
        a = x_sc[:, off + (2 * p) * GDN_DK: off + (2 * p + 1) * GDN_DK]
        b = x_sc[:, off + (2 * p + 1) * GDN_DK: off + (2 * p + 2) * GDN_DK]
        return jnp.concatenate([a, b], axis=0)

    def l2(t):
        return t * lax.rsqrt(jnp.sum(t * t, axis=-1, keepdims=True) + NORM_EPS)

    def col(t, p, base=0):
        return t[:, base + p:base + p + 1]

    q2 = [l2(pair(p, 0)) * (GDN_DK ** -0.5) for p in P]
    k2 = [l2(pair(p, GDN_QK)) for p in P]
    kb = [k2[p] * col(Bs, p) for p in P]
    vb = [pair(p, 2 * GDN_QK) * col(Bs, p) for p in P]
    dec = [jnp.where(strict, jnp.exp(jnp.where(strict, col(Gs, p, 8) - GT[8 + p:9 + p, :], 0.0)), 0.0) for p in P]
    kq = [_dot_nt(jnp.concatenate([kb[p], q2[p]], axis=0).astype(BF16), k2[p].astype(BF16)) for p in P]
    A = [kq[p][:C2] * dec[p] for p in P]
    qk = [(kq[p][C2:] * (dec[p] + jnp.where(diag, 1.0, 0.0))).astype(BF16) for p in P]

    Dg = [jnp.where(blk, A[p], 0.0) for p in P]
    E = [A[p] - Dg[p] for p in P]
    Q = [-Dg[p] for p in P]
    Dgb = [Dg[p].astype(BF16) for p in P]
    Dp = [_dot(Dgb[p], Dgb[p]) for p in P]
    for _ in range(2):
        Dpb = [Dp[p].astype(BF16) for p in P]
        st = [_dot(jnp.concatenate([Dpb[p], Q[p].astype(BF16)], axis=0), Dpb[p]) for p in P]
        Q = [Q[p] + Dp[p] + st[p][C2:] for p in P]
        Dp = [st[p][:C2] for p in P]
    Q = [Q[p] + Dp[p] + _dot(Q[p].astype(BF16), Dp[p].astype(BF16)) for p in P]
    rhs = [jnp.concatenate([kb[p] * col(eG, p, 8), vb[p]], axis=1) for p in P]
    er = [_dot(Q[p].astype(BF16), jnp.concatenate([E[p], rhs[p]], axis=1).astype(BF16)) for p in P]
    N = [E[p] + er[p][:, :C2] for p in P]
    y = [rhs[p] + er[p][:, C2:] for p in P]
    Nb = [N[p].astype(BF16) for p in P]
    ny = [_dot(Nb[p], jnp.concatenate([N[p], y[p]], axis=1).astype(BF16)) for p in P]
    zz = [y[p] - ny[p][:, C2:] for p in P]
    sol = [zz[p] + _dot(ny[p][:, :C2].astype(BF16), zz[p].astype(BF16)) for p in P]

    wq = [jnp.concatenate([sol[p][:, :GDN_DK], q2[p] * col(eG, p, 8)], axis=0).astype(BF16) for p in P]
    S_old = [S_sc[h] for h in range(GDN_HEADS)]
    rS = [_dot(wq[h // 2], S_old[h].astype(BF16)) for h in range(GDN_HEADS)]
    vn = [(sol[p][:, GDN_DK:] - jnp.where(top, rS[2 * p][:C2], rS[2 * p + 1][:C2])).astype(BF16) for p in P]
    kdT = [(k2[p] * col(eGl, p, 8)).T.astype(BF16) for p in P]
    zero = jnp.zeros((C2, GDN_DK), BF16)
    for p in P:
        o2 = jnp.where(top, rS[2 * p][C2:], rS[2 * p + 1][C2:]) + _dot(qk[p], vn[p])
        S_sc[2 * p] = S_old[2 * p] * egl[0:1, 8 + p:9 + p] + _dot(kdT[p], jnp.where(top, vn[p], zero))
        S_sc[2 * p + 1] = S_old[2 * p + 1] * egl[C:C + 1, 8 + p:9 + p] + _dot(kdT[p], jnp.where(top, zero, vn[p]))
        for e in range(2):
            h = 2 * p + e
            sl = slice(h * GDN_DK, (h + 1) * GDN_DK)
            o_ref[0, :, sl] = _gated_out(o2[e * C:(e + 1) * C], z_ref[0, :, sl], gn).astype(o_ref.dtype)

    @pl.when(c == pl.num_programs(1) - 1)
    def _():
        s_ref[0] = S_sc[...]


def _gdn_prompt(az, ba, conv_w, par, gn):
    B, T, _ = az.shape
    C = GDN_CHUNK
    return pl.pallas_call(
        _gdn_chunk_kernel,
        out_shape=[jax.ShapeDtypeStruct((B, T, GDN_QK), BF16),
                   jax.ShapeDtypeStruct((B, GDN_HEADS, GDN_DK, GDN_DK), F32)],
        grid=(B, T // C),
        in_specs=[pl.BlockSpec((1, C, CONV_CH), lambda b, c: (b, c, 0)),
                  pl.BlockSpec((1, C, GDN_QK), lambda b, c: (b, c, CONV_CH // GDN_QK)),
                  pl.BlockSpec((1, C, LANES), lambda b, c: (b, c, 0)),
                  pl.BlockSpec((GDN_CONV, CONV_CH), lambda b, c: (0, 0)),
                  pl.BlockSpec((8, LANES), lambda b, c: (0, 0)),
                  pl.BlockSpec((1, GDN_DK), lambda b, c: (0, 0))],
        out_specs=[pl.BlockSpec((1, C, GDN_QK), lambda b, c: (b, c, 0)),
                   pl.BlockSpec((1, GDN_HEADS, GDN_DK, GDN_DK), lambda b, c: (b, 0, 0, 0))],
        scratch_shapes=[pltpu.VMEM((C + 8, CONV_CH), F32), pltpu.VMEM((C, CONV_CH), F32),
                        pltpu.VMEM((GDN_HEADS, GDN_DK, GDN_DK), F32)],
        compiler_params=_params("parallel", "arbitrary"),
        name="gdn_chunked",
    )(az, az, ba, conv_w, par, gn)


def _gdn_step_kernel(qkv_ref, hist_ref, z_ref, ba_ref, s0_ref, cw_ref, par_ref, gn_ref, o_ref, s_ref):
    acc = qkv_ref[0] * cw_ref[GDN_CONV - 1:GDN_CONV, :]
    for j in range(GDN_CONV - 1):
        acc = acc + hist_ref[0, j:j + 1, :] * cw_ref[j:j + 1, :]
    x = _silu(acc)
    beta_t, g_t = _gdn_gates(ba_ref[0], par_ref[...])
    eg_t = jnp.exp(g_t)

    def head(off, h):
        return x[:, off + h * GDN_DK: off + (h + 1) * GDN_DK]

    rows = []
    for off, scale in ((GDN_QK, 1.0), (0, GDN_DK ** -0.5)):
        for h in range(GDN_HEADS):
            t = head(off, h)
            rows.append(t * lax.rsqrt(jnp.sum(t * t, axis=-1, keepdims=True) + NORM_EPS) * scale)
    ri = lax.broadcasted_iota(jnp.int32, (LANES, LANES), 0)
    tile = jnp.zeros((LANES, LANES), F32)
    for r, t in enumerate(rows):
        tile = jnp.where(ri == r, t, tile)
    cols = tile.T
    gn = gn_ref[...]
    for h in range(GDN_HEADS):
        lane = _HEAD_ORDER.index(h)
        kcol = cols[:, h:h + 1]
        qcol = cols[:, GDN_HEADS + h:GDN_HEADS + h + 1]
        S = s0_ref[0, h] * eg_t[:, 8 + lane:9 + lane]
        kv = jnp.sum(kcol * S, axis=0, keepdims=True)
        delta = beta_t[:, lane:lane + 1] * (head(2 * GDN_QK, h) - kv)
        S = S + kcol * delta
        s_ref[0, h] = S
        o = jnp.sum(qcol * S, axis=0, keepdims=True)
        sl = slice(h * GDN_DK, (h + 1) * GDN_DK)
        o_ref[0, :, sl] = _gated_out(o, z_ref[0, :, sl], gn).astype(o_ref.dtype)


def _gdn_sample(az, ba, hist, s0, conv_w, par, gn):
    N = az.shape[0]
    return pl.pallas_call(
        _gdn_step_kernel,
        out_shape=[jax.ShapeDtypeStruct((N, 1, GDN_QK), BF16),
                   jax.ShapeDtypeStruct((N, GDN_HEADS, GDN_DK, GDN_DK), F32)],
        grid=(N,),
        in_specs=[pl.BlockSpec((1, 1, CONV_CH), lambda b: (b, 0, 0)),
                  pl.BlockSpec((1, GDN_CONV - 1, CONV_CH), lambda b: (b, 0, 0)),
                  pl.BlockSpec((1, 1, GDN_QK), lambda b: (b, 0, CONV_CH // GDN_QK)),
                  pl.BlockSpec((1, 1, LANES), lambda b: (b, 0, 0)),
                  pl.BlockSpec((1, GDN_HEADS, GDN_DK, GDN_DK), lambda b: (b, 0, 0, 0)),
                  pl.BlockSpec((GDN_CONV, CONV_CH), lambda b: (0, 0)),
                  pl.BlockSpec((8, LANES), lambda b: (0, 0)),
                  pl.BlockSpec((1, GDN_DK), lambda b: (0, 0))],
        out_specs=[pl.BlockSpec((1, 1, GDN_QK), lambda b: (b, 0, 0)),
                   pl.BlockSpec((1, GDN_HEADS, GDN_DK, GDN_DK), lambda b: (b, 0, 0, 0))],
        compiler_params=_params("parallel"),
        name="gdn_step",
    )(az, hist, az, ba, s0, conv_w, par, gn)


def _dwa_block(q, kc, vc, kp, vp, prev_on, masks):
    cur_ok, prev_ok = masks
    scale = DWA_DH ** -0.5
    qb = q.astype(BF16)
    sc = jnp.where(cur_ok, _dot_nt(qb, kc.astype(BF16)) * scale, -jnp.inf)
    m = jnp.max(sc, axis=-1, keepdims=True)
    if kp is not None:
        sp = jnp.where(prev_ok & prev_on, _dot_nt(qb, kp.astype(BF16)) * scale, -jnp.inf)
        m = jnp.maximum(m, jnp.max(sp, axis=-1, keepdims=True))
    pc = jnp.exp(sc - m)
    l = jnp.sum(pc, axis=-1, keepdims=True)
    acc = _dot(pc.astype(BF16), vc.astype(BF16))
    if kp is not None:
        pp = jnp.exp(sp - m)
        l = l + jnp.sum(pp, axis=-1, keepdims=True)
        acc = acc + _dot(pp.astype(BF16), vp.astype(BF16))
    return m, l, acc


def _dwa_kernel(q0_ref, q1_ref, q2_ref, k0_ref, v0_ref, k1_ref, v1_ref, k2_ref, v2_ref, o_ref, m_sc, l_sc, acc_sc):
    BL = DWA_BLOCK
    T = o_ref.shape[1]
    ri = lax.broadcasted_iota(jnp.int32, (BL, BL), 0)
    ci = lax.broadcasted_iota(jnp.int32, (BL, BL), 1)
    masks = (ri >= ci, ci >= ri)
    q_refs, k_refs, v_refs = (q0_ref, q1_ref, q2_ref), (k0_ref, k1_ref, k2_ref), (v0_ref, v1_ref, v2_ref)

    for gi, (_, dil) in enumerate(DWA_GROUPS):
        q_ref, k_ref, v_ref = q_refs[gi], k_refs[gi], v_refs[gi]
        nb = T // dil // BL

        for r in range(dil):
            def body(n, carry, r=r, dil=dil, q_ref=q_ref, k_ref=k_ref, v_ref=v_ref, nb=nb, first=gi == 0):
                def rows(blk):
                    start = blk * (BL * dil) + r
                    if dil == 1:
                        return pl.ds(pl.multiple_of(start, BL), BL)
                    return pl.ds(start, BL, stride=dil)

                cur = rows(n)
                if nb > 1:
                    prev = rows(jnp.maximum(n - 1, 0))
                    kp, vp = k_ref[0, prev, :], v_ref[0, prev, :]
                else:
                    kp = vp = None
                m, l, acc = _dwa_block(q_ref[0, cur, :], k_ref[0, cur, :], v_ref[0, cur, :], kp, vp, n > 0, masks)
                if first:
                    m_sc[cur, :] = jnp.broadcast_to(m, (BL, LANES))
                    l_sc[cur, :] = jnp.broadcast_to(l, (BL, LANES))
                    acc_sc[cur, :] = acc
                else:
                    m_old = m_sc[cur, :]
                    m_new = jnp.maximum(m_old, m)
                    a = jnp.exp(m_old - m_new)
                    b = jnp.exp(m - m_new)
                    m_sc[cur, :] = m_new
                    l_sc[cur, :] = a * l_sc[cur, :] + b * l
                    acc_sc[cur, :] = a * acc_sc[cur, :] + b * acc
                return carry

            if nb == 1:
                body(0, 0)
            else:
                lax.fori_loop(0, nb, body, 0)

    o_ref[0] = (acc_sc[...] / l_sc[...]).astype(o_ref.dtype)


def _dwa_prompt(q, kvs):
    B, T, _ = q.shape

    def col(c):
        return pl.BlockSpec((1, T, DWA_DH), lambda b, h: (b, 0, c(h)))

    ng = len(DWA_GROUPS)
    in_specs = [col(lambda h, g=g: g * DWA_HEADS + h) for g in range(ng)]
    args = [q] * ng
    for g in range(ng):
        in_specs += [col(lambda h: h), col(lambda h: DWA_HEADS + h)]
        args += [kvs[g], kvs[g]]
    return pl.pallas_call(
        _dwa_kernel,
        out_shape=jax.ShapeDtypeStruct((B, T, DWA_GW), BF16),
        grid=(B, DWA_HEADS),
        in_specs=in_specs,
        out_specs=col(lambda h: h),
        scratch_shapes=[pltpu.VMEM((T, LANES), F32), pltpu.VMEM((T, LANES), F32), pltpu.VMEM((T, DWA_DH), F32)],
        compiler_params=_params("parallel", "parallel"),
        name="dwa_prompt",
    )(*args)


def _dwa_step_kernel(q_ref, n0_ref, n1_ref, n2_ref, c0_ref, c1_ref, c2_ref, o_ref):
    new_refs = (n0_ref, n1_ref, n2_ref)
    cache_refs = (c0_ref, c1_ref, c2_ref)
    scale = DWA_DH ** -0.5
    parts = []
    for gi in range(len(DWA_GROUPS)):
        q = q_ref[0, gi * DWA_HEADS:(gi + 1) * DWA_HEADS, :]
        k = cache_refs[gi][0, :, 0, 0]
        v = cache_refs[gi][0, :, 0, 1]
        s = jnp.sum(k * q[None], axis=-1, keepdims=True) * scale
        s_new = jnp.sum(new_refs[gi][0, 0] * q, axis=-1, keepdims=True) * scale
        m = jnp.maximum(jnp.max(s, axis=0), s_new)
        p = jnp.exp(s - m[None])
        p_new = jnp.exp(s_new - m)
        l = jnp.sum(p, axis=0) + p_new
        acc = jnp.sum(p * v, axis=0) + p_new * new_refs[gi][0, 1]
        parts.append((m, l, acc))
    mm = jnp.maximum(jnp.maximum(parts[0][0], parts[1][0]), parts[2][0])
    num = jnp.zeros((DWA_HEADS, DWA_DH), F32)
    den = jnp.zeros((DWA_HEADS, 1), F32)
    for m, l, acc in parts:
        e = jnp.exp(m - mm)
        num = num + e * acc
        den = den + e * l
    o_ref[0] = num / den


def _dwa_sample(q, kv_new, caches):
    N = q.shape[0]
    span = DWA_BLOCK
    in_specs = [pl.BlockSpec((1, len(DWA_GROUPS) * DWA_HEADS, DWA_DH), lambda b: (b, 0, 0))]
    in_specs += [pl.BlockSpec((1, 2, DWA_HEADS, DWA_DH), lambda b: (b, 0, 0, 0))] * 3
    views = []
    for (win, dil), cache in zip(DWA_GROUPS, caches):
        L = cache.shape[1]
        assert L == win and L // dil == span
        views.append(cache.reshape(N, span, dil, 2, DWA_HEADS, DWA_DH))
        in_specs.append(pl.BlockSpec((1, span, 1, 2, DWA_HEADS, DWA_DH), lambda b: (b, 0, 0, 0, 0, 0)))
    return pl.pallas_call(
        _dwa_step_kernel,
        out_shape=jax.ShapeDtypeStruct((N, DWA_HEADS, DWA_DH), F32),
        grid=(N,),
        in_specs=in_specs,
        out_specs=pl.BlockSpec((1, DWA_HEADS, DWA_DH), lambda b: (b, 0, 0)),
        compiler_params=_params("parallel"),
        name="dwa_step",
    )(q, *kv_new, *views)


def _mix_out_kernel(oa_ref, ob_ref, ga_ref, gb_ref, h_ref, gt_ref, wa_ref, wb_ref, wo_ref, out_ref):
    ta = _dot(oa_ref[0], wa_ref[...])
    tb = _dot(ob_ref[0], wb_ref[...])
    merged = _sigmoid(ga_ref[0]) * ta + _sigmoid(gb_ref[0]) * tb
    out_ref[0] = h_ref[0] + gt_ref[0] * _dot(merged.astype(BF16), wo_ref[...])


def _mix_out(oa, ob, gates, h, mod3, k_gt, wa, wb, wo):
    G, R, _ = h.shape
    rm = mod3.shape[1]
    tm = min(256, R)
    if rm == R:
        gt_spec = pl.BlockSpec((1, tm, D_MODEL), lambda g, i: (g, i, k_gt))
    else:
        gt_spec = pl.BlockSpec((1, 1, D_MODEL), lambda g, i: (g, 0, k_gt))

    def rows(width, col=0):
        return pl.BlockSpec((1, tm, width), lambda g, i: (g, i, col))

    def whole(shape):
        return pl.BlockSpec(shape, lambda g, i: (0, 0))

    return pl.pallas_call(
        _mix_out_kernel,
        out_shape=jax.ShapeDtypeStruct((G, R, D_MODEL), F32),
        grid=(G, R // tm),
        in_specs=[rows(GDN_QK), rows(DWA_GW), rows(D_MODEL, 0), rows(D_MODEL, 1), rows(D_MODEL), gt_spec,
                  whole(wa.shape), whole(wb.shape), whole(wo.shape)],
        out_specs=rows(D_MODEL),
        compiler_params=_params("parallel", "parallel"),
        name="mix_out",
    )(oa, ob, gates, gates, h, mod3, wa, wb, wo)


def _prep_weights(w_in, conv_w, a_log, dt_bias, gdn_norm, w_proj_a, w_proj_b, w_out,
                  w_ffn1_up, w_ffn1_down, w_ffn2_up, w_ffn2_down):
    w = w_in[0]
    order = np.array(_HEAD_ORDER)
    ba_cols = np.concatenate([_C_B + order, _C_A + order])
    par = jnp.zeros((8, LANES), F32)
    par = par.at[0, 8:16].set(a_log[0][order]).at[1, 8:16].set(dt_bias[0][order])
    return dict(
        az=w[:, :_C_B].astype(BF16),
        ba=jnp.pad(w[:, ba_cols], ((0, 0), (0, LANES - 2 * GDN_HEADS))).astype(BF16),
        q=w[:, _C_Q:_C_K].astype(BF16),
        kv=[jnp.concatenate([w[:, _C_K + g * DWA_GW:_C_K + (g + 1) * DWA_GW],
                             w[:, _C_V + g * DWA_GW:_C_V + (g + 1) * DWA_GW]], axis=1).astype(BF16)
            for g in range(len(DWA_GROUPS))],
        gates=w[:, _C_GA:].astype(BF16),
        conv=conv_w[0], par=par, gn=gdn_norm,
        wa=w_proj_a[0].astype(BF16), wb=w_proj_b[0].astype(BF16), wo=w_out[0].astype(BF16),
        up1=w_ffn1_up[0].astype(BF16), down1=w_ffn1_down[0].astype(BF16),
        up2=w_ffn2_up[0].astype(BF16), down2=w_ffn2_down[0].astype(BF16),
    )


def _trunk(x3, mod3, pos, W, norms, *, sample_state=None):
    G, R, _ = x3.shape
    h1, xm = _ffn(x3, mod3, 0, norms["ffn1"], W["up1"], W["down1"], next_k=3, next_gain=norms["mix"])
    tables = _rope_tables(pos)
    az = _proj(xm, W["az"], F32, name="proj_az")
    ba = _proj(xm, W["ba"], F32, tn=LANES, name="proj_ba")
    gates = _proj(xm, W["gates"], F32, name="proj_gates")
    kvs = [_proj(xm, W["kv"][g], F32, rope="first", tables=tables, name=f"proj_kv{g}")
           for g in range(len(DWA_GROUPS))]
    q = _proj(xm, W["q"], F32, rope="all", tables=tables, name="proj_q")
    if sample_state is None:
        o_a, s_new = _gdn_prompt(az, ba, W["conv"], W["par"], W["gn"])
        o_b = _dwa_prompt(q, kvs)
    else:
        hist, s0, caches = sample_state
        o_a, s_new = _gdn_sample(az.reshape(R, 1, -1), ba.reshape(R, 1, LANES), hist, s0,
                                 W["conv"], W["par"], W["gn"])
        o_b = _dwa_sample(q.reshape(R, len(DWA_GROUPS) * DWA_HEADS, DWA_DH),
                          [kv.reshape(R, 2, DWA_HEADS, DWA_DH) for kv in kvs], caches)
        o_a = o_a.reshape(1, R, -1)
        o_b = o_b.reshape(1, R, DWA_GW).astype(BF16)
    h2 = _mix_out(o_a, o_b, gates, h1, mod3, 5, W["wa"], W["wb"], W["wo"])
    y = _ffn(h2, mod3, 6, norms["ffn2"], W["up2"], W["down2"], final_gain=norms["final"])
    return y, az, s_new, kvs


def kernel(x_prompt, x_sample, state_conv, state_delta, cache_kv_w128, cache_kv_w512, cache_kv_w2048, c_prompt, c_sample, w_ada, b_ada, norm_ffn1, w_ffn1_up, w_ffn1_down, norm_mix, w_in, conv_w, a_log, dt_bias, gdn_norm, w_proj_a, w_proj_b, w_out, norm_ffn2, w_ffn2_up, w_ffn2_down, norm_final):
    B, T, _ = x_prompt.shape
    N, S, _ = x_sample.shape
    assert S == 1 and T % (DWA_BLOCK * DWA_GROUPS[-1][1]) == 0
    W = _prep_weights(w_in, conv_w, a_log, dt_bias, gdn_norm, w_proj_a, w_proj_b, w_out,
                      w_ffn1_up, w_ffn1_down, w_ffn2_up, w_ffn2_down)
    norms = dict(ffn1=norm_ffn1[0], mix=norm_mix[0], ffn2=norm_ffn2[0], final=norm_final)

    mod = _ada(jnp.concatenate([c_prompt, c_sample], axis=0), w_ada[0], b_ada)
    mod_p = mod[:B].reshape(B, 1, N_MOD * D_MODEL)
    mod_s = mod[B:].reshape(1, N, N_MOD * D_MODEL)

    y_p, az_p, s_p, kv_p = _trunk(x_prompt, mod_p, jnp.arange(T, dtype=jnp.int32), W, norms)
    caches = (cache_kv_w128[0], cache_kv_w512[0], cache_kv_w2048[0])
    y_s, az_s, s_s, kv_s = _trunk(x_sample.reshape(1, N, D_MODEL), mod_s,
                                  jnp.full((N,), PAST_LEN, dtype=jnp.int32), W, norms,
                                  sample_state=(state_conv[0], state_delta[0], caches))

    keep = GDN_CONV - 1
    conv_p = az_p[:, T - keep:, :CONV_CH][None]
    conv_s = jnp.concatenate([state_conv[0], az_s.reshape(N, 1, -1)[:, :, :CONV_CH]], axis=1)[:, -keep:][None]
    kv_out_p = []
    for (win, _), kv in zip(DWA_GROUPS, kv_p):
        k = min(win, T)
        kv_out_p.append(kv[:, T - k:].reshape(1, B, k, 2, DWA_HEADS, DWA_DH))
    kv_out_s = [kv.reshape(1, N, 1, 2, DWA_HEADS, DWA_DH) for kv in kv_s]
    return (y_p, y_s.reshape(N, 1, D_MODEL), conv_p, s_p[None], *kv_out_p,
            conv_s, s_s[None], *kv_out_s)
```

```python
import functools

import jax
import jax.numpy as jnp
import numpy as np
from jax import lax
from jax.experimental import pallas as pl
from jax.experimental.pallas import tpu as pltpu

F32 = jnp.float32
BF16 = jnp.bfloat16

D_MODEL = 2048
D_FF = 5632
N_MOD = 9
NORM_EPS = 1e-6
PAST_LEN = 16384

GDN_HEADS = 8
GDN_DK = 128
GDN_CONV = 4
GDN_CHUNK = 64
GDN_QK = GDN_HEADS * GDN_DK
CONV_CH = 3 * GDN_QK

DWA_GROUPS = ((128, 1), (512, 4), (2048, 16))
DWA_HEADS = 4
DWA_DH = 128
DWA_GW = DWA_HEADS * DWA_DH
DWA_WIDTH = len(DWA_GROUPS) * DWA_GW
DWA_BLOCK = 128
ROPE_THETA = 10000.0

LANES = 128
VMEM_LIMIT = 56 * 1024 * 1024

_C_Z = CONV_CH
_C_B = _C_Z + GDN_QK
_C_A = _C_B + GDN_HEADS
_C_Q = _C_A + GDN_HEADS
_C_K = _C_Q + DWA_WIDTH
_C_V = _C_K + DWA_WIDTH
_C_GA = _C_V + DWA_WIDTH
_C_GB = _C_GA + D_MODEL

_HEAD_ORDER = (0, 2, 4, 6, 1, 3, 5, 7)
_N_PAIRS = GDN_HEADS // 2


def _sigmoid(x):
    return 1.0 / (1.0 + jnp.exp(-x))


def _silu(x):
    return x * _sigmoid(x)


def _dot(a, b):
    return jnp.dot(a, b, preferred_element_type=F32)


def _dot_nt(a, b):
    return lax.dot_general(a, b, (((1,), (1,)), ((), ())), preferred_element_type=F32)


def _params(*sem):
    return pltpu.CompilerParams(dimension_semantics=sem, vmem_limit_bytes=VMEM_LIMIT)


def _ada_kernel(c_ref, w_ref, b_ref, o_ref):
    a = _silu(c_ref[...]).astype(BF16)
    o_ref[...] = _dot(a, w_ref[...].astype(BF16)) + b_ref[...]


def _ada(c, w, b):
    m, n = c.shape[0], w.shape[1]
    tn = 1024
    return pl.pallas_call(
        _ada_kernel,
        out_shape=jax.ShapeDtypeStruct((m, n), F32),
        grid=(n // tn,),
        in_specs=[pl.BlockSpec((m, D_MODEL), lambda j: (0, 0)),
                  pl.BlockSpec((D_MODEL, tn), lambda j: (0, j)),
                  pl.BlockSpec((1, tn), lambda j: (0, j))],
        out_specs=pl.BlockSpec((m, tn), lambda j: (0, j)),
        compiler_params=_params("arbitrary"),
        name="ada_mod",
    )(c, w, b)


def _rms(x):
    return x * lax.rsqrt(jnp.mean(x * x, axis=-1, keepdims=True) + NORM_EPS)


def _modulated(h, gain, scale, shift):
    return ((_rms(h) * gain) * (1.0 + scale) + shift).astype(BF16)


def _swiglu_step(xm, wg_ref, wu_ref, wd_ref, acc_sc):
    g = _dot(xm, wg_ref[...])
    u = _dot(xm, wu_ref[...])
    acc_sc[...] += _dot((_silu(g) * u).astype(BF16), wd_ref[...])


def _ffn_first_kernel(x_ref, sh_ref, sc_ref, gt_ref, gain_ref, wg_ref, wu_ref, wd_ref,
                      nsh_ref, nsc_ref, ngain_ref, out_ref, nxt_ref, xm_sc, acc_sc):
    j = pl.program_id(2)

    @pl.when(j == 0)
    def _():
        xm_sc[...] = _modulated(x_ref[0], gain_ref[...], sc_ref[0], sh_ref[0])
        acc_sc[...] = jnp.zeros_like(acc_sc)

    _swiglu_step(xm_sc[...], wg_ref, wu_ref, wd_ref, acc_sc)

    @pl.when(j == pl.num_programs(2) - 1)
    def _():
        h = x_ref[0] + 0.5 * gt_ref[0] * acc_sc[...]
        out_ref[0] = h
        nxt_ref[0] = _modulated(h, ngain_ref[...], nsc_ref[0], nsh_ref[0])


def _ffn_last_kernel(x_ref, xm_ref, gt_ref, wg_ref, wu_ref, wd_ref, ngain_ref, out_ref, acc_sc):
    j = pl.program_id(2)

    @pl.when(j == 0)
    def _():
        acc_sc[...] = jnp.zeros_like(acc_sc)

    _swiglu_step(xm_ref[0], wg_ref, wu_ref, wd_ref, acc_sc)

    @pl.when(j == pl.num_programs(2) - 1)
    def _():
        out_ref[0] = _rms(x_ref[0] + 0.5 * gt_ref[0] * acc_sc[...]) * ngain_ref[...]


FFN_TM = 512
FFN_TF = 512


def _mod_spec(mod3, R, tm, k):
    if mod3.shape[1] == R:
        return pl.BlockSpec((1, tm, D_MODEL), lambda g, i, *_: (g, i, k))
    return pl.BlockSpec((1, 1, D_MODEL), lambda g, i, *_: (g, 0, k))


def _ffn_specs(R):
    tm = min(FFN_TM, R)
    nff = D_FF // FFN_TF
    rows = pl.BlockSpec((1, tm, D_MODEL), lambda g, i, j: (g, i, 0))
    vec = pl.BlockSpec((1, D_MODEL), lambda g, i, j: (0, 0))
    weights = [pl.BlockSpec((D_MODEL, FFN_TF), lambda g, i, j: (0, j)),
               pl.BlockSpec((D_MODEL, FFN_TF), lambda g, i, j: (0, j + nff)),
               pl.BlockSpec((FFN_TF, D_MODEL), lambda g, i, j: (j, 0))]
    return tm, nff, rows, vec, weights


def _ffn_first(x3, mod3, gain, w_up, w_down, next_gain):
    G, R, _ = x3.shape
    tm, nff, rows, vec, weights = _ffn_specs(R)
    ms = [_mod_spec(mod3, R, tm, k) for k in range(5)]
    return pl.pallas_call(
        _ffn_first_kernel,
        out_shape=[jax.ShapeDtypeStruct((G, R, D_MODEL), F32), jax.ShapeDtypeStruct((G, R, D_MODEL), BF16)],
        grid=(G, R // tm, nff),
        in_specs=[rows, ms[0], ms[1], ms[2], vec] + weights + [ms[3], ms[4], vec],
        out_specs=[rows, rows],
        scratch_shapes=[pltpu.VMEM((tm, D_MODEL), BF16), pltpu.VMEM((tm, D_MODEL), F32)],
        compiler_params=_params("parallel", "parallel", "arbitrary"),
        name="ffn_first",
    )(x3, mod3, mod3, mod3, gain.reshape(1, D_MODEL), w_up, w_up, w_down, mod3, mod3,
      next_gain.reshape(1, D_MODEL))


def _ffn_last(x3, xm3, mod3, w_up, w_down, final_gain):
    G, R, _ = x3.shape
    tm, nff, rows, vec, weights = _ffn_specs(R)
    return pl.pallas_call(
        _ffn_last_kernel,
        out_shape=jax.ShapeDtypeStruct((G, R, D_MODEL), F32),
        grid=(G, R // tm, nff),
        in_specs=[rows, rows, _mod_spec(mod3, R, tm, 8)] + weights + [vec],
        out_specs=rows,
        scratch_shapes=[pltpu.VMEM((tm, D_MODEL), F32)],
        compiler_params=_params("parallel", "parallel", "arbitrary"),
        name="ffn_last",
    )(x3, xm3, mod3, w_up, w_up, w_down, final_gain.reshape(1, D_MODEL))


def _rope_tile(x, cosf, sinf):
    return x * cosf + pltpu.roll(x, DWA_DH // 2, axis=1) * sinf


TAIL_ROWS = 8


def _proj_kernel(a_ref, w_ref, *rest, rope, tn, tail):
    if rope == "none":
        acc = _dot(a_ref[0], w_ref[...])
        rest[0][0] = acc.astype(rest[0].dtype)
        if tail:
            rest[1][0] = acc[acc.shape[0] - TAIL_ROWS:]
        return
    cos_ref, sin_ref, o_ref = rest
    acc = _dot(a_ref[0], w_ref[...])

    def roped():
        cosf, sinf = cos_ref[...], sin_ref[...]
        for h in range(tn // DWA_DH):
            sl = slice(h * DWA_DH, (h + 1) * DWA_DH)
            o_ref[0, :, sl] = _rope_tile(acc[:, sl], cosf, sinf).astype(o_ref.dtype)

    if rope == "all":
        roped()
    else:
        j = pl.program_id(2)
        pl.when(j == 0)(roped)

        @pl.when(j != 0)
        def _():
            o_ref[0] = acc.astype(o_ref.dtype)


PROJ_TM = 1024


def _proj(xm3, w, out_dtype, *, rope="none", tables=None, tn=512, tail=False, name="proj"):
    G, R, _ = xm3.shape
    n = w.shape[1]
    tm = min(PROJ_TM, R)
    in_specs = [pl.BlockSpec((1, tm, D_MODEL), lambda g, i, j: (g, i, 0)),
                pl.BlockSpec((D_MODEL, tn), lambda g, i, j: (0, j))]
    args = [xm3, w]
    if rope != "none":
        tab = pl.BlockSpec((tm, DWA_DH), lambda g, i, j: (i, 0))
        in_specs += [tab, tab]
        args += list(tables)
    out_shape = jax.ShapeDtypeStruct((G, R, n), out_dtype)
    out_specs = pl.BlockSpec((1, tm, tn), lambda g, i, j: (g, i, j))
    if tail:
        out_shape = [out_shape, jax.ShapeDtypeStruct((G, R // tm * TAIL_ROWS, n), F32)]
        out_specs = [out_specs, pl.BlockSpec((1, TAIL_ROWS, tn), lambda g, i, j: (g, i, j))]
    res = pl.pallas_call(
        functools.partial(_proj_kernel, rope=rope, tn=tn, tail=tail),
        out_shape=out_shape,
        grid=(G, R // tm, n // tn),
        in_specs=in_specs,
        out_specs=out_specs,
        compiler_params=_params("parallel", "parallel", "arbitrary"),
        name=name,
    )(*args)
    if tail:
        return res[0], res[1][:, -TAIL_ROWS:]
    return res


def _rope_tables(pos):
    half = DWA_DH // 2
    inv = jnp.power(ROPE_THETA, -jnp.arange(half, dtype=F32) / half)
    ang = pos.astype(F32)[:, None] * inv[None, :]
    cos, sin = jnp.cos(ang), jnp.sin(ang)
    return jnp.concatenate([cos, cos], axis=-1), jnp.concatenate([-sin, sin], axis=-1)


def _gdn_gates(ba, par):
    beta = _sigmoid(ba)
    x = ba + par[1:2]
    softplus = jnp.maximum(x, 0.0) + jnp.log1p(jnp.exp(-jnp.abs(x)))
    return beta, -jnp.exp(par[0:1]) * softplus


def _gated_out(o, z, gn):
    return (_rms(o) * gn) * _silu(z.astype(F32))


def _gdn_chunk_kernel(qkv_ref, z_ref, ba_ref, cw_ref, par_ref, gn_ref, o_ref, s_ref, buf_sc, x_sc, S_sc):
    C = GDN_CHUNK
    c = pl.program_id(1)

    @pl.when(c == 0)
    def _():
        buf_sc[0:8, :] = jnp.zeros((8, CONV_CH), F32)
        S_sc[...] = jnp.zeros_like(S_sc)

    buf_sc[8:8 + C, :] = qkv_ref[0].astype(F32)
    acc = buf_sc[5:5 + C, :] * cw_ref[0:1, :]
    for j in range(1, GDN_CONV):
        acc = acc + buf_sc[5 + j:5 + j + C, :] * cw_ref[j:j + 1, :]
    x_sc[...] = _silu(acc)
    buf_sc[0:8, :] = buf_sc[C:C + 8, :]

    beta_t, g_t = _gdn_gates(ba_ref[0], par_ref[...])
    row = lax.broadcasted_iota(jnp.int32, (C, LANES), 0)
    G = g_t
    s = 1
    while s < C:
        G = G + jnp.where(row >= s, pltpu.roll(G, s, axis=0), 0.0)
        s *= 2
    shift = LANES - _N_PAIRS
    Gs = jnp.concatenate([G, pltpu.roll(G, shift, axis=1)], axis=0)
    Bs = jnp.concatenate([beta_t, pltpu.roll(beta_t, shift, axis=1)], axis=0)
    GT = Gs.T
    r2 = lax.broadcasted_iota(jnp.int32, (2 * C, LANES), 0)
    Glast = jnp.where(r2 < C, Gs[C - 1:C, :], Gs[2 * C - 1:2 * C, :])
    eG = jnp.exp(Gs)
    eGl = jnp.exp(Glast - Gs)
    egl = jnp.exp(Glast)

    ii = lax.broadcasted_iota(jnp.int32, (2 * C, 2 * C), 0)
    jj = lax.broadcasted_iota(jnp.int32, (2 * C, 2 * C), 1)
    same = (ii // C) == (jj // C)
    strict = same & (ii > jj)
    diag = ii == jj
    blk = (ii // 16) == (jj // 16)
    top = r2 < C
    gn = gn_ref[...]

    P = range(_N_PAIRS)
    C2 = 2 * C

    def pair(p, off):
        a = x_sc[:, off + (2 * p) * GDN_DK: off + (2 * p + 1) * GDN_DK]
        b = x_sc[:, off + (2 * p + 1) * GDN_DK: off + (2 * p + 2) * GDN_DK]
        return jnp.concatenate([a, b], axis=0)

    def l2(t):
        return t * lax.rsqrt(jnp.sum(t * t, axis=-1, keepdims=True) + NORM_EPS)

    def col(t, p, base=0):
        return t[:, base + p:base + p + 1]

    q2 = [l2(pair(p, 0)) * (GDN_DK ** -0.5) for p in P]
    k2 = [l2(pair(p, GDN_QK)) for p in P]
    kb = [k2[p] * col(Bs, p) for p in P]
    vb = [pair(p, 2 * GDN_QK) * col(Bs, p) for p in P]
    dec = [jnp.where(strict, jnp.exp(jnp.where(strict, col(Gs, p, 8) - GT[8 + p:9 + p, :], 0.0)), 0.0) for p in P]
    kq = [_dot_nt(jnp.concatenate([kb[p], q2[p]], axis=0).astype(BF16), k2[p].astype(BF16)) for p in P]
    A = [kq[p][:C2] * dec[p] for p in P]
    qk = [(kq[p][C2:] * (dec[p] + jnp.where(diag, 1.0, 0.0))).astype(BF16) for p in P]

    Dg = [jnp.where(blk, A[p], 0.0) for p in P]
    E = [A[p] - Dg[p] for p in P]
    Q = [-Dg[p] for p in P]
    Dgb = [Dg[p].astype(BF16) for p in P]
    Dp = [_dot(Dgb[p], Dgb[p]) for p in P]
    for _ in range(2):
        Dpb = [Dp[p].astype(BF16) for p in P]
        st = [_dot(jnp.concatenate([Dpb[p], Q[p].astype(BF16)], axis=0), Dpb[p]) for p in P]
        Q = [Q[p] + Dp[p] + st[p][C2:] for p in P]
        Dp = [st[p][:C2] for p in P]
    Q = [Q[p] + Dp[p] + _dot(Q[p].astype(BF16), Dp[p].astype(BF16)) for p in P]
    rhs = [jnp.concatenate([kb[p] * col(eG, p, 8), vb[p]], axis=1) for p in P]
    er = [_dot(Q[p].astype(BF16), jnp.concatenate([E[p], rhs[p]], axis=1).astype(BF16)) for p in P]
    N = [E[p] + er[p][:, :C2] for p in P]
    y = [rhs[p] + er[p][:, C2:] for p in P]
    Nb = [N[p].astype(BF16) for p in P]
    ny = [_dot(Nb[p], jnp.concatenate([N[p], y[p]], axis=1).astype(BF16)) for p in P]
    zz = [y[p] - ny[p][:, C2:] for p in P]
    sol = [zz[p] + _dot(ny[p][:, :C2].astype(BF16), zz[p].astype(BF16)) for p in P]

    wq = [jnp.concatenate([sol[p][:, :GDN_DK], q2[p] * col(eG, p, 8)], axis=0).astype(BF16) for p in P]
    S_old = [S_sc[h] for h in range(GDN_HEADS)]
    rS = [_dot(wq[h // 2], S_old[h].astype(BF16)) for h in range(GDN_HEADS)]
    vn = [(sol[p][:, GDN_DK:] - jnp.where(top, rS[2 * p][:C2], rS[2 * p + 1][:C2])).astype(BF16) for p in P]
    kdT = [(k2[p] * col(eGl, p, 8)).T.astype(BF16) for p in P]
    zero = jnp.zeros((C2, GDN_DK), BF16)
    for p in P:
        o2 = jnp.where(top, rS[2 * p][C2:], rS[2 * p + 1][C2:]) + _dot(qk[p], vn[p])
        S_sc[2 * p] = S_old[2 * p] * egl[0:1, 8 + p:9 + p] + _dot(kdT[p], jnp.where(top, vn[p], zero))
        S_sc[2 * p + 1] = S_old[2 * p + 1] * egl[C:C + 1, 8 + p:9 + p] + _dot(kdT[p], jnp.where(top, zero, vn[p]))
        for e in range(2):
            h = 2 * p + e
            sl = slice(h * GDN_DK, (h + 1) * GDN_DK)
            o_ref[0, :, sl] = _gated_out(o2[e * C:(e + 1) * C], z_ref[0, :, sl], gn).astype(o_ref.dtype)

    @pl.when(c == pl.num_programs(1) - 1)
    def _():
        s_ref[0] = S_sc[...]


def _gdn_prompt(az, ba, conv_w, par, gn):
    B, T, _ = az.shape
    C = GDN_CHUNK
    return pl.pallas_call(
        _gdn_chunk_kernel,
        out_shape=[jax.ShapeDtypeStruct((B, T, GDN_QK), BF16),
                   jax.ShapeDtypeStruct((B, GDN_HEADS, GDN_DK, GDN_DK), F32)],
        grid=(B, T // C),
        in_specs=[pl.BlockSpec((1, C, CONV_CH), lambda b, c: (b, c, 0)),
                  pl.BlockSpec((1, C, GDN_QK), lambda b, c: (b, c, CONV_CH // GDN_QK)),
                  pl.BlockSpec((1, C, LANES), lambda b, c: (b, c, 0)),
                  pl.BlockSpec((GDN_CONV, CONV_CH), lambda b, c: (0, 0)),
                  pl.BlockSpec((8, LANES), lambda b, c: (0, 0)),
                  pl.BlockSpec((1, GDN_DK), lambda b, c: (0, 0))],
        out_specs=[pl.BlockSpec((1, C, GDN_QK), lambda b, c: (b, c, 0)),
                   pl.BlockSpec((1, GDN_HEADS, GDN_DK, GDN_DK), lambda b, c: (b, 0, 0, 0))],
        scratch_shapes=[pltpu.VMEM((C + 8, CONV_CH), F32), pltpu.VMEM((C, CONV_CH), F32),
                        pltpu.VMEM((GDN_HEADS, GDN_DK, GDN_DK), F32)],
        compiler_params=_params("parallel", "arbitrary"),
        name="gdn_chunked",
    )(az, az, ba, conv_w, par, gn)


def _gdn_step_kernel(qkv_ref, hist_ref, z_ref, ba_ref, s0_ref, cw_ref, par_ref, gn_ref, o_ref, s_ref):
    acc = qkv_ref[0] * cw_ref[GDN_CONV - 1:GDN_CONV, :]
    for j in range(GDN_CONV - 1):
        acc = acc + hist_ref[0, j:j + 1, :] * cw_ref[j:j + 1, :]
    x = _silu(acc)
    beta_t, g_t = _gdn_gates(ba_ref[0], par_ref[...])
    eg_t = jnp.exp(g_t)

    def head(off, h):
        return x[:, off + h * GDN_DK: off + (h + 1) * GDN_DK]

    rows = []
    for off, scale in ((GDN_QK, 1.0), (0, GDN_DK ** -0.5)):
        for h in range(GDN_HEADS):
            t = head(off, h)
            rows.append(t * lax.rsqrt(jnp.sum(t * t, axis=-1, keepdims=True) + NORM_EPS) * scale)
    ri = lax.broadcasted_iota(jnp.int32, (LANES, LANES), 0)
    tile = jnp.zeros((LANES, LANES), F32)
    for r, t in enumerate(rows):
        tile = jnp.where(ri == r, t, tile)
    cols = tile.T
    gn = gn_ref[...]
    for h in range(GDN_HEADS):
        lane = _HEAD_ORDER.index(h)
        kcol = cols[:, h:h + 1]
        qcol = cols[:, GDN_HEADS + h:GDN_HEADS + h + 1]
        S = s0_ref[0, h] * eg_t[:, 8 + lane:9 + lane]
        kv = jnp.sum(kcol * S, axis=0, keepdims=True)
        delta = beta_t[:, lane:lane + 1] * (head(2 * GDN_QK, h) - kv)
        S = S + kcol * delta
        s_ref[0, h] = S
        o = jnp.sum(qcol * S, axis=0, keepdims=True)
        sl = slice(h * GDN_DK, (h + 1) * GDN_DK)
        o_ref[0, :, sl] = _gated_out(o, z_ref[0, :, sl], gn).astype(o_ref.dtype)


def _gdn_sample(az, ba, hist, s0, conv_w, par, gn):
    N = az.shape[0]
    return pl.pallas_call(
        _gdn_step_kernel,
        out_shape=[jax.ShapeDtypeStruct((N, 1, GDN_QK), BF16),
                   jax.ShapeDtypeStruct((N, GDN_HEADS, GDN_DK, GDN_DK), F32)],
        grid=(N,),
        in_specs=[pl.BlockSpec((1, 1, CONV_CH), lambda b: (b, 0, 0)),
                  pl.BlockSpec((1, GDN_CONV - 1, CONV_CH), lambda b: (b, 0, 0)),
                  pl.BlockSpec((1, 1, GDN_QK), lambda b: (b, 0, CONV_CH // GDN_QK)),
                  pl.BlockSpec((1, 1, LANES), lambda b: (b, 0, 0)),
                  pl.BlockSpec((1, GDN_HEADS, GDN_DK, GDN_DK), lambda b: (b, 0, 0, 0)),
                  pl.BlockSpec((GDN_CONV, CONV_CH), lambda b: (0, 0)),
                  pl.BlockSpec((8, LANES), lambda b: (0, 0)),
                  pl.BlockSpec((1, GDN_DK), lambda b: (0, 0))],
        out_specs=[pl.BlockSpec((1, 1, GDN_QK), lambda b: (b, 0, 0)),
                   pl.BlockSpec((1, GDN_HEADS, GDN_DK, GDN_DK), lambda b: (b, 0, 0, 0))],
        compiler_params=_params("parallel"),
        name="gdn_step",
    )(az, hist, az, ba, s0, conv_w, par, gn)


DWA_BATCH = 4


def _dwa_blocks(refs, blocks, masks, first, scratch):
    q_ref, k_ref, v_ref = refs
    m_sc, l_sc, acc_sc = scratch
    cur_ok, prev_ok = masks
    scale = DWA_DH ** -0.5
    BL = DWA_BLOCK
    qb = [q_ref[0, cur, :].astype(BF16) for cur, _, _ in blocks]
    sc = [jnp.where(cur_ok, _dot_nt(qb[i], k_ref[0, cur, :].astype(BF16)) * scale, -jnp.inf)
          for i, (cur, _, _) in enumerate(blocks)]
    sp = [None if prev is None else
          jnp.where(prev_ok & on, _dot_nt(qb[i], k_ref[0, prev, :].astype(BF16)) * scale, -jnp.inf)
          for i, (_, prev, on) in enumerate(blocks)]
    m, l, pc, pp = [], [], [], []
    for i in range(len(blocks)):
        mi = jnp.max(sc[i], axis=-1, keepdims=True)
        if sp[i] is not None:
            mi = jnp.maximum(mi, jnp.max(sp[i], axis=-1, keepdims=True))
        p = jnp.exp(sc[i] - mi)
        li = jnp.sum(p, axis=-1, keepdims=True)
        pc.append(p.astype(BF16))
        if sp[i] is not None:
            p = jnp.exp(sp[i] - mi)
            li = li + jnp.sum(p, axis=-1, keepdims=True)
            pp.append(p.astype(BF16))
        else:
            pp.append(None)
        m.append(mi)
        l.append(li)
    acc = [_dot(pc[i], v_ref[0, cur, :].astype(BF16)) for i, (cur, _, _) in enumerate(blocks)]
    acc = [a if pp[i] is None else a + _dot(pp[i], v_ref[0, blocks[i][1], :].astype(BF16))
           for i, a in enumerate(acc)]
    for i, (cur, _, _) in enumerate(blocks):
        if first:
            m_sc[cur, :] = jnp.broadcast_to(m[i], (BL, LANES))
            l_sc[cur, :] = jnp.broadcast_to(l[i], (BL, LANES))
            acc_sc[cur, :] = acc[i]
        else:
            m_old = m_sc[cur, :]
            m_new = jnp.maximum(m_old, m[i])
            a = jnp.exp(m_old - m_new)
            b = jnp.exp(m[i] - m_new)
            m_sc[cur, :] = m_new
            l_sc[cur, :] = a * l_sc[cur, :] + b * l[i]
            acc_sc[cur, :] = a * acc_sc[cur, :] + b * acc[i]


def _dwa_kernel(q0_ref, q1_ref, q2_ref, k0_ref, v0_ref, k1_ref, v1_ref, k2_ref, v2_ref, o_ref, m_sc, l_sc, acc_sc):
    BL = DWA_BLOCK
    T = o_ref.shape[1]
    ri = lax.broadcasted_iota(jnp.int32, (BL, BL), 0)
    ci = lax.broadcasted_iota(jnp.int32, (BL, BL), 1)
    masks = (ri >= ci, ci >= ri)
    q_refs, k_refs, v_refs = (q0_ref, q1_ref, q2_ref), (k0_ref, k1_ref, k2_ref), (v0_ref, v1_ref, v2_ref)
    scratch = (m_sc, l_sc, acc_sc)

    for gi, (_, dil) in enumerate(DWA_GROUPS):
        refs = (q_refs[gi], k_refs[gi], v_refs[gi])
        nb = T // dil // BL

        def rows(blk, r, dil=dil):
            start = blk * (BL * dil) + r
            if dil == 1:
                return pl.ds(pl.multiple_of(start, BL), BL)
            return pl.ds(start, BL, stride=dil)

        def block(n, r, nb=nb, rows=rows):
            if nb == 1:
                return rows(n, r), None, None
            return rows(n, r), rows(jnp.maximum(n - 1, 0), r), n > 0

        if dil >= DWA_BATCH:
            for r0 in range(0, dil, DWA_BATCH):
                def body(n, carry, r0=r0, refs=refs, block=block, first=gi == 0):
                    _dwa_blocks(refs, [block(n, r0 + t) for t in range(DWA_BATCH)], masks, first, scratch)
                    return carry
                if nb == 1:
                    body(0, 0)
                else:
                    lax.fori_loop(0, nb, body, 0)
        else:
            assert dil == 1 and nb % DWA_BATCH == 0

            def body(i, carry, refs=refs, block=block, first=gi == 0):
                _dwa_blocks(refs, [block(i * DWA_BATCH + t, 0) for t in range(DWA_BATCH)], masks, first, scratch)
                return carry
            lax.fori_loop(0, nb // DWA_BATCH, body, 0)

    o_ref[0] = (acc_sc[...] / l_sc[...]).astype(o_ref.dtype)


def _dwa_prompt(q, kvs):
    B, T, _ = q.shape

    def col(c):
        return pl.BlockSpec((1, T, DWA_DH), lambda b, h: (b, 0, c(h)))

    ng = len(DWA_GROUPS)
    in_specs = [col(lambda h, g=g: g * DWA_HEADS + h) for g in range(ng)]
    args = [q] * ng
    for g in range(ng):
        in_specs += [col(lambda h: h), col(lambda h: DWA_HEADS + h)]
        args += [kvs[g], kvs[g]]
    return pl.pallas_call(
        _dwa_kernel,
        out_shape=jax.ShapeDtypeStruct((B, T, DWA_GW), BF16),
        grid=(B, DWA_HEADS),
        in_specs=in_specs,
        out_specs=col(lambda h: h),
        scratch_shapes=[pltpu.VMEM((T, LANES), F32), pltpu.VMEM((T, LANES), F32), pltpu.VMEM((T, DWA_DH), F32)],
        compiler_params=_params("parallel", "parallel"),
        name="dwa_prompt",
    )(*args)


def _dwa_step_kernel(q_ref, n0_ref, n1_ref, n2_ref, c0_ref, c1_ref, c2_ref, o_ref):
    new_refs = (n0_ref, n1_ref, n2_ref)
    cache_refs = (c0_ref, c1_ref, c2_ref)
    scale = DWA_DH ** -0.5
    parts = []
    for gi in range(len(DWA_GROUPS)):
        q = q_ref[0, gi * DWA_HEADS:(gi + 1) * DWA_HEADS, :]
        k = cache_refs[gi][0, :, 0, 0]
        v = cache_refs[gi][0, :, 0, 1]
        s = jnp.sum(k * q[None], axis=-1, keepdims=True) * scale
        s_new = jnp.sum(new_refs[gi][0, 0] * q, axis=-1, keepdims=True) * scale
        m = jnp.maximum(jnp.max(s, axis=0), s_new)
        p = jnp.exp(s - m[None])
        p_new = jnp.exp(s_new - m)
        l = jnp.sum(p, axis=0) + p_new
        acc = jnp.sum(p * v, axis=0) + p_new * new_refs[gi][0, 1]
        parts.append((m, l, acc))
    mm = jnp.maximum(jnp.maximum(parts[0][0], parts[1][0]), parts[2][0])
    num = jnp.zeros((DWA_HEADS, DWA_DH), F32)
    den = jnp.zeros((DWA_HEADS, 1), F32)
    for m, l, acc in parts:
        e = jnp.exp(m - mm)
        num = num + e * acc
        den = den + e * l
    o_ref[0] = num / den


def _dwa_sample(q, kv_new, caches):
    N = q.shape[0]
    span = DWA_BLOCK
    in_specs = [pl.BlockSpec((1, len(DWA_GROUPS) * DWA_HEADS, DWA_DH), lambda b: (b, 0, 0))]
    in_specs += [pl.BlockSpec((1, 2, DWA_HEADS, DWA_DH), lambda b: (b, 0, 0, 0))] * 3
    views = []
    for (win, dil), cache in zip(DWA_GROUPS, caches):
        L = cache.shape[1]
        assert L == win and L // dil == span
        views.append(cache.reshape(N, span, dil, 2, DWA_HEADS, DWA_DH))
        in_specs.append(pl.BlockSpec((1, span, 1, 2, DWA_HEADS, DWA_DH), lambda b: (b, 0, 0, 0, 0, 0)))
    return pl.pallas_call(
        _dwa_step_kernel,
        out_shape=jax.ShapeDtypeStruct((N, DWA_HEADS, DWA_DH), F32),
        grid=(N,),
        in_specs=in_specs,
        out_specs=pl.BlockSpec((1, DWA_HEADS, DWA_DH), lambda b: (b, 0, 0)),
        compiler_params=_params("parallel"),
        name="dwa_step",
    )(q, *kv_new, *views)


def _mix_out_kernel(oa_ref, ob_ref, ga_ref, gb_ref, h_ref, gt_ref, nsh_ref, nsc_ref, ngain_ref,
                    wa_ref, wb_ref, wo_ref, out_ref, nxt_ref):
    ta = _dot(oa_ref[0], wa_ref[...])
    tb = _dot(ob_ref[0], wb_ref[...])
    merged = _sigmoid(ga_ref[0].astype(F32)) * ta + _sigmoid(gb_ref[0].astype(F32)) * tb
    h = h_ref[0] + gt_ref[0] * _dot(merged.astype(BF16), wo_ref[...])
    out_ref[0] = h
    nxt_ref[0] = _modulated(h, ngain_ref[...], nsc_ref[0], nsh_ref[0])


MIX_TM = 512


def _mix_out(oa, ob, gates, h, mod3, wa, wb, wo, next_gain):
    G, R, _ = h.shape
    tm = min(MIX_TM, R)

    def rows(width, col=0):
        return pl.BlockSpec((1, tm, width), lambda g, i: (g, i, col))

    def whole(shape):
        return pl.BlockSpec(shape, lambda g, i: (0, 0), pipeline_mode=pl.Buffered(1))

    return pl.pallas_call(
        _mix_out_kernel,
        out_shape=[jax.ShapeDtypeStruct((G, R, D_MODEL), F32), jax.ShapeDtypeStruct((G, R, D_MODEL), BF16)],
        grid=(G, R // tm),
        in_specs=[rows(GDN_QK), rows(DWA_GW), rows(D_MODEL, 0), rows(D_MODEL, 1), rows(D_MODEL),
                  _mod_spec(mod3, R, tm, 5), _mod_spec(mod3, R, tm, 6), _mod_spec(mod3, R, tm, 7),
                  pl.BlockSpec((1, D_MODEL), lambda g, i: (0, 0)),
                  whole(wa.shape), whole(wb.shape), whole(wo.shape)],
        out_specs=[rows(D_MODEL), rows(D_MODEL)],
        compiler_params=_params("parallel", "parallel"),
        name="mix_out",
    )(oa, ob, gates, gates, h, mod3, mod3, mod3, next_gain.reshape(1, D_MODEL), wa, wb, wo)


def _prep_weights(w_in, conv_w, a_log, dt_bias, gdn_norm, w_proj_a, w_proj_b, w_out,
                  w_ffn1_up, w_ffn1_down, w_ffn2_up, w_ffn2_down):
    w = w_in[0]
    order = np.array(_HEAD_ORDER)
    ba_cols = np.concatenate([_C_B + order, _C_A + order])
    par = jnp.zeros((8, LANES), F32)
    par = par.at[0, 8:16].set(a_log[0][order]).at[1, 8:16].set(dt_bias[0][order])
    return dict(
        az=w[:, :_C_B].astype(BF16),
        ba=jnp.pad(w[:, ba_cols], ((0, 0), (0, LANES - 2 * GDN_HEADS))).astype(BF16),
        q=w[:, _C_Q:_C_K].astype(BF16),
        kv=[jnp.concatenate([w[:, _C_K + g * DWA_GW:_C_K + (g + 1) * DWA_GW],
                             w[:, _C_V + g * DWA_GW:_C_V + (g + 1) * DWA_GW]], axis=1).astype(BF16)
            for g in range(len(DWA_GROUPS))],
        gates=w[:, _C_GA:].astype(BF16),
        conv=conv_w[0], par=par, gn=gdn_norm,
        wa=w_proj_a[0].astype(BF16), wb=w_proj_b[0].astype(BF16), wo=w_out[0].astype(BF16),
        up1=w_ffn1_up[0].astype(BF16), down1=w_ffn1_down[0].astype(BF16),
        up2=w_ffn2_up[0].astype(BF16), down2=w_ffn2_down[0].astype(BF16),
    )


def _trunk(x3, mod3, pos, W, norms, *, sample_state=None):
    G, R, _ = x3.shape
    h1, xm = _ffn_first(x3, mod3, norms["ffn1"], W["up1"], W["down1"], norms["mix"])
    tables = _rope_tables(pos)
    ba = _proj(xm, W["ba"], F32, tn=LANES, name="proj_ba")
    gates = _proj(xm, W["gates"], BF16, tn=1024, name="proj_gates")
    kvs = [_proj(xm, W["kv"][g], F32, rope="first", tables=tables, name=f"proj_kv{g}")
           for g in range(len(DWA_GROUPS))]
    q = _proj(xm, W["q"], F32, rope="all", tables=tables, name="proj_q")
    if sample_state is None:
        az, conv_rows = _proj(xm, W["az"], BF16, tn=1024, tail=True, name="proj_az")
        o_a, s_new = _gdn_prompt(az, ba, W["conv"], W["par"], W["gn"])
        o_b = _dwa_prompt(q, kvs)
    else:
        hist, s0, caches = sample_state
        conv_rows = _proj(xm, W["az"], F32, tn=1024, name="proj_az").reshape(R, 1, -1)
        o_a, s_new = _gdn_sample(conv_rows, ba.reshape(R, 1, LANES), hist, s0, W["conv"], W["par"], W["gn"])
        o_b = _dwa_sample(q.reshape(R, len(DWA_GROUPS) * DWA_HEADS, DWA_DH),
                          [kv.reshape(R, 2, DWA_HEADS, DWA_DH) for kv in kvs], caches)
        o_a = o_a.reshape(1, R, -1)
        o_b = o_b.reshape(1, R, DWA_GW).astype(BF16)
    h2, xm2 = _mix_out(o_a, o_b, gates, h1, mod3, W["wa"], W["wb"], W["wo"], norms["ffn2"])
    y = _ffn_last(h2, xm2, mod3, W["up2"], W["down2"], norms["final"])
    return y, conv_rows[:, :, :CONV_CH], s_new, kvs


def kernel(x_prompt, x_sample, state_conv, state_delta, cache_kv_w128, cache_kv_w512, cache_kv_w2048, c_prompt, c_sample, w_ada, b_ada, norm_ffn1, w_ffn1_up, w_ffn1_down, norm_mix, w_in, conv_w, a_log, dt_bias, gdn_norm, w_proj_a, w_proj_b, w_out, norm_ffn2, w_ffn2_up, w_ffn2_down, norm_final):
    B, T, _ = x_prompt.shape
    N, S, _ = x_sample.shape
    assert S == 1 and T % (DWA_BLOCK * DWA_GROUPS[-1][1]) == 0
    W = _prep_weights(w_in, conv_w, a_log, dt_bias, gdn_norm, w_proj_a, w_proj_b, w_out,
                      w_ffn1_up, w_ffn1_down, w_ffn2_up, w_ffn2_down)
    norms = dict(ffn1=norm_ffn1[0], mix=norm_mix[0], ffn2=norm_ffn2[0], final=norm_final)

    mod = _ada(jnp.concatenate([c_prompt, c_sample], axis=0), w_ada[0], b_ada)
    mod_p = mod[:B].reshape(B, 1, N_MOD * D_MODEL)
    mod_s = mod[B:].reshape(1, N, N_MOD * D_MODEL)

    y_p, rows_p, s_p, kv_p = _trunk(x_prompt, mod_p, jnp.arange(T, dtype=jnp.int32), W, norms)
    caches = (cache_kv_w128[0], cache_kv_w512[0], cache_kv_w2048[0])
    y_s, rows_s, s_s, kv_s = _trunk(x_sample.reshape(1, N, D_MODEL), mod_s,
                                    jnp.full((N,), PAST_LEN, dtype=jnp.int32), W, norms,
                                    sample_state=(state_conv[0], state_delta[0], caches))

    keep = GDN_CONV - 1
    conv_p = rows_p[:, TAIL_ROWS - keep:][None]
    conv_s = jnp.concatenate([state_conv[0], rows_s], axis=1)[:, -keep:][None]
    kv_out_p = []
    for (win, _), kv in zip(DWA_GROUPS, kv_p):
        k = min(win, T)
        kv_out_p.append(kv[:, T - k:].reshape(1, B, k, 2, DWA_HEADS, DWA_DH))
    kv_out_s = [kv.reshape(1, N, 1, 2, DWA_HEADS, DWA_DH) for kv in kv_s]
    return (y_p, y_s.reshape(N, 1, D_MODEL), conv_p, s_p[None], *kv_out_p,
            conv_s, s_s[None], *kv_out_s)
```

```python
import functools

import jax
import jax.numpy as jnp
import numpy as np
from jax import lax
from jax.experimental import pallas as pl
from jax.experimental.pallas import tpu as pltpu

F32 = jnp.float32
BF16 = jnp.bfloat16

D_MODEL = 2048
D_FF = 5632
N_MOD = 9
NORM_EPS = 1e-6
PAST_LEN = 16384

GDN_HEADS = 8
GDN_DK = 128
GDN_CONV = 4
GDN_CHUNK = 64
GDN_QK = GDN_HEADS * GDN_DK
CONV_CH = 3 * GDN_QK

DWA_GROUPS = ((128, 1), (512, 4), (2048, 16))
DWA_HEADS = 4
DWA_DH = 128
DWA_GW = DWA_HEADS * DWA_DH
DWA_WIDTH = len(DWA_GROUPS) * DWA_GW
DWA_BLOCK = 128
ROPE_THETA = 10000.0

LANES = 128
MXU_COLS = 256
VMEM_LIMIT = 56 * 1024 * 1024

_C_Z = CONV_CH
_C_B = _C_Z + GDN_QK
_C_A = _C_B + GDN_HEADS
_C_Q = _C_A + GDN_HEADS
_C_K = _C_Q + DWA_WIDTH
_C_V = _C_K + DWA_WIDTH
_C_GA = _C_V + DWA_WIDTH
_C_GB = _C_GA + D_MODEL

_HEAD_ORDER = (0, 2, 4, 6, 1, 3, 5, 7)
_N_PAIRS = GDN_HEADS // 2


def _sigmoid(x):
    return 0.5 * jnp.tanh(0.5 * x) + 0.5


def _silu(x):
    return x * _sigmoid(x)


def _dot(a, b):
    return jnp.dot(a, b, preferred_element_type=F32)


def _dot_nt(a, b):
    return lax.dot_general(a, b, (((1,), (1,)), ((), ())), preferred_element_type=F32)


def _params(*sem):
    return pltpu.CompilerParams(dimension_semantics=sem, vmem_limit_bytes=VMEM_LIMIT)


def _ada_kernel(c_ref, w_ref, b_ref, o_ref):
    a = _silu(c_ref[...]).astype(BF16)
    o_ref[...] = _dot(a, w_ref[...].astype(BF16)) + b_ref[...]


def _ada(c, w, b):
    m, n = c.shape[0], w.shape[1]
    tn = 1024
    return pl.pallas_call(
        _ada_kernel,
        out_shape=jax.ShapeDtypeStruct((m, n), F32),
        grid=(n // tn,),
        in_specs=[pl.BlockSpec((m, D_MODEL), lambda j: (0, 0)),
                  pl.BlockSpec((D_MODEL, tn), lambda j: (0, j)),
                  pl.BlockSpec((1, tn), lambda j: (0, j))],
        out_specs=pl.BlockSpec((m, tn), lambda j: (0, j)),
        compiler_params=_params("arbitrary"),
        name="ada_mod",
    )(c, w, b)


def _rms(x):
    return x * lax.rsqrt(jnp.mean(x * x, axis=-1, keepdims=True) + NORM_EPS)


def _modulated(h, gain, scale, shift):
    return ((_rms(h) * gain) * (1.0 + scale) + shift).astype(BF16)


def _swiglu_step(xm, wg_ref, wu_ref, wd_ref, acc_sc):
    g = _dot(xm, wg_ref[...])
    u = _dot(xm, wu_ref[...])
    acc_sc[...] += _dot((_silu(g) * u).astype(BF16), wd_ref[...])


def _ffn_first_kernel(x_ref, sh_ref, sc_ref, gt_ref, gain_ref, wg_ref, wu_ref, wd_ref,
                      nsh_ref, nsc_ref, ngain_ref, out_ref, nxt_ref, xm_sc, acc_sc):
    j = pl.program_id(2)

    @pl.when(j == 0)
    def _():
        xm_sc[...] = _modulated(x_ref[0], gain_ref[...], sc_ref[0], sh_ref[0])
        acc_sc[...] = jnp.zeros_like(acc_sc)

    _swiglu_step(xm_sc[...], wg_ref, wu_ref, wd_ref, acc_sc)

    @pl.when(j == pl.num_programs(2) - 1)
    def _():
        h = x_ref[0] + 0.5 * gt_ref[0] * acc_sc[...]
        out_ref[0] = h
        nxt_ref[0] = _modulated(h, ngain_ref[...], nsc_ref[0], nsh_ref[0])


def _ffn_last_kernel(x_ref, xm_ref, gt_ref, wg_ref, wu_ref, wd_ref, ngain_ref, out_ref, acc_sc):
    j = pl.program_id(2)

    @pl.when(j == 0)
    def _():
        acc_sc[...] = jnp.zeros_like(acc_sc)

    _swiglu_step(xm_ref[0], wg_ref, wu_ref, wd_ref, acc_sc)

    @pl.when(j == pl.num_programs(2) - 1)
    def _():
        out_ref[0] = _rms(x_ref[0] + 0.5 * gt_ref[0] * acc_sc[...]) * ngain_ref[...]


FFN_TM = 512
FFN_TF = 512


def _mod_spec(mod3, R, tm, k):
    if mod3.shape[1] == R:
        return pl.BlockSpec((1, tm, D_MODEL), lambda g, i, *_: (g, i, k))
    return pl.BlockSpec((1, 1, D_MODEL), lambda g, i, *_: (g, 0, k))


def _ffn_specs(R):
    tm = min(FFN_TM, R)
    nff = D_FF // FFN_TF
    rows = pl.BlockSpec((1, tm, D_MODEL), lambda g, i, j: (g, i, 0))
    vec = pl.BlockSpec((1, D_MODEL), lambda g, i, j: (0, 0))
    weights = [pl.BlockSpec((D_MODEL, FFN_TF), lambda g, i, j: (0, j)),
               pl.BlockSpec((D_MODEL, FFN_TF), lambda g, i, j: (0, j + nff)),
               pl.BlockSpec((FFN_TF, D_MODEL), lambda g, i, j: (j, 0))]
    return tm, nff, rows, vec, weights


def _ffn_first(x3, mod3, gain, w_up, w_down, next_gain):
    G, R, _ = x3.shape
    tm, nff, rows, vec, weights = _ffn_specs(R)
    ms = [_mod_spec(mod3, R, tm, k) for k in range(5)]
    return pl.pallas_call(
        _ffn_first_kernel,
        out_shape=[jax.ShapeDtypeStruct((G, R, D_MODEL), F32), jax.ShapeDtypeStruct((G, R, D_MODEL), BF16)],
        grid=(G, R // tm, nff),
        in_specs=[rows, ms[0], ms[1], ms[2], vec] + weights + [ms[3], ms[4], vec],
        out_specs=[rows, rows],
        scratch_shapes=[pltpu.VMEM((tm, D_MODEL), BF16), pltpu.VMEM((tm, D_MODEL), F32)],
        compiler_params=_params("parallel", "parallel", "arbitrary"),
        name="ffn_first",
    )(x3, mod3, mod3, mod3, gain.reshape(1, D_MODEL), w_up, w_up, w_down, mod3, mod3,
      next_gain.reshape(1, D_MODEL))


def _ffn_last(x3, xm3, mod3, w_up, w_down, final_gain):
    G, R, _ = x3.shape
    tm, nff, rows, vec, weights = _ffn_specs(R)
    return pl.pallas_call(
        _ffn_last_kernel,
        out_shape=jax.ShapeDtypeStruct((G, R, D_MODEL), F32),
        grid=(G, R // tm, nff),
        in_specs=[rows, rows, _mod_spec(mod3, R, tm, 8)] + weights + [vec],
        out_specs=rows,
        scratch_shapes=[pltpu.VMEM((tm, D_MODEL), F32)],
        compiler_params=_params("parallel", "parallel", "arbitrary"),
        name="ffn_last",
    )(x3, xm3, mod3, w_up, w_up, w_down, final_gain.reshape(1, D_MODEL))


def _rope_tile(x, cosf, sinf):
    return x * cosf + pltpu.roll(x, DWA_DH // 2, axis=1) * sinf


TAIL_ROWS = 8


def _proj_kernel(a_ref, w_ref, *rest, rope, tn, tail):
    if rope == "none":
        acc = _dot(a_ref[0], w_ref[...])
        rest[0][0] = acc.astype(rest[0].dtype)
        if tail:
            rest[1][0] = acc[acc.shape[0] - TAIL_ROWS:]
        return
    cos_ref, sin_ref, o_ref = rest
    cosf, sinf = cos_ref[...], sin_ref[...]
    keep_plain = None if rope == "all" else pl.program_id(2) != 0
    a = a_ref[0]
    for c in range(0, tn, MXU_COLS):
        acc = _dot(a, w_ref[:, c:c + MXU_COLS])
        for h in range(c, c + MXU_COLS, DWA_DH):
            x = acc[:, h - c:h - c + DWA_DH]
            r = _rope_tile(x, cosf, sinf)
            if keep_plain is not None:
                r = jnp.where(keep_plain, x, r)
            o_ref[0, :, h:h + DWA_DH] = r.astype(o_ref.dtype)


PROJ_TM = 1024


def _proj(xm3, w, out_dtype, *, rope="none", tables=None, tn=512, tail=False, name="proj"):
    G, R, _ = xm3.shape
    n = w.shape[1]
    tm = min(PROJ_TM, R)
    in_specs = [pl.BlockSpec((1, tm, D_MODEL), lambda g, i, j: (g, i, 0)),
                pl.BlockSpec((D_MODEL, tn), lambda g, i, j: (0, j))]
    args = [xm3, w]
    if rope != "none":
        tab = pl.BlockSpec((tm, DWA_DH), lambda g, i, j: (i, 0))
        in_specs += [tab, tab]
        args += list(tables)
    out_shape = jax.ShapeDtypeStruct((G, R, n), out_dtype)
    out_specs = pl.BlockSpec((1, tm, tn), lambda g, i, j: (g, i, j))
    if tail:
        out_shape = [out_shape, jax.ShapeDtypeStruct((G, R // tm * TAIL_ROWS, n), F32)]
        out_specs = [out_specs, pl.BlockSpec((1, TAIL_ROWS, tn), lambda g, i, j: (g, i, j))]
    res = pl.pallas_call(
        functools.partial(_proj_kernel, rope=rope, tn=tn, tail=tail),
        out_shape=out_shape,
        grid=(G, R // tm, n // tn),
        in_specs=in_specs,
        out_specs=out_specs,
        compiler_params=_params("parallel", "parallel", "arbitrary"),
        name=name,
    )(*args)
    if tail:
        return res[0], res[1][:, -TAIL_ROWS:]
    return res


def _rope_tables(pos):
    half = DWA_DH // 2
    inv = jnp.power(ROPE_THETA, -jnp.arange(half, dtype=F32) / half)
    ang = pos.astype(F32)[:, None] * inv[None, :]
    cos, sin = jnp.cos(ang), jnp.sin(ang)
    return jnp.concatenate([cos, cos], axis=-1), jnp.concatenate([-sin, sin], axis=-1)


def _gdn_gates(ba, par):
    beta = _sigmoid(ba)
    x = ba + par[1:2]
    softplus = jnp.maximum(x, 0.0) + jnp.log1p(jnp.exp(-jnp.abs(x)))
    return beta, -jnp.exp(par[0:1]) * softplus


def _gated_out(o, z, gn):
    return (_rms(o) * gn) * _silu(z.astype(F32))


def _l2norm(t):
    return t * lax.rsqrt(jnp.sum(t * t, axis=-1, keepdims=True) + NORM_EPS)


def _gdn_frontend(hist, raw, cw_ref, buf_sc, x_out):
    C = GDN_CHUNK
    buf_sc[0:8, :] = hist
    buf_sc[8:8 + C, :] = raw
    rows = buf_sc[...]
    acc = rows[8:] * cw_ref[GDN_CONV - 1:GDN_CONV, :]
    for s in range(1, GDN_CONV):
        acc = acc + pltpu.roll(rows, s, axis=0)[8:] * cw_ref[GDN_CONV - 1 - s:GDN_CONV - s, :]
    x = _silu(acc)
    for h in range(2 * GDN_HEADS):
        sl = slice(h * GDN_DK, (h + 1) * GDN_DK)
        x_out[:, sl] = _l2norm(x[:, sl]) * (GDN_DK ** -0.5 if h < GDN_HEADS else 1.0)
    x_out[:, 2 * GDN_QK:] = x[:, 2 * GDN_QK:]


def _gdn_chunk_kernel(cur_ref, nxt_ref, z_ref, ba_ref, cw_ref, par_ref, gn_ref, o_ref, s_ref,
                      buf_sc, xa_sc, xb_sc, S_sc):
    C = GDN_CHUNK
    c = pl.program_id(1)

    @pl.when(c == 0)
    def _():
        S_sc[...] = jnp.zeros_like(S_sc)
        _gdn_frontend(jnp.zeros((8, CONV_CH), F32), cur_ref[0].astype(F32), cw_ref, buf_sc, xa_sc)

    def step(x_sc, x_next):
        _gdn_frontend(cur_ref[0, C - 8:C, :].astype(F32), nxt_ref[0].astype(F32), cw_ref, buf_sc, x_next)
        _gdn_chain(x_sc, z_ref, ba_ref, par_ref, gn_ref, o_ref, S_sc)

    pl.when(c % 2 == 0)(lambda: step(xa_sc, xb_sc))
    pl.when(c % 2 == 1)(lambda: step(xb_sc, xa_sc))

    @pl.when(c == pl.num_programs(1) - 1)
    def _():
        s_ref[0] = S_sc[...]


def _gdn_chain(x_sc, z_ref, ba_ref, par_ref, gn_ref, o_ref, S_sc):
    C = GDN_CHUNK
    beta_t, g_t = _gdn_gates(ba_ref[0], par_ref[...])
    row = lax.broadcasted_iota(jnp.int32, (C, LANES), 0)
    G = g_t
    s = 1
    while s < C:
        G = G + jnp.where(row >= s, pltpu.roll(G, s, axis=0), 0.0)
        s *= 2
    shift = LANES - _N_PAIRS
    Gs = jnp.concatenate([G, pltpu.roll(G, shift, axis=1)], axis=0)
    Bs = jnp.concatenate([beta_t, pltpu.roll(beta_t, shift, axis=1)], axis=0)
    GT = Gs.T
    r2 = lax.broadcasted_iota(jnp.int32, (2 * C, LANES), 0)
    Glast = jnp.where(r2 < C, Gs[C - 1:C, :], Gs[2 * C - 1:2 * C, :])
    eG = jnp.exp(Gs)
    eGl = jnp.exp(Glast - Gs)
    egl = jnp.exp(Glast)

    ii = lax.broadcasted_iota(jnp.int32, (2 * C, 2 * C), 0)
    jj = lax.broadcasted_iota(jnp.int32, (2 * C, 2 * C), 1)
    same = (ii // C) == (jj // C)
    strict = same & (ii > jj)
    diag = ii == jj
    blk = (ii // 16) == (jj // 16)
    top = r2 < C
    gn = gn_ref[...]

    P = range(_N_PAIRS)
    C2 = 2 * C

    def pair(p, off):
        a = x_sc[:, off + (2 * p) * GDN_DK: off + (2 * p + 1) * GDN_DK]
        b = x_sc[:, off + (2 * p + 1) * GDN_DK: off + (2 * p + 2) * GDN_DK]
        return jnp.concatenate([a, b], axis=0)

    def col(t, p, base=0):
        return t[:, base + p:base + p + 1]

    q2 = [pair(p, 0) for p in P]
    k2 = [pair(p, GDN_QK) for p in P]
    kb = [k2[p] * col(Bs, p) for p in P]
    vb = [pair(p, 2 * GDN_QK) * col(Bs, p) for p in P]
    dec = [jnp.where(strict, jnp.exp(jnp.where(strict, col(Gs, p, 8) - GT[8 + p:9 + p, :], 0.0)), 0.0) for p in P]
    kq = [_dot_nt(jnp.concatenate([kb[p], q2[p]], axis=0).astype(BF16), k2[p].astype(BF16)) for p in P]
    A = [kq[p][:C2] * dec[p] for p in P]
    qk = [(kq[p][C2:] * (dec[p] + jnp.where(diag, 1.0, 0.0))).astype(BF16) for p in P]

    Dg = [jnp.where(blk, A[p], 0.0) for p in P]
    E = [A[p] - Dg[p] for p in P]
    Q = [-Dg[p] for p in P]
    Dgb = [Dg[p].astype(BF16) for p in P]
    Dp = [_dot(Dgb[p], Dgb[p]) for p in P]
    for _ in range(2):
        Dpb = [Dp[p].astype(BF16) for p in P]
        st = [_dot(jnp.concatenate([Dpb[p], Q[p].astype(BF16)], axis=0), Dpb[p]) for p in P]
        Q = [Q[p] + Dp[p] + st[p][C2:] for p in P]
        Dp = [st[p][:C2] for p in P]
    Q = [Q[p] + Dp[p] + _dot(Q[p].astype(BF16), Dp[p].astype(BF16)) for p in P]
    rhs = [jnp.concatenate([kb[p] * col(eG, p, 8), vb[p]], axis=1) for p in P]
    er = [_dot(Q[p].astype(BF16), jnp.concatenate([E[p], rhs[p]], axis=1).astype(BF16)) for p in P]
    N = [E[p] + er[p][:, :C2] for p in P]
    y = [rhs[p] + er[p][:, C2:] for p in P]
    Nb = [N[p].astype(BF16) for p in P]
    ny = [_dot(Nb[p], jnp.concatenate([N[p], y[p]], axis=1).astype(BF16)) for p in P]
    zz = [y[p] - ny[p][:, C2:] for p in P]
    sol = [zz[p] + _dot(ny[p][:, :C2].astype(BF16), zz[p].astype(BF16)) for p in P]

    wq = [jnp.concatenate([sol[p][:, :GDN_DK], q2[p] * col(eG, p, 8)], axis=0).astype(BF16) for p in P]
    S_old = [S_sc[h] for h in range(GDN_HEADS)]
    rS = [_dot(wq[h // 2], S_old[h].astype(BF16)) for h in range(GDN_HEADS)]
    vn = [(sol[p][:, GDN_DK:] - jnp.where(top, rS[2 * p][:C2], rS[2 * p + 1][:C2])).astype(BF16) for p in P]
    kdT = [(k2[p] * col(eGl, p, 8)).T.astype(BF16) for p in P]
    zero = jnp.zeros((C2, GDN_DK), BF16)
    for p in P:
        o2 = jnp.where(top, rS[2 * p][C2:], rS[2 * p + 1][C2:]) + _dot(qk[p], vn[p])
        S_sc[2 * p] = S_old[2 * p] * egl[0:1, 8 + p:9 + p] + _dot(kdT[p], jnp.where(top, vn[p], zero))
        S_sc[2 * p + 1] = S_old[2 * p + 1] * egl[C:C + 1, 8 + p:9 + p] + _dot(kdT[p], jnp.where(top, zero, vn[p]))
        for e in range(2):
            h = 2 * p + e
            sl = slice(h * GDN_DK, (h + 1) * GDN_DK)
            o_ref[0, :, sl] = _gated_out(o2[e * C:(e + 1) * C], z_ref[0, :, sl], gn).astype(o_ref.dtype)


def _gdn_prompt(az, ba, conv_w, par, gn):
    B, T, _ = az.shape
    C = GDN_CHUNK
    return pl.pallas_call(
        _gdn_chunk_kernel,
        out_shape=[jax.ShapeDtypeStruct((B, T, GDN_QK), BF16),
                   jax.ShapeDtypeStruct((B, GDN_HEADS, GDN_DK, GDN_DK), F32)],
        grid=(B, T // C),
        in_specs=[pl.BlockSpec((1, C, CONV_CH), lambda b, c: (b, c, 0)),
                  pl.BlockSpec((1, C, CONV_CH), lambda b, c: (b, jnp.minimum(c + 1, T // C - 1), 0)),
                  pl.BlockSpec((1, C, GDN_QK), lambda b, c: (b, c, CONV_CH // GDN_QK)),
                  pl.BlockSpec((1, C, LANES), lambda b, c: (b, c, 0)),
                  pl.BlockSpec((GDN_CONV, CONV_CH), lambda b, c: (0, 0)),
                  pl.BlockSpec((8, LANES), lambda b, c: (0, 0)),
                  pl.BlockSpec((1, GDN_DK), lambda b, c: (0, 0))],
        out_specs=[pl.BlockSpec((1, C, GDN_QK), lambda b, c: (b, c, 0)),
                   pl.BlockSpec((1, GDN_HEADS, GDN_DK, GDN_DK), lambda b, c: (b, 0, 0, 0))],
        scratch_shapes=[pltpu.VMEM((C + 8, CONV_CH), F32), pltpu.VMEM((C, CONV_CH), F32),
                        pltpu.VMEM((C, CONV_CH), F32), pltpu.VMEM((GDN_HEADS, GDN_DK, GDN_DK), F32)],
        compiler_params=_params("parallel", "arbitrary"),
        name="gdn_chunked",
    )(az, az, az, ba, conv_w, par, gn)


def _gdn_step_kernel(qkv_ref, hist_ref, z_ref, ba_ref, s0_ref, cw_ref, par_ref, gn_ref, o_ref, s_ref):
    acc = qkv_ref[0] * cw_ref[GDN_CONV - 1:GDN_CONV, :]
    for j in range(GDN_CONV - 1):
        acc = acc + hist_ref[0, j:j + 1, :] * cw_ref[j:j + 1, :]
    x = _silu(acc)
    beta_t, g_t = _gdn_gates(ba_ref[0], par_ref[...])
    eg_t = jnp.exp(g_t)

    def head(off, h):
        return x[:, off + h * GDN_DK: off + (h + 1) * GDN_DK]

    rows = []
    for off, scale in ((GDN_QK, 1.0), (0, GDN_DK ** -0.5)):
        for h in range(GDN_HEADS):
            t = head(off, h)
            rows.append(t * lax.rsqrt(jnp.sum(t * t, axis=-1, keepdims=True) + NORM_EPS) * scale)
    ri = lax.broadcasted_iota(jnp.int32, (LANES, LANES), 0)
    tile = jnp.zeros((LANES, LANES), F32)
    for r, t in enumerate(rows):
        tile = jnp.where(ri == r, t, tile)
    cols = tile.T
    gn = gn_ref[...]
    for h in range(GDN_HEADS):
        lane = _HEAD_ORDER.index(h)
        kcol = cols[:, h:h + 1]
        qcol = cols[:, GDN_HEADS + h:GDN_HEADS + h + 1]
        S = s0_ref[0, h] * eg_t[:, 8 + lane:9 + lane]
        kv = jnp.sum(kcol * S, axis=0, keepdims=True)
        delta = beta_t[:, lane:lane + 1] * (head(2 * GDN_QK, h) - kv)
        S = S + kcol * delta
        s_ref[0, h] = S
        o = jnp.sum(qcol * S, axis=0, keepdims=True)
        sl = slice(h * GDN_DK, (h + 1) * GDN_DK)
        o_ref[0, :, sl] = _gated_out(o, z_ref[0, :, sl], gn).astype(o_ref.dtype)


def _gdn_sample(az, ba, hist, s0, conv_w, par, gn):
    N = az.shape[0]
    return pl.pallas_call(
        _gdn_step_kernel,
        out_shape=[jax.ShapeDtypeStruct((N, 1, GDN_QK), BF16),
                   jax.ShapeDtypeStruct((N, GDN_HEADS, GDN_DK, GDN_DK), F32)],
        grid=(N,),
        in_specs=[pl.BlockSpec((1, 1, CONV_CH), lambda b: (b, 0, 0)),
                  pl.BlockSpec((1, GDN_CONV - 1, CONV_CH), lambda b: (b, 0, 0)),
                  pl.BlockSpec((1, 1, GDN_QK), lambda b: (b, 0, CONV_CH // GDN_QK)),
                  pl.BlockSpec((1, 1, LANES), lambda b: (b, 0, 0)),
                  pl.BlockSpec((1, GDN_HEADS, GDN_DK, GDN_DK), lambda b: (b, 0, 0, 0)),
                  pl.BlockSpec((GDN_CONV, CONV_CH), lambda b: (0, 0)),
                  pl.BlockSpec((8, LANES), lambda b: (0, 0)),
                  pl.BlockSpec((1, GDN_DK), lambda b: (0, 0))],
        out_specs=[pl.BlockSpec((1, 1, GDN_QK), lambda b: (b, 0, 0)),
                   pl.BlockSpec((1, GDN_HEADS, GDN_DK, GDN_DK), lambda b: (b, 0, 0, 0))],
        compiler_params=_params("parallel"),
        name="gdn_step",
    )(az, hist, az, ba, s0, conv_w, par, gn)


DWA_BATCH = 4


def _dwa_blocks(refs, blocks, masks, first, scratch):
    q_ref, k_ref, v_ref = refs
    m_sc, l_sc, acc_sc = scratch
    cur_ok, prev_ok = masks
    scale = DWA_DH ** -0.5
    BL = DWA_BLOCK
    qb = [q_ref[0, cur, :].astype(BF16) for cur, _, _ in blocks]
    sc = [jnp.where(cur_ok, _dot_nt(qb[i], k_ref[0, cur, :].astype(BF16)) * scale, -jnp.inf)
          for i, (cur, _, _) in enumerate(blocks)]
    sp = [None if prev is None else
          jnp.where(prev_ok & on, _dot_nt(qb[i], k_ref[0, prev, :].astype(BF16)) * scale, -jnp.inf)
          for i, (_, prev, on) in enumerate(blocks)]
    m, l, pc, pp = [], [], [], []
    for i in range(len(blocks)):
        mi = jnp.max(sc[i], axis=-1, keepdims=True)
        if sp[i] is not None:
            mi = jnp.maximum(mi, jnp.max(sp[i], axis=-1, keepdims=True))
        p = jnp.exp(sc[i] - mi)
        li = jnp.sum(p, axis=-1, keepdims=True)
        pc.append(p.astype(BF16))
        if sp[i] is not None:
            p = jnp.exp(sp[i] - mi)
            li = li + jnp.sum(p, axis=-1, keepdims=True)
            pp.append(p.astype(BF16))
        else:
            pp.append(None)
        m.append(mi)
        l.append(li)
    acc = [_dot(pc[i], v_ref[0, cur, :].astype(BF16)) for i, (cur, _, _) in enumerate(blocks)]
    acc = [a if pp[i] is None else a + _dot(pp[i], v_ref[0, blocks[i][1], :].astype(BF16))
           for i, a in enumerate(acc)]
    for i, (cur, _, _) in enumerate(blocks):
        if first:
            m_sc[cur, :] = jnp.broadcast_to(m[i], (BL, LANES))
            l_sc[cur, :] = jnp.broadcast_to(l[i], (BL, LANES))
            acc_sc[cur, :] = acc[i]
        else:
            m_old = m_sc[cur, :]
            m_new = jnp.maximum(m_old, m[i])
            a = jnp.exp(m_old - m_new)
            b = jnp.exp(m[i] - m_new)
            m_sc[cur, :] = m_new
            l_sc[cur, :] = a * l_sc[cur, :] + b * l[i]
            acc_sc[cur, :] = a * acc_sc[cur, :] + b * acc[i]


def _dwa_kernel(q0_ref, q1_ref, q2_ref, k0_ref, v0_ref, k1_ref, v1_ref, k2_ref, v2_ref, o_ref, m_sc, l_sc, acc_sc):
    BL = DWA_BLOCK
    T = o_ref.shape[1]
    ri = lax.broadcasted_iota(jnp.int32, (BL, BL), 0)
    ci = lax.broadcasted_iota(jnp.int32, (BL, BL), 1)
    masks = (ri >= ci, ci >= ri)
    q_refs, k_refs, v_refs = (q0_ref, q1_ref, q2_ref), (k0_ref, k1_ref, k2_ref), (v0_ref, v1_ref, v2_ref)
    scratch = (m_sc, l_sc, acc_sc)

    for gi, (_, dil) in enumerate(DWA_GROUPS):
        refs = (q_refs[gi], k_refs[gi], v_refs[gi])
        nb = T // dil // BL

        def rows(blk, r, dil=dil):
            start = blk * (BL * dil) + r
            if dil == 1:
                return pl.ds(pl.multiple_of(start, BL), BL)
            return pl.ds(start, BL, stride=dil)

        def block(n, r, nb=nb, rows=rows):
            if nb == 1:
                return rows(n, r), None, None
            return rows(n, r), rows(jnp.maximum(n - 1, 0), r), n > 0

        if dil >= DWA_BATCH:
            for r0 in range(0, dil, DWA_BATCH):
                def body(n, carry, r0=r0, refs=refs, block=block, first=gi == 0):
                    _dwa_blocks(refs, [block(n, r0 + t) for t in range(DWA_BATCH)], masks, first, scratch)
                    return carry
                if nb == 1:
                    body(0, 0)
                else:
                    lax.fori_loop(0, nb, body, 0)
        else:
            assert dil == 1 and nb % DWA_BATCH == 0

            def body(i, carry, refs=refs, block=block, first=gi == 0):
                _dwa_blocks(refs, [block(i * DWA_BATCH + t, 0) for t in range(DWA_BATCH)], masks, first, scratch)
                return carry
            lax.fori_loop(0, nb // DWA_BATCH, body, 0)

    o_ref[0] = (acc_sc[...] / l_sc[...]).astype(o_ref.dtype)


def _dwa_prompt(q, kvs):
    B, T, _ = q.shape

    def col(c):
        return pl.BlockSpec((1, T, DWA_DH), lambda b, h: (b, 0, c(h)))

    ng = len(DWA_GROUPS)
    in_specs = [col(lambda h, g=g: g * DWA_HEADS + h) for g in range(ng)]
    args = [q] * ng
    for g in range(ng):
        in_specs += [col(lambda h: h), col(lambda h: DWA_HEADS + h)]
        args += [kvs[g], kvs[g]]
    return pl.pallas_call(
        _dwa_kernel,
        out_shape=jax.ShapeDtypeStruct((B, T, DWA_GW), BF16),
        grid=(B, DWA_HEADS),
        in_specs=in_specs,
        out_specs=col(lambda h: h),
        scratch_shapes=[pltpu.VMEM((T, LANES), F32), pltpu.VMEM((T, LANES), F32), pltpu.VMEM((T, DWA_DH), F32)],
        compiler_params=_params("parallel", "parallel"),
        name="dwa_prompt",
    )(*args)


def _dwa_step_kernel(q_ref, n0_ref, n1_ref, n2_ref, c0_ref, c1_ref, c2_ref, o_ref):
    new_refs = (n0_ref, n1_ref, n2_ref)
    cache_refs = (c0_ref, c1_ref, c2_ref)
    scale = DWA_DH ** -0.5
    parts = []
    for gi in range(len(DWA_GROUPS)):
        q = q_ref[0, gi * DWA_HEADS:(gi + 1) * DWA_HEADS, :]
        k = cache_refs[gi][0, :, 0, 0]
        v = cache_refs[gi][0, :, 0, 1]
        s = jnp.sum(k * q[None], axis=-1, keepdims=True) * scale
        s_new = jnp.sum(new_refs[gi][0, 0] * q, axis=-1, keepdims=True) * scale
        m = jnp.maximum(jnp.max(s, axis=0), s_new)
        p = jnp.exp(s - m[None])
        p_new = jnp.exp(s_new - m)
        l = jnp.sum(p, axis=0) + p_new
        acc = jnp.sum(p * v, axis=0) + p_new * new_refs[gi][0, 1]
        parts.append((m, l, acc))
    mm = jnp.maximum(jnp.maximum(parts[0][0], parts[1][0]), parts[2][0])
    num = jnp.zeros((DWA_HEADS, DWA_DH), F32)
    den = jnp.zeros((DWA_HEADS, 1), F32)
    for m, l, acc in parts:
        e = jnp.exp(m - mm)
        num = num + e * acc
        den = den + e * l
    o_ref[0] = num / den


def _dwa_sample(q, kv_new, caches):
    N = q.shape[0]
    span = DWA_BLOCK
    in_specs = [pl.BlockSpec((1, len(DWA_GROUPS) * DWA_HEADS, DWA_DH), lambda b: (b, 0, 0))]
    in_specs += [pl.BlockSpec((1, 2, DWA_HEADS, DWA_DH), lambda b: (b, 0, 0, 0))] * 3
    views = []
    for (win, dil), cache in zip(DWA_GROUPS, caches):
        L = cache.shape[1]
        assert L == win and L // dil == span
        views.append(cache.reshape(N, span, dil, 2, DWA_HEADS, DWA_DH))
        in_specs.append(pl.BlockSpec((1, span, 1, 2, DWA_HEADS, DWA_DH), lambda b: (b, 0, 0, 0, 0, 0)))
    return pl.pallas_call(
        _dwa_step_kernel,
        out_shape=jax.ShapeDtypeStruct((N, DWA_HEADS, DWA_DH), F32),
        grid=(N,),
        in_specs=in_specs,
        out_specs=pl.BlockSpec((1, DWA_HEADS, DWA_DH), lambda b: (b, 0, 0)),
        compiler_params=_params("parallel"),
        name="dwa_step",
    )(q, *kv_new, *views)


def _mix_out_kernel(oa_ref, ob_ref, ga_ref, gb_ref, h_ref, gt_ref, nsh_ref, nsc_ref, ngain_ref,
                    wa_ref, wb_ref, wo_ref, out_ref, nxt_ref):
    ta = _dot(oa_ref[0], wa_ref[...])
    tb = _dot(ob_ref[0], wb_ref[...])
    merged = _sigmoid(ga_ref[0].astype(F32)) * ta + _sigmoid(gb_ref[0].astype(F32)) * tb
    h = h_ref[0] + gt_ref[0] * _dot(merged.astype(BF16), wo_ref[...])
    out_ref[0] = h
    nxt_ref[0] = _modulated(h, ngain_ref[...], nsc_ref[0], nsh_ref[0])


MIX_TM = 512


def _mix_out(oa, ob, gates, h, mod3, wa, wb, wo, next_gain):
    G, R, _ = h.shape
    tm = min(MIX_TM, R)

    def rows(width, col=0):
        return pl.BlockSpec((1, tm, width), lambda g, i: (g, i, col))

    def whole(shape):
        return pl.BlockSpec(shape, lambda g, i: (0, 0), pipeline_mode=pl.Buffered(1))

    return pl.pallas_call(
        _mix_out_kernel,
        out_shape=[jax.ShapeDtypeStruct((G, R, D_MODEL), F32), jax.ShapeDtypeStruct((G, R, D_MODEL), BF16)],
        grid=(G, R // tm),
        in_specs=[rows(GDN_QK), rows(DWA_GW), rows(D_MODEL, 0), rows(D_MODEL, 1), rows(D_MODEL),
                  _mod_spec(mod3, R, tm, 5), _mod_spec(mod3, R, tm, 6), _mod_spec(mod3, R, tm, 7),
                  pl.BlockSpec((1, D_MODEL), lambda g, i: (0, 0)),
                  whole(wa.shape), whole(wb.shape), whole(wo.shape)],
        out_specs=[rows(D_MODEL), rows(D_MODEL)],
        compiler_params=_params("parallel", "parallel"),
        name="mix_out",
    )(oa, ob, gates, gates, h, mod3, mod3, mod3, next_gain.reshape(1, D_MODEL), wa, wb, wo)


def _prep_weights(w_in, conv_w, a_log, dt_bias, gdn_norm, w_proj_a, w_proj_b, w_out,
                  w_ffn1_up, w_ffn1_down, w_ffn2_up, w_ffn2_down):
    w = w_in[0]
    order = np.array(_HEAD_ORDER)
    ba_cols = np.concatenate([_C_B + order, _C_A + order])
    par = jnp.zeros((8, LANES), F32)
    par = par.at[0, 8:16].set(a_log[0][order]).at[1, 8:16].set(dt_bias[0][order])
    return dict(
        az=w[:, :_C_B].astype(BF16),
        ba=jnp.pad(w[:, ba_cols], ((0, 0), (0, LANES - 2 * GDN_HEADS))).astype(BF16),
        q=w[:, _C_Q:_C_K].astype(BF16),
        kv=[jnp.concatenate([w[:, _C_K + g * DWA_GW:_C_K + (g + 1) * DWA_GW],
                             w[:, _C_V + g * DWA_GW:_C_V + (g + 1) * DWA_GW]], axis=1).astype(BF16)
            for g in range(len(DWA_GROUPS))],
        gates=w[:, _C_GA:].astype(BF16),
        conv=conv_w[0], par=par, gn=gdn_norm,
        wa=w_proj_a[0].astype(BF16), wb=w_proj_b[0].astype(BF16), wo=w_out[0].astype(BF16),
        up1=w_ffn1_up[0].astype(BF16), down1=w_ffn1_down[0].astype(BF16),
        up2=w_ffn2_up[0].astype(BF16), down2=w_ffn2_down[0].astype(BF16),
    )


def _trunk(x3, mod3, pos, W, norms, *, sample_state=None):
    G, R, _ = x3.shape
    h1, xm = _ffn_first(x3, mod3, norms["ffn1"], W["up1"], W["down1"], norms["mix"])
    tables = _rope_tables(pos)
    ba = _proj(xm, W["ba"], F32, tn=LANES, name="proj_ba")
    gates = _proj(xm, W["gates"], BF16, tn=1024, name="proj_gates")
    kvs = [_proj(xm, W["kv"][g], F32, rope="first", tables=tables, name=f"proj_kv{g}")
           for g in range(len(DWA_GROUPS))]
    q = _proj(xm, W["q"], F32, rope="all", tables=tables, name="proj_q")
    if sample_state is None:
        az, conv_rows = _proj(xm, W["az"], BF16, tn=1024, tail=True, name="proj_az")
        o_a, s_new = _gdn_prompt(az, ba, W["conv"], W["par"], W["gn"])
        o_b = _dwa_prompt(q, kvs)
    else:
        hist, s0, caches = sample_state
        conv_rows = _proj(xm, W["az"], F32, tn=1024, name="proj_az").reshape(R, 1, -1)
        o_a, s_new = _gdn_sample(conv_rows, ba.reshape(R, 1, LANES), hist, s0, W["conv"], W["par"], W["gn"])
        o_b = _dwa_sample(q.reshape(R, len(DWA_GROUPS) * DWA_HEADS, DWA_DH),
                          [kv.reshape(R, 2, DWA_HEADS, DWA_DH) for kv in kvs], caches)
        o_a = o_a.reshape(1, R, -1)
        o_b = o_b.reshape(1, R, DWA_GW).astype(BF16)
    h2, xm2 = _mix_out(o_a, o_b, gates, h1, mod3, W["wa"], W["wb"], W["wo"], norms["ffn2"])
    y = _ffn_last(h2, xm2, mod3, W["up2"], W["down2"], norms["final"])
    return y, conv_rows[:, :, :CONV_CH], s_new, kvs


def kernel(x_prompt, x_sample, state_conv, state_delta, cache_kv_w128, cache_kv_w512, cache_kv_w2048, c_prompt, c_sample, w_ada, b_ada, norm_ffn1, w_ffn1_up, w_ffn1_down, norm_mix, w_in, conv_w, a_log, dt_bias, gdn_norm, w_proj_a, w_proj_b, w_out, norm_ffn2, w_ffn2_up, w_ffn2_down, norm_final):
    B, T, _ = x_prompt.shape
    N, S, _ = x_sample.shape
    assert S == 1 and T % (DWA_BLOCK * DWA_GROUPS[-1][1]) == 0
    W = _prep_weights(w_in, conv_w, a_log, dt_bias, gdn_norm, w_proj_a, w_proj_b, w_out,
                      w_ffn1_up, w_ffn1_down, w_ffn2_up, w_ffn2_down)
    norms = dict(ffn1=norm_ffn1[0], mix=norm_mix[0], ffn2=norm_ffn2[0], final=norm_final)

    mod = _ada(jnp.concatenate([c_prompt, c_sample], axis=0), w_ada[0], b_ada)
    mod_p = mod[:B].reshape(B, 1, N_MOD * D_MODEL)
    mod_s = mod[B:].reshape(1, N, N_MOD * D_MODEL)

    y_p, rows_p, s_p, kv_p = _trunk(x_prompt, mod_p, jnp.arange(T, dtype=jnp.int32), W, norms)
    caches = (cache_kv_w128[0], cache_kv_w512[0], cache_kv_w2048[0])
    y_s, rows_s, s_s, kv_s = _trunk(x_sample.reshape(1, N, D_MODEL), mod_s,
                                    jnp.full((N,), PAST_LEN, dtype=jnp.int32), W, norms,
                                    sample_state=(state_conv[0], state_delta[0], caches))

    keep = GDN_CONV - 1
    conv_p = rows_p[:, TAIL_ROWS - keep:][None]
    conv_s = jnp.concatenate([state_conv[0], rows_s], axis=1)[:, -keep:][None]
    kv_out_p = []
    for (win, _), kv in zip(DWA_GROUPS, kv_p):
        k = min(win, T)
        kv_out_p.append(kv[:, T - k:].reshape(1, B, k, 2, DWA_HEADS, DWA_DH))
    kv_out_s = [kv.reshape(1, N, 1, 2, DWA_HEADS, DWA_DH) for kv in kv_s]
    return (y_p, y_s.reshape(N, 1, D_MODEL), conv_p, s_p[None], *kv_out_p,
            conv_s, s_s[None], *kv_out_s)
```

```python
import functools

import jax
import jax.numpy as jnp
import numpy as np
from jax import lax
from jax.experimental import pallas as pl
from jax.experimental.pallas import tpu as pltpu

F32 = jnp.float32
BF16 = jnp.bfloat16

D_MODEL = 2048
D_FF = 5632
N_MOD = 9
NORM_EPS = 1e-6
PAST_LEN = 16384

GDN_HEADS = 8
GDN_DK = 128
GDN_CONV = 4
GDN_CHUNK = 64
GDN_QK = GDN_HEADS * GDN_DK
CONV_CH = 3 * GDN_QK

DWA_GROUPS = ((128, 1), (512, 4), (2048, 16))
DWA_HEADS = 4
DWA_DH = 128
DWA_GW = DWA_HEADS * DWA_DH
DWA_WIDTH = len(DWA_GROUPS) * DWA_GW
DWA_BLOCK = 128
ROPE_THETA = 10000.0

LANES = 128
MXU_COLS = 256
VMEM_LIMIT = 56 * 1024 * 1024

_C_Z = CONV_CH
_C_B = _C_Z + GDN_QK
_C_A = _C_B + GDN_HEADS
_C_Q = _C_A + GDN_HEADS
_C_K = _C_Q + DWA_WIDTH
_C_V = _C_K + DWA_WIDTH
_C_GA = _C_V + DWA_WIDTH
_C_GB = _C_GA + D_MODEL

_HEAD_ORDER = (0, 2, 4, 6, 1, 3, 5, 7)
_N_PAIRS = GDN_HEADS // 2


def _sigmoid(x):
    return 0.5 * jnp.tanh(0.5 * x) + 0.5


def _silu(x):
    return x * _sigmoid(x)


def _dot(a, b):
    return jnp.dot(a, b, preferred_element_type=F32)


def _dot_nt(a, b):
    return lax.dot_general(a, b, (((1,), (1,)), ((), ())), preferred_element_type=F32)


def _params(*sem):
    return pltpu.CompilerParams(dimension_semantics=sem, vmem_limit_bytes=VMEM_LIMIT)


def _ada_kernel(c_ref, w_ref, b_ref, o_ref):
    a = _silu(c_ref[...]).astype(BF16)
    o_ref[...] = _dot(a, w_ref[...].astype(BF16)) + b_ref[...]


def _ada(c, w, b):
    m, n = c.shape[0], w.shape[1]
    tn = 1024
    return pl.pallas_call(
        _ada_kernel,
        out_shape=jax.ShapeDtypeStruct((m, n), F32),
        grid=(n // tn,),
        in_specs=[pl.BlockSpec((m, D_MODEL), lambda j: (0, 0)),
                  pl.BlockSpec((D_MODEL, tn), lambda j: (0, j)),
                  pl.BlockSpec((1, tn), lambda j: (0, j))],
        out_specs=pl.BlockSpec((m, tn), lambda j: (0, j)),
        compiler_params=_params("arbitrary"),
        name="ada_mod",
    )(c, w, b)


def _rms(x):
    return x * lax.rsqrt(jnp.mean(x * x, axis=-1, keepdims=True) + NORM_EPS)


def _modulated(h, gain, scale, shift):
    return ((_rms(h) * gain) * (1.0 + scale) + shift).astype(BF16)


def _swiglu_step(xm, wg_ref, wu_ref, wd_ref, acc_sc):
    g = _dot(xm, wg_ref[...])
    u = _dot(xm, wu_ref[...])
    acc_sc[...] += _dot((_silu(g) * u).astype(BF16), wd_ref[...])


def _ffn_first_kernel(x_ref, sh_ref, sc_ref, gt_ref, gain_ref, wg_ref, wu_ref, wd_ref,
                      nsh_ref, nsc_ref, ngain_ref, *rest, n_cast):
    cast_in = rest[:n_cast]
    out_ref, nxt_ref = rest[n_cast:n_cast + 2]
    cast_out = rest[n_cast + 2:2 * n_cast + 2]
    xm_sc, acc_sc = rest[2 * n_cast + 2:]
    j = pl.program_id(2)

    @pl.when(j == 0)
    def _():
        xm_sc[...] = _modulated(x_ref[0], gain_ref[...], sc_ref[0], sh_ref[0])
        acc_sc[...] = jnp.zeros_like(acc_sc)

    _cast_blocks(cast_in, cast_out)
    _swiglu_step(xm_sc[...], wg_ref, wu_ref, wd_ref, acc_sc)

    @pl.when(j == pl.num_programs(2) - 1)
    def _():
        h = x_ref[0] + 0.5 * gt_ref[0] * acc_sc[...]
        out_ref[0] = h
        nxt_ref[0] = _modulated(h, ngain_ref[...], nsc_ref[0], nsh_ref[0])


def _ffn_last_kernel(x_ref, xm_ref, gt_ref, wg_ref, wu_ref, wd_ref, ngain_ref, out_ref, acc_sc):
    j = pl.program_id(2)

    @pl.when(j == 0)
    def _():
        acc_sc[...] = jnp.zeros_like(acc_sc)

    _swiglu_step(xm_ref[0], wg_ref, wu_ref, wd_ref, acc_sc)

    @pl.when(j == pl.num_programs(2) - 1)
    def _():
        out_ref[0] = _rms(x_ref[0] + 0.5 * gt_ref[0] * acc_sc[...]) * ngain_ref[...]


FFN_TM = 512
FFN_TF = 512


def _mod_spec(mod3, R, tm, k):
    if mod3.shape[1] == R:
        return pl.BlockSpec((1, tm, D_MODEL), lambda g, i, *_: (g, i, k))
    return pl.BlockSpec((1, 1, D_MODEL), lambda g, i, *_: (g, 0, k))


def _ffn_specs(R):
    tm = min(FFN_TM, R)
    nff = D_FF // FFN_TF
    rows = pl.BlockSpec((1, tm, D_MODEL), lambda g, i, j: (g, i, 0))
    vec = pl.BlockSpec((1, D_MODEL), lambda g, i, j: (0, 0))
    weights = [pl.BlockSpec((D_MODEL, FFN_TF), lambda g, i, j: (0, j)),
               pl.BlockSpec((D_MODEL, FFN_TF), lambda g, i, j: (0, j + nff)),
               pl.BlockSpec((FFN_TF, D_MODEL), lambda g, i, j: (j, 0))]
    return tm, nff, rows, vec, weights


CAST_BLOCK = 128


def _cast_jobs(arrays, step_of, total_steps):
    in_specs, out_specs, out_shapes = [], [], []
    start = 0
    for arr, axis, extent, slab in arrays:
        assert extent % slab == 0 and slab % CAST_BLOCK == 0
        nblk = extent // slab
        shape = tuple(slab if d == axis else n for d, n in enumerate(arr.shape))

        def index(*grid, start=start, nblk=nblk, axis=axis):
            blk = jnp.clip(step_of(*grid) - start, 0, nblk - 1)
            return (blk, 0) if axis == 0 else (0, blk)

        in_specs.append(pl.BlockSpec(shape, index))
        out_specs.append(pl.BlockSpec(shape, index))
        out_shapes.append(jax.ShapeDtypeStruct(
            tuple(extent if d == axis else n for d, n in enumerate(arr.shape)), BF16))
        start += nblk
    assert start <= total_steps, (start, total_steps)
    return in_specs, out_specs, out_shapes


def _cast_blocks(cast_in, cast_out):
    for src, dst in zip(cast_in, cast_out):
        dst[...] = src[...].astype(BF16)


def _ffn_first(x3, mod3, gain, w_up, w_down, next_gain, cast=()):
    G, R, _ = x3.shape
    tm, nff, rows, vec, weights = _ffn_specs(R)
    ms = [_mod_spec(mod3, R, tm, k) for k in range(5)]
    c_in, c_out, c_shapes = _cast_jobs(cast, lambda g, i, j: (g * (R // tm) + i) * nff + j, G * (R // tm) * nff)
    sem = ("arbitrary",) * 3 if cast else ("parallel", "parallel", "arbitrary")
    return pl.pallas_call(
        functools.partial(_ffn_first_kernel, n_cast=len(cast)),
        out_shape=[jax.ShapeDtypeStruct((G, R, D_MODEL), F32), jax.ShapeDtypeStruct((G, R, D_MODEL), BF16)]
        + c_shapes,
        grid=(G, R // tm, nff),
        in_specs=[rows, ms[0], ms[1], ms[2], vec] + weights + [ms[3], ms[4], vec] + c_in,
        out_specs=[rows, rows] + c_out,
        scratch_shapes=[pltpu.VMEM((tm, D_MODEL), BF16), pltpu.VMEM((tm, D_MODEL), F32)],
        compiler_params=_params(*sem),
        name="ffn_first",
    )(x3, mod3, mod3, mod3, gain.reshape(1, D_MODEL), w_up, w_up, w_down, mod3, mod3,
      next_gain.reshape(1, D_MODEL), *[a for a, _, _, _ in cast])


def _ffn_last(x3, xm3, mod3, w_up, w_down, final_gain):
    G, R, _ = x3.shape
    tm, nff, rows, vec, weights = _ffn_specs(R)
    return pl.pallas_call(
        _ffn_last_kernel,
        out_shape=jax.ShapeDtypeStruct((G, R, D_MODEL), F32),
        grid=(G, R // tm, nff),
        in_specs=[rows, rows, _mod_spec(mod3, R, tm, 8)] + weights + [vec],
        out_specs=rows,
        scratch_shapes=[pltpu.VMEM((tm, D_MODEL), F32)],
        compiler_params=_params("parallel", "parallel", "arbitrary"),
        name="ffn_last",
    )(x3, xm3, mod3, w_up, w_up, w_down, final_gain.reshape(1, D_MODEL))


def _rope_tile(x, cosf, sinf):
    return x * cosf + pltpu.roll(x, DWA_DH // 2, axis=1) * sinf


TAIL_ROWS = 8


def _proj_kernel(a_ref, w_ref, *rest, rope, tn, tail):
    if rope == "none":
        acc = _dot(a_ref[0], w_ref[...])
        rest[0][0] = acc.astype(rest[0].dtype)
        if tail:
            rest[1][0] = acc[acc.shape[0] - TAIL_ROWS:]
        return
    cos_ref, sin_ref, o_ref = rest
    cosf, sinf = cos_ref[...], sin_ref[...]
    j = pl.program_id(2)
    keep_plain = None if rope == "all" else (j < ATTN_Q_TILE) & (j % 2 == 1)
    a = a_ref[0]
    for c in range(0, tn, MXU_COLS):
        acc = _dot(a, w_ref[:, c:c + MXU_COLS])
        for h in range(c, c + MXU_COLS, DWA_DH):
            x = acc[:, h - c:h - c + DWA_DH]
            r = _rope_tile(x, cosf, sinf)
            if keep_plain is not None:
                r = jnp.where(keep_plain, x, r)
            o_ref[0, :, h:h + DWA_DH] = r.astype(o_ref.dtype)


PROJ_TM = 1024
ATTN_Q_COL = 2 * DWA_WIDTH
ATTN_Q_TILE = ATTN_Q_COL // DWA_GW


def _proj(xm3, w, out_dtype, *, rope="none", tables=None, tn=512, tail=False, name="proj"):
    G, R, _ = xm3.shape
    n = w.shape[1]
    tm = min(PROJ_TM, R)
    in_specs = [pl.BlockSpec((1, tm, D_MODEL), lambda g, i, j: (g, i, 0)),
                pl.BlockSpec((D_MODEL, tn), lambda g, i, j: (0, j))]
    args = [xm3, w]
    if rope != "none":
        tab = pl.BlockSpec((tm, DWA_DH), lambda g, i, j: (i, 0))
        in_specs += [tab, tab]
        args += list(tables)
    out_shape = jax.ShapeDtypeStruct((G, R, n), out_dtype)
    out_specs = pl.BlockSpec((1, tm, tn), lambda g, i, j: (g, i, j))
    if tail:
        out_shape = [out_shape, jax.ShapeDtypeStruct((G, R // tm * TAIL_ROWS, n), F32)]
        out_specs = [out_specs, pl.BlockSpec((1, TAIL_ROWS, tn), lambda g, i, j: (g, i, j))]
    res = pl.pallas_call(
        functools.partial(_proj_kernel, rope=rope, tn=tn, tail=tail),
        out_shape=out_shape,
        grid=(G, R // tm, n // tn),
        in_specs=in_specs,
        out_specs=out_specs,
        compiler_params=_params("parallel", "parallel", "arbitrary"),
        name=name,
    )(*args)
    if tail:
        return res[0], res[1][:, -TAIL_ROWS:]
    return res


def _rope_tables(pos):
    half = DWA_DH // 2
    inv = jnp.power(ROPE_THETA, -jnp.arange(half, dtype=F32) / half)
    ang = pos.astype(F32)[:, None] * inv[None, :]
    cos, sin = jnp.cos(ang), jnp.sin(ang)
    return jnp.concatenate([cos, cos], axis=-1), jnp.concatenate([-sin, sin], axis=-1)


def _gdn_gates(ba, par):
    beta = _sigmoid(ba)
    x = ba + par[1:2]
    softplus = jnp.maximum(x, 0.0) + jnp.log1p(jnp.exp(-jnp.abs(x)))
    return beta, -jnp.exp(par[0:1]) * softplus


def _gated_out(o, z, gn):
    return (_rms(o) * gn) * _silu(z.astype(F32))


def _l2norm(t):
    return t * lax.rsqrt(jnp.sum(t * t, axis=-1, keepdims=True) + NORM_EPS)


def _gdn_frontend(hist, raw, cw_ref, buf_sc, x_out):
    C = GDN_CHUNK
    buf_sc[0:8, :] = hist
    buf_sc[8:8 + C, :] = raw
    rows = buf_sc[...]
    acc = rows[8:] * cw_ref[GDN_CONV - 1:GDN_CONV, :]
    for s in range(1, GDN_CONV):
        acc = acc + pltpu.roll(rows, s, axis=0)[8:] * cw_ref[GDN_CONV - 1 - s:GDN_CONV - s, :]
    x = _silu(acc)
    for h in range(2 * GDN_HEADS):
        sl = slice(h * GDN_DK, (h + 1) * GDN_DK)
        x_out[:, sl] = _l2norm(x[:, sl]) * (GDN_DK ** -0.5 if h < GDN_HEADS else 1.0)
    x_out[:, 2 * GDN_QK:] = x[:, 2 * GDN_QK:]


def _gdn_chunk_kernel(cur_ref, nxt_ref, z_ref, ba_ref, cw_ref, par_ref, gn_ref, o_ref, s_ref,
                      buf_sc, xa_sc, xb_sc, S_sc):
    C = GDN_CHUNK
    c = pl.program_id(1)

    @pl.when(c == 0)
    def _():
        S_sc[...] = jnp.zeros_like(S_sc)
        _gdn_frontend(jnp.zeros((8, CONV_CH), F32), cur_ref[0].astype(F32), cw_ref, buf_sc, xa_sc)

    def step(x_sc, x_next):
        _gdn_frontend(cur_ref[0, C - 8:C, :].astype(F32), nxt_ref[0].astype(F32), cw_ref, buf_sc, x_next)
        _gdn_chain(x_sc, z_ref, ba_ref, par_ref, gn_ref, o_ref, S_sc)

    pl.when(c % 2 == 0)(lambda: step(xa_sc, xb_sc))
    pl.when(c % 2 == 1)(lambda: step(xb_sc, xa_sc))

    @pl.when(c == pl.num_programs(1) - 1)
    def _():
        s_ref[0] = S_sc[...]


def _gdn_chain(x_sc, z_ref, ba_ref, par_ref, gn_ref, o_ref, S_sc):
    C = GDN_CHUNK
    beta_t, g_t = _gdn_gates(ba_ref[0], par_ref[...])
    row = lax.broadcasted_iota(jnp.int32, (C, LANES), 0)
    G = g_t
    s = 1
    while s < C:
        G = G + jnp.where(row >= s, pltpu.roll(G, s, axis=0), 0.0)
        s *= 2
    shift = LANES - _N_PAIRS
    Gs = jnp.concatenate([G, pltpu.roll(G, shift, axis=1)], axis=0)
    Bs = jnp.concatenate([beta_t, pltpu.roll(beta_t, shift, axis=1)], axis=0)
    GT = Gs.T
    r2 = lax.broadcasted_iota(jnp.int32, (2 * C, LANES), 0)
    Glast = jnp.where(r2 < C, Gs[C - 1:C, :], Gs[2 * C - 1:2 * C, :])
    eG = jnp.exp(Gs)
    eGl = jnp.exp(Glast - Gs)
    egl = jnp.exp(Glast)

    ii = lax.broadcasted_iota(jnp.int32, (2 * C, 2 * C), 0)
    jj = lax.broadcasted_iota(jnp.int32, (2 * C, 2 * C), 1)
    same = (ii // C) == (jj // C)
    strict = same & (ii > jj)
    diag = ii == jj
    blk = (ii // 16) == (jj // 16)
    top = r2 < C
    gn = gn_ref[...]

    P = range(_N_PAIRS)
    C2 = 2 * C

    def pair(p, off):
        a = x_sc[:, off + (2 * p) * GDN_DK: off + (2 * p + 1) * GDN_DK]
        b = x_sc[:, off + (2 * p + 1) * GDN_DK: off + (2 * p + 2) * GDN_DK]
        return jnp.concatenate([a, b], axis=0)

    def col(t, p, base=0):
        return t[:, base + p:base + p + 1]

    q2 = [pair(p, 0) for p in P]
    k2 = [pair(p, GDN_QK) for p in P]
    kb = [k2[p] * col(Bs, p) for p in P]
    vb = [pair(p, 2 * GDN_QK) * col(Bs, p) for p in P]
    dec = [jnp.where(strict, jnp.exp(jnp.where(strict, col(Gs, p, 8) - GT[8 + p:9 + p, :], 0.0)), 0.0) for p in P]
    kq = [_dot_nt(jnp.concatenate([kb[p], q2[p]], axis=0).astype(BF16), k2[p].astype(BF16)) for p in P]
    A = [kq[p][:C2] * dec[p] for p in P]
    qk = [(kq[p][C2:] * (dec[p] + jnp.where(diag, 1.0, 0.0))).astype(BF16) for p in P]

    Dg = [jnp.where(blk, A[p], 0.0) for p in P]
    E = [A[p] - Dg[p] for p in P]
    Q = [-Dg[p] for p in P]
    Dgb = [Dg[p].astype(BF16) for p in P]
    Dp = [_dot(Dgb[p], Dgb[p]) for p in P]
    for _ in range(2):
        Dpb = [Dp[p].astype(BF16) for p in P]
        st = [_dot(jnp.concatenate([Dpb[p], Q[p].astype(BF16)], axis=0), Dpb[p]) for p in P]
        Q = [Q[p] + Dp[p] + st[p][C2:] for p in P]
        Dp = [st[p][:C2] for p in P]
    Q = [Q[p] + Dp[p] + _dot(Q[p].astype(BF16), Dp[p].astype(BF16)) for p in P]
    rhs = [jnp.concatenate([kb[p] * col(eG, p, 8), vb[p]], axis=1) for p in P]
    er = [_dot(Q[p].astype(BF16), jnp.concatenate([E[p], rhs[p]], axis=1).astype(BF16)) for p in P]
    N = [E[p] + er[p][:, :C2] for p in P]
    y = [rhs[p] + er[p][:, C2:] for p in P]
    Nb = [N[p].astype(BF16) for p in P]
    ny = [_dot(Nb[p], jnp.concatenate([N[p], y[p]], axis=1).astype(BF16)) for p in P]
    zz = [y[p] - ny[p][:, C2:] for p in P]
    sol = [zz[p] + _dot(ny[p][:, :C2].astype(BF16), zz[p].astype(BF16)) for p in P]

    wq = [jnp.concatenate([sol[p][:, :GDN_DK], q2[p] * col(eG, p, 8)], axis=0).astype(BF16) for p in P]
    S_old = [S_sc[h] for h in range(GDN_HEADS)]
    rS = [_dot(wq[h // 2], S_old[h].astype(BF16)) for h in range(GDN_HEADS)]
    vn = [(sol[p][:, GDN_DK:] - jnp.where(top, rS[2 * p][:C2], rS[2 * p + 1][:C2])).astype(BF16) for p in P]
    kdT = [(k2[p] * col(eGl, p, 8)).T.astype(BF16) for p in P]
    zero = jnp.zeros((C2, GDN_DK), BF16)
    for p in P:
        o2 = jnp.where(top, rS[2 * p][C2:], rS[2 * p + 1][C2:]) + _dot(qk[p], vn[p])
        S_sc[2 * p] = S_old[2 * p] * egl[0:1, 8 + p:9 + p] + _dot(kdT[p], jnp.where(top, vn[p], zero))
        S_sc[2 * p + 1] = S_old[2 * p + 1] * egl[C:C + 1, 8 + p:9 + p] + _dot(kdT[p], jnp.where(top, zero, vn[p]))
        for e in range(2):
            h = 2 * p + e
            sl = slice(h * GDN_DK, (h + 1) * GDN_DK)
            o_ref[0, :, sl] = _gated_out(o2[e * C:(e + 1) * C], z_ref[0, :, sl], gn).astype(o_ref.dtype)


def _gdn_prompt(az, ba, conv_w, par, gn):
    B, T, _ = az.shape
    C = GDN_CHUNK
    return pl.pallas_call(
        _gdn_chunk_kernel,
        out_shape=[jax.ShapeDtypeStruct((B, T, GDN_QK), BF16),
                   jax.ShapeDtypeStruct((B, GDN_HEADS, GDN_DK, GDN_DK), F32)],
        grid=(B, T // C),
        in_specs=[pl.BlockSpec((1, C, CONV_CH), lambda b, c: (b, c, 0)),
                  pl.BlockSpec((1, C, CONV_CH), lambda b, c: (b, jnp.minimum(c + 1, T // C - 1), 0)),
                  pl.BlockSpec((1, C, GDN_QK), lambda b, c: (b, c, CONV_CH // GDN_QK)),
                  pl.BlockSpec((1, C, LANES), lambda b, c: (b, c, 0)),
                  pl.BlockSpec((GDN_CONV, CONV_CH), lambda b, c: (0, 0)),
                  pl.BlockSpec((8, LANES), lambda b, c: (0, 0)),
                  pl.BlockSpec((1, GDN_DK), lambda b, c: (0, 0))],
        out_specs=[pl.BlockSpec((1, C, GDN_QK), lambda b, c: (b, c, 0)),
                   pl.BlockSpec((1, GDN_HEADS, GDN_DK, GDN_DK), lambda b, c: (b, 0, 0, 0))],
        scratch_shapes=[pltpu.VMEM((C + 8, CONV_CH), F32), pltpu.VMEM((C, CONV_CH), F32),
                        pltpu.VMEM((C, CONV_CH), F32), pltpu.VMEM((GDN_HEADS, GDN_DK, GDN_DK), F32)],
        compiler_params=_params("parallel", "arbitrary"),
        name="gdn_chunked",
    )(az, az, az, ba, conv_w, par, gn)


def _gdn_step_kernel(qkv_ref, hist_ref, z_ref, ba_ref, s0_ref, cw_ref, par_ref, gn_ref, o_ref, s_ref):
    acc = qkv_ref[0] * cw_ref[GDN_CONV - 1:GDN_CONV, :]
    for j in range(GDN_CONV - 1):
        acc = acc + hist_ref[0, j:j + 1, :] * cw_ref[j:j + 1, :]
    x = _silu(acc)
    beta_t, g_t = _gdn_gates(ba_ref[0], par_ref[...])
    eg_t = jnp.exp(g_t)

    def head(off, h):
        return x[:, off + h * GDN_DK: off + (h + 1) * GDN_DK]

    rows = []
    for off, scale in ((GDN_QK, 1.0), (0, GDN_DK ** -0.5)):
        for h in range(GDN_HEADS):
            t = head(off, h)
            rows.append(t * lax.rsqrt(jnp.sum(t * t, axis=-1, keepdims=True) + NORM_EPS) * scale)
    ri = lax.broadcasted_iota(jnp.int32, (LANES, LANES), 0)
    tile = jnp.zeros((LANES, LANES), F32)
    for r, t in enumerate(rows):
        tile = jnp.where(ri == r, t, tile)
    cols = tile.T
    gn = gn_ref[...]
    for h in range(GDN_HEADS):
        lane = _HEAD_ORDER.index(h)
        kcol = cols[:, h:h + 1]
        qcol = cols[:, GDN_HEADS + h:GDN_HEADS + h + 1]
        S = s0_ref[0, h] * eg_t[:, 8 + lane:9 + lane]
        kv = jnp.sum(kcol * S, axis=0, keepdims=True)
        delta = beta_t[:, lane:lane + 1] * (head(2 * GDN_QK, h) - kv)
        S = S + kcol * delta
        s_ref[0, h] = S
        o = jnp.sum(qcol * S, axis=0, keepdims=True)
        sl = slice(h * GDN_DK, (h + 1) * GDN_DK)
        o_ref[0, :, sl] = _gated_out(o, z_ref[0, :, sl], gn).astype(o_ref.dtype)


def _gdn_sample(az, ba, hist, s0, conv_w, par, gn):
    N = az.shape[0]
    return pl.pallas_call(
        _gdn_step_kernel,
        out_shape=[jax.ShapeDtypeStruct((N, 1, GDN_QK), BF16),
                   jax.ShapeDtypeStruct((N, GDN_HEADS, GDN_DK, GDN_DK), F32)],
        grid=(N,),
        in_specs=[pl.BlockSpec((1, 1, CONV_CH), lambda b: (b, 0, 0)),
                  pl.BlockSpec((1, GDN_CONV - 1, CONV_CH), lambda b: (b, 0, 0)),
                  pl.BlockSpec((1, 1, GDN_QK), lambda b: (b, 0, CONV_CH // GDN_QK)),
                  pl.BlockSpec((1, 1, LANES), lambda b: (b, 0, 0)),
                  pl.BlockSpec((1, GDN_HEADS, GDN_DK, GDN_DK), lambda b: (b, 0, 0, 0)),
                  pl.BlockSpec((GDN_CONV, CONV_CH), lambda b: (0, 0)),
                  pl.BlockSpec((8, LANES), lambda b: (0, 0)),
                  pl.BlockSpec((1, GDN_DK), lambda b: (0, 0))],
        out_specs=[pl.BlockSpec((1, 1, GDN_QK), lambda b: (b, 0, 0)),
                   pl.BlockSpec((1, GDN_HEADS, GDN_DK, GDN_DK), lambda b: (b, 0, 0, 0))],
        compiler_params=_params("parallel"),
        name="gdn_step",
    )(az, hist, az, ba, s0, conv_w, par, gn)


DWA_BATCH = 4


def _dwa_blocks(refs, blocks, masks, first, scratch):
    q_ref, k_ref, v_ref = refs
    m_sc, l_sc, acc_sc = scratch
    cur_ok, prev_ok = masks
    scale = DWA_DH ** -0.5
    BL = DWA_BLOCK
    qb = [q_ref[0, cur, :].astype(BF16) for cur, _, _ in blocks]
    sc = [jnp.where(cur_ok, _dot_nt(qb[i], k_ref[0, cur, :].astype(BF16)) * scale, -jnp.inf)
          for i, (cur, _, _) in enumerate(blocks)]
    sp = [None if prev is None else
          jnp.where(prev_ok & on, _dot_nt(qb[i], k_ref[0, prev, :].astype(BF16)) * scale, -jnp.inf)
          for i, (_, prev, on) in enumerate(blocks)]
    m, l, pc, pp = [], [], [], []
    for i in range(len(blocks)):
        mi = jnp.max(sc[i], axis=-1, keepdims=True)
        if sp[i] is not None:
            mi = jnp.maximum(mi, jnp.max(sp[i], axis=-1, keepdims=True))
        p = jnp.exp(sc[i] - mi)
        li = jnp.sum(p, axis=-1, keepdims=True)
        pc.append(p.astype(BF16))
        if sp[i] is not None:
            p = jnp.exp(sp[i] - mi)
            li = li + jnp.sum(p, axis=-1, keepdims=True)
            pp.append(p.astype(BF16))
        else:
            pp.append(None)
        m.append(mi)
        l.append(li)
    acc = [_dot(pc[i], v_ref[0, cur, :].astype(BF16)) for i, (cur, _, _) in enumerate(blocks)]
    acc = [a if pp[i] is None else a + _dot(pp[i], v_ref[0, blocks[i][1], :].astype(BF16))
           for i, a in enumerate(acc)]
    for i, (cur, _, _) in enumerate(blocks):
        if first:
            m_sc[cur, :] = jnp.broadcast_to(m[i], (BL, LANES))
            l_sc[cur, :] = jnp.broadcast_to(l[i], (BL, LANES))
            acc_sc[cur, :] = acc[i]
        else:
            m_old = m_sc[cur, :]
            m_new = jnp.maximum(m_old, m[i])
            a = jnp.exp(m_old - m_new)
            b = jnp.exp(m[i] - m_new)
            m_sc[cur, :] = m_new
            l_sc[cur, :] = a * l_sc[cur, :] + b * l[i]
            acc_sc[cur, :] = a * acc_sc[cur, :] + b * acc[i]


def _dwa_kernel(q0_ref, q1_ref, q2_ref, k0_ref, v0_ref, k1_ref, v1_ref, k2_ref, v2_ref, *rest, n_cast):
    o_ref = rest[n_cast]
    m_sc, l_sc, acc_sc = rest[2 * n_cast + 1:]
    _cast_blocks(rest[:n_cast], rest[n_cast + 1:2 * n_cast + 1])
    BL = DWA_BLOCK
    T = o_ref.shape[1]
    ri = lax.broadcasted_iota(jnp.int32, (BL, BL), 0)
    ci = lax.broadcasted_iota(jnp.int32, (BL, BL), 1)
    masks = (ri >= ci, ci >= ri)
    q_refs, k_refs, v_refs = (q0_ref, q1_ref, q2_ref), (k0_ref, k1_ref, k2_ref), (v0_ref, v1_ref, v2_ref)
    scratch = (m_sc, l_sc, acc_sc)

    for gi, (_, dil) in enumerate(DWA_GROUPS):
        refs = (q_refs[gi], k_refs[gi], v_refs[gi])
        nb = T // dil // BL

        def rows(blk, r, dil=dil):
            start = blk * (BL * dil) + r
            if dil == 1:
                return pl.ds(pl.multiple_of(start, BL), BL)
            return pl.ds(start, BL, stride=dil)

        def block(n, r, nb=nb, rows=rows):
            if nb == 1:
                return rows(n, r), None, None
            return rows(n, r), rows(jnp.maximum(n - 1, 0), r), n > 0

        if dil >= DWA_BATCH:
            for r0 in range(0, dil, DWA_BATCH):
                def body(n, carry, r0=r0, refs=refs, block=block, first=gi == 0):
                    _dwa_blocks(refs, [block(n, r0 + t) for t in range(DWA_BATCH)], masks, first, scratch)
                    return carry
                if nb == 1:
                    body(0, 0)
                else:
                    lax.fori_loop(0, nb, body, 0)
        else:
            assert dil == 1 and nb % DWA_BATCH == 0

            def body(i, carry, refs=refs, block=block, first=gi == 0):
                _dwa_blocks(refs, [block(i * DWA_BATCH + t, 0) for t in range(DWA_BATCH)], masks, first, scratch)
                return carry
            lax.fori_loop(0, nb // DWA_BATCH, body, 0)

    o_ref[0] = (acc_sc[...] / l_sc[...]).astype(o_ref.dtype)


def _dwa_prompt(qkv, cast=()):
    B, T, _ = qkv.shape
    c_in, c_out, c_shapes = _cast_jobs(cast, lambda b, h: b * DWA_HEADS + h, B * DWA_HEADS)

    def col(c):
        return pl.BlockSpec((1, T, DWA_DH), lambda b, h: (b, 0, c(h)))

    ng = len(DWA_GROUPS)
    q_blk = ATTN_Q_COL // DWA_DH
    in_specs = [col(lambda h, g=g: q_blk + g * DWA_HEADS + h) for g in range(ng)]
    for g in range(ng):
        in_specs += [col(lambda h, g=g: 2 * g * DWA_HEADS + h), col(lambda h, g=g: (2 * g + 1) * DWA_HEADS + h)]
    args = [qkv] * (3 * ng) + [a for a, _, _, _ in cast]
    res = pl.pallas_call(
        functools.partial(_dwa_kernel, n_cast=len(cast)),
        out_shape=[jax.ShapeDtypeStruct((B, T, DWA_GW), BF16)] + c_shapes,
        grid=(B, DWA_HEADS),
        in_specs=in_specs + c_in,
        out_specs=[col(lambda h: h)] + c_out,
        scratch_shapes=[pltpu.VMEM((T, LANES), F32), pltpu.VMEM((T, LANES), F32), pltpu.VMEM((T, DWA_DH), F32)],
        compiler_params=_params(*(("arbitrary",) * 2 if cast else ("parallel",) * 2)),
        name="dwa_prompt",
    )(*args)
    return res[0], res[1:]


def _dwa_step_kernel(q_ref, n0_ref, n1_ref, n2_ref, c0_ref, c1_ref, c2_ref, o_ref):
    new_refs = (n0_ref, n1_ref, n2_ref)
    cache_refs = (c0_ref, c1_ref, c2_ref)
    scale = DWA_DH ** -0.5
    parts = []
    for gi in range(len(DWA_GROUPS)):
        q = q_ref[0, gi * DWA_HEADS:(gi + 1) * DWA_HEADS, :]
        k = cache_refs[gi][0, :, 0, 0]
        v = cache_refs[gi][0, :, 0, 1]
        s = jnp.sum(k * q[None], axis=-1, keepdims=True) * scale
        s_new = jnp.sum(new_refs[gi][0, 0] * q, axis=-1, keepdims=True) * scale
        m = jnp.maximum(jnp.max(s, axis=0), s_new)
        p = jnp.exp(s - m[None])
        p_new = jnp.exp(s_new - m)
        l = jnp.sum(p, axis=0) + p_new
        acc = jnp.sum(p * v, axis=0) + p_new * new_refs[gi][0, 1]
        parts.append((m, l, acc))
    mm = jnp.maximum(jnp.maximum(parts[0][0], parts[1][0]), parts[2][0])
    num = jnp.zeros((DWA_HEADS, DWA_DH), F32)
    den = jnp.zeros((DWA_HEADS, 1), F32)
    for m, l, acc in parts:
        e = jnp.exp(m - mm)
        num = num + e * acc
        den = den + e * l
    o_ref[0] = num / den


def _dwa_sample(q, kv_new, caches):
    N = q.shape[0]
    span = DWA_BLOCK
    in_specs = [pl.BlockSpec((1, len(DWA_GROUPS) * DWA_HEADS, DWA_DH), lambda b: (b, 0, 0))]
    in_specs += [pl.BlockSpec((1, 2, DWA_HEADS, DWA_DH), lambda b: (b, 0, 0, 0))] * 3
    views = []
    for (win, dil), cache in zip(DWA_GROUPS, caches):
        L = cache.shape[1]
        assert L == win and L // dil == span
        views.append(cache.reshape(N, span, dil, 2, DWA_HEADS, DWA_DH))
        in_specs.append(pl.BlockSpec((1, span, 1, 2, DWA_HEADS, DWA_DH), lambda b: (b, 0, 0, 0, 0, 0)))
    return pl.pallas_call(
        _dwa_step_kernel,
        out_shape=jax.ShapeDtypeStruct((N, DWA_HEADS, DWA_DH), F32),
        grid=(N,),
        in_specs=in_specs,
        out_specs=pl.BlockSpec((1, DWA_HEADS, DWA_DH), lambda b: (b, 0, 0)),
        compiler_params=_params("parallel"),
        name="dwa_step",
    )(q, *kv_new, *views)


def _mix_out_kernel(oa_ref, ob_ref, ga_ref, gb_ref, h_ref, gt_ref, nsh_ref, nsc_ref, ngain_ref,
                    wa_ref, wb_ref, wo_ref, *rest, n_cast):
    out_ref, nxt_ref = rest[n_cast:n_cast + 2]
    _cast_blocks(rest[:n_cast], rest[n_cast + 2:])
    ta = _dot(oa_ref[0], wa_ref[...])
    tb = _dot(ob_ref[0], wb_ref[...])
    merged = _sigmoid(ga_ref[0].astype(F32)) * ta + _sigmoid(gb_ref[0].astype(F32)) * tb
    h = h_ref[0] + gt_ref[0] * _dot(merged.astype(BF16), wo_ref[...])
    out_ref[0] = h
    nxt_ref[0] = _modulated(h, ngain_ref[...], nsc_ref[0], nsh_ref[0])


MIX_TM = 512


def _mix_out(oa, ob, gates, h, mod3, wa, wb, wo, next_gain, cast=()):
    G, R, _ = h.shape
    tm = min(MIX_TM, R)
    c_in, c_out, c_shapes = _cast_jobs(cast, lambda g, i: g * (R // tm) + i, G * (R // tm))

    def rows(width, col=0):
        return pl.BlockSpec((1, tm, width), lambda g, i: (g, i, col))

    def whole(shape):
        return pl.BlockSpec(shape, lambda g, i: (0, 0), pipeline_mode=pl.Buffered(1))

    res = pl.pallas_call(
        functools.partial(_mix_out_kernel, n_cast=len(cast)),
        out_shape=[jax.ShapeDtypeStruct((G, R, D_MODEL), F32), jax.ShapeDtypeStruct((G, R, D_MODEL), BF16)]
        + c_shapes,
        grid=(G, R // tm),
        in_specs=[rows(GDN_QK), rows(DWA_GW), rows(D_MODEL, 0), rows(D_MODEL, 1), rows(D_MODEL),
                  _mod_spec(mod3, R, tm, 5), _mod_spec(mod3, R, tm, 6), _mod_spec(mod3, R, tm, 7),
                  pl.BlockSpec((1, D_MODEL), lambda g, i: (0, 0)),
                  whole(wa.shape), whole(wb.shape), whole(wo.shape)] + c_in,
        out_specs=[rows(D_MODEL), rows(D_MODEL)] + c_out,
        compiler_params=_params(*(("arbitrary",) * 2 if cast else ("parallel",) * 2)),
        name="mix_out",
    )(oa, ob, gates, gates, h, mod3, mod3, mod3, next_gain.reshape(1, D_MODEL), wa, wb, wo,
      *[a for a, _, _, _ in cast])
    return res[0], res[1], res[2:]


_W_IN_COLS = _C_GB + D_MODEL
_W_IN_ALIGNED = _W_IN_COLS // CAST_BLOCK * CAST_BLOCK


def _mixer_weights(w_in_f32, w_in_bf, a_log, dt_bias):
    w = jnp.concatenate([w_in_bf, w_in_f32[:, _W_IN_ALIGNED:].astype(BF16)], axis=1)
    order = np.array(_HEAD_ORDER)
    ba_cols = np.concatenate([_C_B + order, _C_A + order])
    par = jnp.zeros((8, LANES), F32)
    par = par.at[0, 8:16].set(a_log[0][order]).at[1, 8:16].set(dt_bias[0][order])
    return dict(
        az=w[:, :_C_B],
        ba=jnp.pad(w[:, ba_cols], ((0, 0), (0, LANES - 2 * GDN_HEADS))),
        attn=jnp.concatenate(
            [w[:, c + g * DWA_GW:c + (g + 1) * DWA_GW] for g in range(len(DWA_GROUPS)) for c in (_C_K, _C_V)]
            + [w[:, _C_Q:_C_K]], axis=1),
        gates=w[:, _C_GA:],
        par=par,
    )


def _trunk(h1, xm, mod3, pos, W, norms, ffn2, *, sample_state=None):
    G, R, _ = h1.shape
    up2, down2 = ffn2
    tables = _rope_tables(pos)
    ba = _proj(xm, W["ba"], F32, tn=LANES, name="proj_ba")
    gates = _proj(xm, W["gates"], BF16, tn=1024, name="proj_gates")
    qkv = _proj(xm, W["attn"], F32, rope="attn", tables=tables, name="proj_attn")
    kvs = [qkv[:, :, 2 * g * DWA_GW:2 * (g + 1) * DWA_GW] for g in range(len(DWA_GROUPS))]
    if sample_state is None:
        az, conv_rows = _proj(xm, W["az"], BF16, tn=1024, tail=True, name="proj_az")
        o_a, s_new = _gdn_prompt(az, ba, W["conv"], W["par"], W["gn"])
        o_b, (up2,) = _dwa_prompt(qkv, cast=((up2, 1, 2 * D_FF, 4 * CAST_BLOCK),))
        down_cast = ((down2, 0, D_FF, 2 * CAST_BLOCK),)
    else:
        hist, s0, caches = sample_state
        conv_rows = _proj(xm, W["az"], F32, tn=1024, name="proj_az").reshape(R, 1, -1)
        o_a, s_new = _gdn_sample(conv_rows, ba.reshape(R, 1, LANES), hist, s0, W["conv"], W["par"], W["gn"])
        o_b = _dwa_sample(qkv[:, :, ATTN_Q_COL:].reshape(R, len(DWA_GROUPS) * DWA_HEADS, DWA_DH),
                          [kv.reshape(R, 2, DWA_HEADS, DWA_DH) for kv in kvs], caches)
        o_a = o_a.reshape(1, R, -1)
        o_b = o_b.reshape(1, R, DWA_GW).astype(BF16)
        down_cast = ()
    h2, xm2, cast_out = _mix_out(o_a, o_b, gates, h1, mod3, W["wa"], W["wb"], W["wo"], norms["ffn2"],
                                 cast=down_cast)
    if down_cast:
        (down2,) = cast_out
    y = _ffn_last(h2, xm2, mod3, up2, down2, norms["final"])
    return y, conv_rows[:, :, :CONV_CH], s_new, kvs, (up2, down2)


def kernel(x_prompt, x_sample, state_conv, state_delta, cache_kv_w128, cache_kv_w512, cache_kv_w2048, c_prompt, c_sample, w_ada, b_ada, norm_ffn1, w_ffn1_up, w_ffn1_down, norm_mix, w_in, conv_w, a_log, dt_bias, gdn_norm, w_proj_a, w_proj_b, w_out, norm_ffn2, w_ffn2_up, w_ffn2_down, norm_final):
    B, T, _ = x_prompt.shape
    N, S, _ = x_sample.shape
    assert S == 1 and T % (DWA_BLOCK * DWA_GROUPS[-1][1]) == 0
    norms = dict(ffn1=norm_ffn1[0], mix=norm_mix[0], ffn2=norm_ffn2[0], final=norm_final)

    mod = _ada(jnp.concatenate([c_prompt, c_sample], axis=0), w_ada[0], b_ada)
    mod_p = mod[:B].reshape(B, 1, N_MOD * D_MODEL)
    mod_s = mod[B:].reshape(1, N, N_MOD * D_MODEL)

    up1, down1 = w_ffn1_up[0].astype(BF16), w_ffn1_down[0].astype(BF16)
    h1_p, xm_p, w_in_bf = _ffn_first(x_prompt, mod_p, norms["ffn1"], up1, down1, norms["mix"],
                                     cast=((w_in[0], 1, _W_IN_ALIGNED, CAST_BLOCK),))
    W = _mixer_weights(w_in[0], w_in_bf, a_log, dt_bias)
    W.update(conv=conv_w[0], gn=gdn_norm,
             wa=w_proj_a[0].astype(BF16), wb=w_proj_b[0].astype(BF16), wo=w_out[0].astype(BF16))
    h1_s, xm_s = _ffn_first(x_sample.reshape(1, N, D_MODEL), mod_s, norms["ffn1"], up1, down1, norms["mix"])

    y_p, rows_p, s_p, kv_p, ffn2 = _trunk(h1_p, xm_p, mod_p, jnp.arange(T, dtype=jnp.int32), W, norms,
                                          (w_ffn2_up[0], w_ffn2_down[0]))
    caches = (cache_kv_w128[0], cache_kv_w512[0], cache_kv_w2048[0])
    y_s, rows_s, s_s, kv_s, _ = _trunk(h1_s, xm_s, mod_s, jnp.full((N,), PAST_LEN, dtype=jnp.int32), W, norms,
                                       ffn2, sample_state=(state_conv[0], state_delta[0], caches))

    keep = GDN_CONV - 1
    conv_p = rows_p[:, TAIL_ROWS - keep:][None]
    conv_s = jnp.concatenate([state_conv[0], rows_s], axis=1)[:, -keep:][None]
    kv_out_p = []
    for (win, _), kv in zip(DWA_GROUPS, kv_p):
        k = min(win, T)
        kv_out_p.append(kv[:, T - k:].reshape(1, B, k, 2, DWA_HEADS, DWA_DH))
    kv_out_s = [kv.reshape(1, N, 1, 2, DWA_HEADS, DWA_DH) for kv in kv_s]
    return (y_p, y_s.reshape(N, 1, D_MODEL), conv_p, s_p[None], *kv_out_p,
            conv_s, s_s[None], *kv_out_s)
```

```python
import functools

import jax
import jax.numpy as jnp
import numpy as np
from jax import lax
from jax.experimental import pallas as pl
from jax.experimental.pallas import tpu as pltpu

F32 = jnp.float32
BF16 = jnp.bfloat16

D_MODEL = 2048
D_FF = 5632
N_MOD = 9
NORM_EPS = 1e-6
PAST_LEN = 16384

GDN_HEADS = 8
GDN_DK = 128
GDN_CONV = 4
GDN_CHUNK = 64
GDN_QK = GDN_HEADS * GDN_DK
CONV_CH = 3 * GDN_QK

DWA_GROUPS = ((128, 1), (512, 4), (2048, 16))
DWA_HEADS = 4
DWA_DH = 128
DWA_GW = DWA_HEADS * DWA_DH
DWA_WIDTH = len(DWA_GROUPS) * DWA_GW
DWA_BLOCK = 128
ROPE_THETA = 10000.0

LANES = 128
MXU_COLS = 256
VMEM_LIMIT = 56 * 1024 * 1024

_C_Z = CONV_CH
_C_B = _C_Z + GDN_QK
_C_A = _C_B + GDN_HEADS
_C_Q = _C_A + GDN_HEADS
_C_K = _C_Q + DWA_WIDTH
_C_V = _C_K + DWA_WIDTH
_C_GA = _C_V + DWA_WIDTH
_C_GB = _C_GA + D_MODEL

_HEAD_ORDER = (0, 2, 4, 6, 1, 3, 5, 7)
_N_PAIRS = GDN_HEADS // 2


def _sigmoid(x):
    return 0.5 * jnp.tanh(0.5 * x) + 0.5


def _silu(x):
    return x * _sigmoid(x)


def _dot(a, b):
    return jnp.dot(a, b, preferred_element_type=F32)


def _dot_nt(a, b):
    return lax.dot_general(a, b, (((1,), (1,)), ((), ())), preferred_element_type=F32)


def _params(*sem):
    return pltpu.CompilerParams(dimension_semantics=sem, vmem_limit_bytes=VMEM_LIMIT)


def _ada_kernel(c_ref, w_ref, b_ref, o_ref):
    a = _silu(c_ref[...]).astype(BF16)
    o_ref[...] = _dot(a, w_ref[...].astype(BF16)) + b_ref[...]


def _ada(c, w, b):
    m, n = c.shape[0], w.shape[1]
    tn = 1024
    return pl.pallas_call(
        _ada_kernel,
        out_shape=jax.ShapeDtypeStruct((m, n), F32),
        grid=(n // tn,),
        in_specs=[pl.BlockSpec((m, D_MODEL), lambda j: (0, 0)),
                  pl.BlockSpec((D_MODEL, tn), lambda j: (0, j)),
                  pl.BlockSpec((1, tn), lambda j: (0, j))],
        out_specs=pl.BlockSpec((m, tn), lambda j: (0, j)),
        compiler_params=_params("arbitrary"),
        name="ada_mod",
    )(c, w, b)


def _rms(x):
    return x * lax.rsqrt(jnp.mean(x * x, axis=-1, keepdims=True) + NORM_EPS)


def _modulated(h, gain, scale, shift):
    return ((_rms(h) * gain) * (1.0 + scale) + shift).astype(BF16)


def _swiglu_step(xm, wg_ref, wu_ref, wd_ref, acc_sc):
    g = _dot(xm, wg_ref[...])
    u = _dot(xm, wu_ref[...])
    acc_sc[...] += _dot((_silu(g) * u).astype(BF16), wd_ref[...])


def _ffn_first_kernel(x_ref, sh_ref, sc_ref, gt_ref, gain_ref, wg_ref, wu_ref, wd_ref,
                      nsh_ref, nsc_ref, ngain_ref, *rest, n_cast):
    cast_in = rest[:n_cast]
    out_ref, nxt_ref = rest[n_cast:n_cast + 2]
    cast_out = rest[n_cast + 2:2 * n_cast + 2]
    xm_sc, acc_sc = rest[2 * n_cast + 2:]
    j = pl.program_id(2)

    @pl.when(j == 0)
    def _():
        xm_sc[...] = _modulated(x_ref[0], gain_ref[...], sc_ref[0], sh_ref[0])
        acc_sc[...] = jnp.zeros_like(acc_sc)

    _cast_blocks(cast_in, cast_out)
    _swiglu_step(xm_sc[...], wg_ref, wu_ref, wd_ref, acc_sc)

    @pl.when(j == pl.num_programs(2) - 1)
    def _():
        h = x_ref[0] + 0.5 * gt_ref[0] * acc_sc[...]
        out_ref[0] = h
        nxt_ref[0] = _modulated(h, ngain_ref[...], nsc_ref[0], nsh_ref[0])


def _ffn_last_kernel(x_ref, xm_ref, gt_ref, wg_ref, wu_ref, wd_ref, ngain_ref, out_ref, acc_sc):
    j = pl.program_id(2)

    @pl.when(j == 0)
    def _():
        acc_sc[...] = jnp.zeros_like(acc_sc)

    _swiglu_step(xm_ref[0], wg_ref, wu_ref, wd_ref, acc_sc)

    @pl.when(j == pl.num_programs(2) - 1)
    def _():
        out_ref[0] = _rms(x_ref[0] + 0.5 * gt_ref[0] * acc_sc[...]) * ngain_ref[...]


FFN_TM = 512
FFN_TF = 512


def _mod_spec(mod3, R, tm, k):
    if mod3.shape[1] == R:
        return pl.BlockSpec((1, tm, D_MODEL), lambda g, i, *_: (g, i, k))
    return pl.BlockSpec((1, 1, D_MODEL), lambda g, i, *_: (g, 0, k))


def _ffn_specs(R):
    tm = min(FFN_TM, R)
    nff = D_FF // FFN_TF
    rows = pl.BlockSpec((1, tm, D_MODEL), lambda g, i, j: (g, i, 0))
    vec = pl.BlockSpec((1, D_MODEL), lambda g, i, j: (0, 0))
    weights = [pl.BlockSpec((D_MODEL, FFN_TF), lambda g, i, j: (0, j)),
               pl.BlockSpec((D_MODEL, FFN_TF), lambda g, i, j: (0, j + nff)),
               pl.BlockSpec((FFN_TF, D_MODEL), lambda g, i, j: (j, 0))]
    return tm, nff, rows, vec, weights


CAST_BLOCK = 128


def _cast_jobs(arrays, step_of, total_steps):
    in_specs, out_specs, out_shapes = [], [], []
    start = 0
    for arr, axis, extent, slab in arrays:
        assert extent % slab == 0 and slab % CAST_BLOCK == 0
        nblk = extent // slab
        shape = tuple(slab if d == axis else n for d, n in enumerate(arr.shape))

        def index(*grid, start=start, nblk=nblk, axis=axis):
            blk = jnp.clip(step_of(*grid) - start, 0, nblk - 1)
            return (blk, 0) if axis == 0 else (0, blk)

        in_specs.append(pl.BlockSpec(shape, index))
        out_specs.append(pl.BlockSpec(shape, index))
        out_shapes.append(jax.ShapeDtypeStruct(
            tuple(extent if d == axis else n for d, n in enumerate(arr.shape)), BF16))
        start += nblk
    assert start <= total_steps, (start, total_steps)
    return in_specs, out_specs, out_shapes


def _cast_blocks(cast_in, cast_out):
    for src, dst in zip(cast_in, cast_out):
        dst[...] = src[...].astype(BF16)


def _ffn_first(x3, mod3, gain, w_up, w_down, next_gain, cast=()):
    G, R, _ = x3.shape
    tm, nff, rows, vec, weights = _ffn_specs(R)
    ms = [_mod_spec(mod3, R, tm, k) for k in range(5)]
    c_in, c_out, c_shapes = _cast_jobs(cast, lambda g, i, j: (g * (R // tm) + i) * nff + j, G * (R // tm) * nff)
    sem = ("arbitrary",) * 3 if cast else ("parallel", "parallel", "arbitrary")
    return pl.pallas_call(
        functools.partial(_ffn_first_kernel, n_cast=len(cast)),
        out_shape=[jax.ShapeDtypeStruct((G, R, D_MODEL), F32), jax.ShapeDtypeStruct((G, R, D_MODEL), BF16)]
        + c_shapes,
        grid=(G, R // tm, nff),
        in_specs=[rows, ms[0], ms[1], ms[2], vec] + weights + [ms[3], ms[4], vec] + c_in,
        out_specs=[rows, rows] + c_out,
        scratch_shapes=[pltpu.VMEM((tm, D_MODEL), BF16), pltpu.VMEM((tm, D_MODEL), F32)],
        compiler_params=_params(*sem),
        name="ffn_first",
    )(x3, mod3, mod3, mod3, gain.reshape(1, D_MODEL), w_up, w_up, w_down, mod3, mod3,
      next_gain.reshape(1, D_MODEL), *[a for a, _, _, _ in cast])


def _ffn_last(x3, xm3, mod3, w_up, w_down, final_gain):
    G, R, _ = x3.shape
    tm, nff, rows, vec, weights = _ffn_specs(R)
    return pl.pallas_call(
        _ffn_last_kernel,
        out_shape=jax.ShapeDtypeStruct((G, R, D_MODEL), F32),
        grid=(G, R // tm, nff),
        in_specs=[rows, rows, _mod_spec(mod3, R, tm, 8)] + weights + [vec],
        out_specs=rows,
        scratch_shapes=[pltpu.VMEM((tm, D_MODEL), F32)],
        compiler_params=_params("parallel", "parallel", "arbitrary"),
        name="ffn_last",
    )(x3, xm3, mod3, w_up, w_up, w_down, final_gain.reshape(1, D_MODEL))


def _rope_tile(x, cosf, sinf):
    return x * cosf + pltpu.roll(x, DWA_DH // 2, axis=1) * sinf


TAIL_ROWS = 8


def _proj_kernel(a_ref, wt_ref, *rest, tail):
    acc = _dot_nt(a_ref[0], wt_ref[...])
    rest[0][0] = acc.astype(rest[0].dtype)
    if tail:
        rest[1][0] = acc[acc.shape[0] - TAIL_ROWS:]


PROJ_TM = 1024


def _proj(xm3, wt, out_dtype, *, tn=512, tail=False, name="proj"):
    G, R, _ = xm3.shape
    n = wt.shape[0]
    tm = min(PROJ_TM, R)
    out_shape = jax.ShapeDtypeStruct((G, R, n), out_dtype)
    out_specs = pl.BlockSpec((1, tm, tn), lambda g, i, j: (g, i, j))
    if tail:
        out_shape = [out_shape, jax.ShapeDtypeStruct((G, R // tm * TAIL_ROWS, n), F32)]
        out_specs = [out_specs, pl.BlockSpec((1, TAIL_ROWS, tn), lambda g, i, j: (g, i, j))]
    res = pl.pallas_call(
        functools.partial(_proj_kernel, tail=tail),
        out_shape=out_shape,
        grid=(G, R // tm, n // tn),
        in_specs=[pl.BlockSpec((1, tm, D_MODEL), lambda g, i, j: (g, i, 0)),
                  pl.BlockSpec((tn, D_MODEL), lambda g, i, j: (j, 0))],
        out_specs=out_specs,
        compiler_params=_params("parallel", "parallel", "arbitrary"),
        name=name,
    )(xm3, wt)
    if tail:
        return res[0], res[1][:, -TAIL_ROWS:]
    return res


ATTN_TILES = tuple((2 * g, 2 * g + 2) for g in range(len(DWA_GROUPS))) + ((2 * len(DWA_GROUPS), 3 * len(DWA_GROUPS)),)


def _attn_proj_kernel(a_ref, wt_ref, cos_ref, sin_ref, *o_refs):
    j = pl.program_id(2)
    acc = _dot_nt(a_ref[0], wt_ref[...])
    cosf, sinf = cos_ref[...], sin_ref[...]
    for o_ref, (lo, hi) in zip(o_refs, ATTN_TILES):
        is_q = lo == ATTN_TILES[-1][0]

        @pl.when((j >= lo) & (j < hi))
        def _(o_ref=o_ref, lo=lo, is_q=is_q):
            for h in range(0, DWA_GW, DWA_DH):
                x = acc[:, h:h + DWA_DH]
                r = _rope_tile(x, cosf, sinf)
                if not is_q:
                    r = jnp.where(j == lo, r, x)
                o_ref[0, :, h:h + DWA_DH] = r


def _attn_proj(xm3, wt, tables):
    G, R, _ = xm3.shape
    tm = min(PROJ_TM, R)
    tab = pl.BlockSpec((tm, DWA_DH), lambda g, i, j: (i, 0))
    out_shape, out_specs = [], []
    for lo, hi in ATTN_TILES:
        out_shape.append(jax.ShapeDtypeStruct((G, R, (hi - lo) * DWA_GW), F32))
        out_specs.append(pl.BlockSpec((1, tm, DWA_GW),
                                      lambda g, i, j, lo=lo, hi=hi: (g, i, jnp.clip(j - lo, 0, hi - lo - 1))))
    res = pl.pallas_call(
        _attn_proj_kernel,
        out_shape=out_shape,
        grid=(G, R // tm, ATTN_TILES[-1][1]),
        in_specs=[pl.BlockSpec((1, tm, D_MODEL), lambda g, i, j: (g, i, 0)),
                  pl.BlockSpec((DWA_GW, D_MODEL), lambda g, i, j: (j, 0)), tab, tab],
        out_specs=out_specs,
        compiler_params=_params("parallel", "arbitrary", "arbitrary"),
        name="proj_attn",
    )(xm3, wt, *tables)
    return res[:-1], res[-1]


def _rope_tables(pos):
    half = DWA_DH // 2
    inv = jnp.power(ROPE_THETA, -jnp.arange(half, dtype=F32) / half)
    ang = pos.astype(F32)[:, None] * inv[None, :]
    cos, sin = jnp.cos(ang), jnp.sin(ang)
    return jnp.concatenate([cos, cos], axis=-1), jnp.concatenate([-sin, sin], axis=-1)


def _gdn_gates(ba, par):
    beta = _sigmoid(ba)
    x = ba + par[1:2]
    softplus = jnp.maximum(x, 0.0) + jnp.log1p(jnp.exp(-jnp.abs(x)))
    return beta, -jnp.exp(par[0:1]) * softplus


def _gated_out(o, z, gn):
    return (_rms(o) * gn) * _silu(z.astype(F32))


def _l2norm(t):
    return t * lax.rsqrt(jnp.sum(t * t, axis=-1, keepdims=True) + NORM_EPS)


def _gdn_frontend(hist, raw, cw_ref, buf_sc, x_out):
    C = GDN_CHUNK
    buf_sc[0:8, :] = hist
    buf_sc[8:8 + C, :] = raw
    rows = buf_sc[...]
    acc = rows[8:] * cw_ref[GDN_CONV - 1:GDN_CONV, :]
    for s in range(1, GDN_CONV):
        acc = acc + pltpu.roll(rows, s, axis=0)[8:] * cw_ref[GDN_CONV - 1 - s:GDN_CONV - s, :]
    x = _silu(acc)
    for h in range(2 * GDN_HEADS):
        sl = slice(h * GDN_DK, (h + 1) * GDN_DK)
        x_out[:, sl] = _l2norm(x[:, sl]) * (GDN_DK ** -0.5 if h < GDN_HEADS else 1.0)
    x_out[:, 2 * GDN_QK:] = x[:, 2 * GDN_QK:]


def _gdn_chunk_kernel(cur_ref, nxt_ref, z_ref, ba_ref, cw_ref, par_ref, gn_ref, o_ref, s_ref,
                      buf_sc, xa_sc, xb_sc, S_sc):
    C = GDN_CHUNK
    c = pl.program_id(1)

    @pl.when(c == 0)
    def _():
        S_sc[...] = jnp.zeros_like(S_sc)
        _gdn_frontend(jnp.zeros((8, CONV_CH), F32), cur_ref[0].astype(F32), cw_ref, buf_sc, xa_sc)

    def step(x_sc, x_next):
        _gdn_frontend(cur_ref[0, C - 8:C, :].astype(F32), nxt_ref[0].astype(F32), cw_ref, buf_sc, x_next)
        _gdn_chain(x_sc, z_ref, ba_ref, par_ref, gn_ref, o_ref, S_sc)

    pl.when(c % 2 == 0)(lambda: step(xa_sc, xb_sc))
    pl.when(c % 2 == 1)(lambda: step(xb_sc, xa_sc))

    @pl.when(c == pl.num_programs(1) - 1)
    def _():
        s_ref[0] = S_sc[...]


def _gdn_chain(x_sc, z_ref, ba_ref, par_ref, gn_ref, o_ref, S_sc):
    C = GDN_CHUNK
    beta_t, g_t = _gdn_gates(ba_ref[0], par_ref[...])
    row = lax.broadcasted_iota(jnp.int32, (C, LANES), 0)
    G = g_t
    s = 1
    while s < C:
        G = G + jnp.where(row >= s, pltpu.roll(G, s, axis=0), 0.0)
        s *= 2
    shift = LANES - _N_PAIRS
    Gs = jnp.concatenate([G, pltpu.roll(G, shift, axis=1)], axis=0)
    Bs = jnp.concatenate([beta_t, pltpu.roll(beta_t, shift, axis=1)], axis=0)
    GT = Gs.T
    r2 = lax.broadcasted_iota(jnp.int32, (2 * C, LANES), 0)
    Glast = jnp.where(r2 < C, Gs[C - 1:C, :], Gs[2 * C - 1:2 * C, :])
    eG = jnp.exp(Gs)
    eGl = jnp.exp(Glast - Gs)
    egl = jnp.exp(Glast)

    ii = lax.broadcasted_iota(jnp.int32, (2 * C, 2 * C), 0)
    jj = lax.broadcasted_iota(jnp.int32, (2 * C, 2 * C), 1)
    same = (ii // C) == (jj // C)
    strict = same & (ii > jj)
    diag = ii == jj
    blk = (ii // 16) == (jj // 16)
    top = r2 < C
    gn = gn_ref[...]

    P = range(_N_PAIRS)
    C2 = 2 * C

    def pair(p, off):
        a = x_sc[:, off + (2 * p) * GDN_DK: off + (2 * p + 1) * GDN_DK]
        b = x_sc[:, off + (2 * p + 1) * GDN_DK: off + (2 * p + 2) * GDN_DK]
        return jnp.concatenate([a, b], axis=0)

    def col(t, p, base=0):
        return t[:, base + p:base + p + 1]

    q2 = [pair(p, 0) for p in P]
    k2 = [pair(p, GDN_QK) for p in P]
    kb = [k2[p] * col(Bs, p) for p in P]
    vb = [pair(p, 2 * GDN_QK) * col(Bs, p) for p in P]
    dec = [jnp.where(strict, jnp.exp(jnp.where(strict, col(Gs, p, 8) - GT[8 + p:9 + p, :], 0.0)), 0.0) for p in P]
    kq = [_dot_nt(jnp.concatenate([kb[p], q2[p]], axis=0).astype(BF16), k2[p].astype(BF16)) for p in P]
    A = [kq[p][:C2] * dec[p] for p in P]
    qk = [(kq[p][C2:] * (dec[p] + jnp.where(diag, 1.0, 0.0))).astype(BF16) for p in P]

    Dg = [jnp.where(blk, A[p], 0.0) for p in P]
    E = [A[p] - Dg[p] for p in P]
    Q = [-Dg[p] for p in P]
    Dgb = [Dg[p].astype(BF16) for p in P]
    Dp = [_dot(Dgb[p], Dgb[p]) for p in P]
    for _ in range(2):
        Dpb = [Dp[p].astype(BF16) for p in P]
        st = [_dot(jnp.concatenate([Dpb[p], Q[p].astype(BF16)], axis=0), Dpb[p]) for p in P]
        Q = [Q[p] + Dp[p] + st[p][C2:] for p in P]
        Dp = [st[p][:C2] for p in P]
    Q = [Q[p] + Dp[p] + _dot(Q[p].astype(BF16), Dp[p].astype(BF16)) for p in P]
    rhs = [jnp.concatenate([kb[p] * col(eG, p, 8), vb[p]], axis=1) for p in P]
    er = [_dot(Q[p].astype(BF16), jnp.concatenate([E[p], rhs[p]], axis=1).astype(BF16)) for p in P]
    N = [E[p] + er[p][:, :C2] for p in P]
    y = [rhs[p] + er[p][:, C2:] for p in P]
    Nb = [N[p].astype(BF16) for p in P]
    ny = [_dot(Nb[p], jnp.concatenate([N[p], y[p]], axis=1).astype(BF16)) for p in P]
    zz = [y[p] - ny[p][:, C2:] for p in P]
    sol = [zz[p] + _dot(ny[p][:, :C2].astype(BF16), zz[p].astype(BF16)) for p in P]

    wq = [jnp.concatenate([sol[p][:, :GDN_DK], q2[p] * col(eG, p, 8)], axis=0).astype(BF16) for p in P]
    S_old = [S_sc[h] for h in range(GDN_HEADS)]
    rS = [_dot(wq[h // 2], S_old[h].astype(BF16)) for h in range(GDN_HEADS)]
    vn = [(sol[p][:, GDN_DK:] - jnp.where(top, rS[2 * p][:C2], rS[2 * p + 1][:C2])).astype(BF16) for p in P]
    kdT = [(k2[p] * col(eGl, p, 8)).T.astype(BF16) for p in P]
    zero = jnp.zeros((C2, GDN_DK), BF16)
    for p in P:
        o2 = jnp.where(top, rS[2 * p][C2:], rS[2 * p + 1][C2:]) + _dot(qk[p], vn[p])
        S_sc[2 * p] = S_old[2 * p] * egl[0:1, 8 + p:9 + p] + _dot(kdT[p], jnp.where(top, vn[p], zero))
        S_sc[2 * p + 1] = S_old[2 * p + 1] * egl[C:C + 1, 8 + p:9 + p] + _dot(kdT[p], jnp.where(top, zero, vn[p]))
        for e in range(2):
            h = 2 * p + e
            sl = slice(h * GDN_DK, (h + 1) * GDN_DK)
            o_ref[0, :, sl] = _gated_out(o2[e * C:(e + 1) * C], z_ref[0, :, sl], gn).astype(o_ref.dtype)


def _gdn_prompt(az, ba, conv_w, par, gn):
    B, T, _ = az.shape
    C = GDN_CHUNK
    return pl.pallas_call(
        _gdn_chunk_kernel,
        out_shape=[jax.ShapeDtypeStruct((B, T, GDN_QK), BF16),
                   jax.ShapeDtypeStruct((B, GDN_HEADS, GDN_DK, GDN_DK), F32)],
        grid=(B, T // C),
        in_specs=[pl.BlockSpec((1, C, CONV_CH), lambda b, c: (b, c, 0)),
                  pl.BlockSpec((1, C, CONV_CH), lambda b, c: (b, jnp.minimum(c + 1, T // C - 1), 0)),
                  pl.BlockSpec((1, C, GDN_QK), lambda b, c: (b, c, CONV_CH // GDN_QK)),
                  pl.BlockSpec((1, C, LANES), lambda b, c: (b, c, 0)),
                  pl.BlockSpec((GDN_CONV, CONV_CH), lambda b, c: (0, 0)),
                  pl.BlockSpec((8, LANES), lambda b, c: (0, 0)),
                  pl.BlockSpec((1, GDN_DK), lambda b, c: (0, 0))],
        out_specs=[pl.BlockSpec((1, C, GDN_QK), lambda b, c: (b, c, 0)),
                   pl.BlockSpec((1, GDN_HEADS, GDN_DK, GDN_DK), lambda b, c: (b, 0, 0, 0))],
        scratch_shapes=[pltpu.VMEM((C + 8, CONV_CH), F32), pltpu.VMEM((C, CONV_CH), F32),
                        pltpu.VMEM((C, CONV_CH), F32), pltpu.VMEM((GDN_HEADS, GDN_DK, GDN_DK), F32)],
        compiler_params=_params("parallel", "arbitrary"),
        name="gdn_chunked",
    )(az, az, az, ba, conv_w, par, gn)


def _gdn_step_kernel(qkv_ref, hist_ref, z_ref, ba_ref, s0_ref, cw_ref, par_ref, gn_ref, o_ref, s_ref):
    acc = qkv_ref[0] * cw_ref[GDN_CONV - 1:GDN_CONV, :]
    for j in range(GDN_CONV - 1):
        acc = acc + hist_ref[0, j:j + 1, :] * cw_ref[j:j + 1, :]
    x = _silu(acc)
    beta_t, g_t = _gdn_gates(ba_ref[0], par_ref[...])
    eg_t = jnp.exp(g_t)

    def head(off, h):
        return x[:, off + h * GDN_DK: off + (h + 1) * GDN_DK]

    rows = []
    for off, scale in ((GDN_QK, 1.0), (0, GDN_DK ** -0.5)):
        for h in range(GDN_HEADS):
            t = head(off, h)
            rows.append(t * lax.rsqrt(jnp.sum(t * t, axis=-1, keepdims=True) + NORM_EPS) * scale)
    ri = lax.broadcasted_iota(jnp.int32, (LANES, LANES), 0)
    tile = jnp.zeros((LANES, LANES), F32)
    for r, t in enumerate(rows):
        tile = jnp.where(ri == r, t, tile)
    cols = tile.T
    gn = gn_ref[...]
    for h in range(GDN_HEADS):
        lane = _HEAD_ORDER.index(h)
        kcol = cols[:, h:h + 1]
        qcol = cols[:, GDN_HEADS + h:GDN_HEADS + h + 1]
        S = s0_ref[0, h] * eg_t[:, 8 + lane:9 + lane]
        kv = jnp.sum(kcol * S, axis=0, keepdims=True)
        delta = beta_t[:, lane:lane + 1] * (head(2 * GDN_QK, h) - kv)
        S = S + kcol * delta
        s_ref[0, h] = S
        o = jnp.sum(qcol * S, axis=0, keepdims=True)
        sl = slice(h * GDN_DK, (h + 1) * GDN_DK)
        o_ref[0, :, sl] = _gated_out(o, z_ref[0, :, sl], gn).astype(o_ref.dtype)


def _gdn_sample(az, ba, hist, s0, conv_w, par, gn):
    N = az.shape[0]
    return pl.pallas_call(
        _gdn_step_kernel,
        out_shape=[jax.ShapeDtypeStruct((N, 1, GDN_QK), BF16),
                   jax.ShapeDtypeStruct((N, GDN_HEADS, GDN_DK, GDN_DK), F32)],
        grid=(N,),
        in_specs=[pl.BlockSpec((1, 1, CONV_CH), lambda b: (b, 0, 0)),
                  pl.BlockSpec((1, GDN_CONV - 1, CONV_CH), lambda b: (b, 0, 0)),
                  pl.BlockSpec((1, 1, GDN_QK), lambda b: (b, 0, CONV_CH // GDN_QK)),
                  pl.BlockSpec((1, 1, LANES), lambda b: (b, 0, 0)),
                  pl.BlockSpec((1, GDN_HEADS, GDN_DK, GDN_DK), lambda b: (b, 0, 0, 0)),
                  pl.BlockSpec((GDN_CONV, CONV_CH), lambda b: (0, 0)),
                  pl.BlockSpec((8, LANES), lambda b: (0, 0)),
                  pl.BlockSpec((1, GDN_DK), lambda b: (0, 0))],
        out_specs=[pl.BlockSpec((1, 1, GDN_QK), lambda b: (b, 0, 0)),
                   pl.BlockSpec((1, GDN_HEADS, GDN_DK, GDN_DK), lambda b: (b, 0, 0, 0))],
        compiler_params=_params("parallel"),
        name="gdn_step",
    )(az, hist, az, ba, s0, conv_w, par, gn)


DWA_BATCH = 4


def _dwa_blocks(refs, blocks, masks, first, scratch):
    q_ref, k_ref, v_ref = refs
    m_sc, l_sc, acc_sc = scratch
    cur_ok, prev_ok = masks
    scale = DWA_DH ** -0.5
    BL = DWA_BLOCK
    qb = [q_ref[0, cur, :].astype(BF16) for cur, _, _ in blocks]
    sc = [jnp.where(cur_ok, _dot_nt(qb[i], k_ref[0, cur, :].astype(BF16)) * scale, -jnp.inf)
          for i, (cur, _, _) in enumerate(blocks)]
    sp = [None if prev is None else
          jnp.where(prev_ok & on, _dot_nt(qb[i], k_ref[0, prev, :].astype(BF16)) * scale, -jnp.inf)
          for i, (_, prev, on) in enumerate(blocks)]
    m, l, pc, pp = [], [], [], []
    for i in range(len(blocks)):
        mi = jnp.max(sc[i], axis=-1, keepdims=True)
        if sp[i] is not None:
            mi = jnp.maximum(mi, jnp.max(sp[i], axis=-1, keepdims=True))
        p = jnp.exp(sc[i] - mi)
        li = jnp.sum(p, axis=-1, keepdims=True)
        pc.append(p.astype(BF16))
        if sp[i] is not None:
            p = jnp.exp(sp[i] - mi)
            li = li + jnp.sum(p, axis=-1, keepdims=True)
            pp.append(p.astype(BF16))
        else:
            pp.append(None)
        m.append(mi)
        l.append(li)
    acc = [_dot(pc[i], v_ref[0, cur, :].astype(BF16)) for i, (cur, _, _) in enumerate(blocks)]
    acc = [a if pp[i] is None else a + _dot(pp[i], v_ref[0, blocks[i][1], :].astype(BF16))
           for i, a in enumerate(acc)]
    for i, (cur, _, _) in enumerate(blocks):
        if first:
            m_sc[cur, :] = jnp.broadcast_to(m[i], (BL, LANES))
            l_sc[cur, :] = jnp.broadcast_to(l[i], (BL, LANES))
            acc_sc[cur, :] = acc[i]
        else:
            m_old = m_sc[cur, :]
            m_new = jnp.maximum(m_old, m[i])
            a = jnp.exp(m_old - m_new)
            b = jnp.exp(m[i] - m_new)
            m_sc[cur, :] = m_new
            l_sc[cur, :] = a * l_sc[cur, :] + b * l[i]
            acc_sc[cur, :] = a * acc_sc[cur, :] + b * acc[i]


def _dwa_kernel(q0_ref, q1_ref, q2_ref, k0_ref, v0_ref, k1_ref, v1_ref, k2_ref, v2_ref, *rest, n_cast):
    o_ref = rest[n_cast]
    m_sc, l_sc, acc_sc = rest[2 * n_cast + 1:]
    _cast_blocks(rest[:n_cast], rest[n_cast + 1:2 * n_cast + 1])
    BL = DWA_BLOCK
    T = o_ref.shape[1]
    ri = lax.broadcasted_iota(jnp.int32, (BL, BL), 0)
    ci = lax.broadcasted_iota(jnp.int32, (BL, BL), 1)
    masks = (ri >= ci, ci >= ri)
    q_refs, k_refs, v_refs = (q0_ref, q1_ref, q2_ref), (k0_ref, k1_ref, k2_ref), (v0_ref, v1_ref, v2_ref)
    scratch = (m_sc, l_sc, acc_sc)

    for gi, (_, dil) in enumerate(DWA_GROUPS):
        refs = (q_refs[gi], k_refs[gi], v_refs[gi])
        nb = T // dil // BL

        def rows(blk, r, dil=dil):
            start = blk * (BL * dil) + r
            if dil == 1:
                return pl.ds(pl.multiple_of(start, BL), BL)
            return pl.ds(start, BL, stride=dil)

        def block(n, r, nb=nb, rows=rows):
            if nb == 1:
                return rows(n, r), None, None
            return rows(n, r), rows(jnp.maximum(n - 1, 0), r), n > 0

        if dil >= DWA_BATCH:
            for r0 in range(0, dil, DWA_BATCH):
                def body(n, carry, r0=r0, refs=refs, block=block, first=gi == 0):
                    _dwa_blocks(refs, [block(n, r0 + t) for t in range(DWA_BATCH)], masks, first, scratch)
                    return carry
                if nb == 1:
                    body(0, 0)
                else:
                    lax.fori_loop(0, nb, body, 0)
        else:
            assert dil == 1 and nb % DWA_BATCH == 0

            def body(i, carry, refs=refs, block=block, first=gi == 0):
                _dwa_blocks(refs, [block(i * DWA_BATCH + t, 0) for t in range(DWA_BATCH)], masks, first, scratch)
                return carry
            lax.fori_loop(0, nb // DWA_BATCH, body, 0)

    o_ref[0] = (acc_sc[...] / l_sc[...]).astype(o_ref.dtype)


def _dwa_prompt(q, kvs, cast=()):
    B, T, _ = q.shape
    c_in, c_out, c_shapes = _cast_jobs(cast, lambda b, h: b * DWA_HEADS + h, B * DWA_HEADS)

    def col(c):
        return pl.BlockSpec((1, T, DWA_DH), lambda b, h: (b, 0, c(h)))

    ng = len(DWA_GROUPS)
    in_specs = [col(lambda h, g=g: g * DWA_HEADS + h) for g in range(ng)]
    args = [q] * ng
    for g in range(ng):
        in_specs += [col(lambda h: h), col(lambda h: DWA_HEADS + h)]
        args += [kvs[g], kvs[g]]
    args += [a for a, _, _, _ in cast]
    res = pl.pallas_call(
        functools.partial(_dwa_kernel, n_cast=len(cast)),
        out_shape=[jax.ShapeDtypeStruct((B, T, DWA_GW), BF16)] + c_shapes,
        grid=(B, DWA_HEADS),
        in_specs=in_specs + c_in,
        out_specs=[col(lambda h: h)] + c_out,
        scratch_shapes=[pltpu.VMEM((T, LANES), F32), pltpu.VMEM((T, LANES), F32), pltpu.VMEM((T, DWA_DH), F32)],
        compiler_params=_params(*(("arbitrary",) * 2 if cast else ("parallel",) * 2)),
        name="dwa_prompt",
    )(*args)
    return res[0], res[1:]


def _dwa_step_kernel(q_ref, n0_ref, n1_ref, n2_ref, c0_ref, c1_ref, c2_ref, o_ref):
    new_refs = (n0_ref, n1_ref, n2_ref)
    cache_refs = (c0_ref, c1_ref, c2_ref)
    scale = DWA_DH ** -0.5
    parts = []
    for gi in range(len(DWA_GROUPS)):
        q = q_ref[0, gi * DWA_HEADS:(gi + 1) * DWA_HEADS, :]
        k = cache_refs[gi][0, :, 0, 0]
        v = cache_refs[gi][0, :, 0, 1]
        s = jnp.sum(k * q[None], axis=-1, keepdims=True) * scale
        s_new = jnp.sum(new_refs[gi][0, 0] * q, axis=-1, keepdims=True) * scale
        m = jnp.maximum(jnp.max(s, axis=0), s_new)
        p = jnp.exp(s - m[None])
        p_new = jnp.exp(s_new - m)
        l = jnp.sum(p, axis=0) + p_new
        acc = jnp.sum(p * v, axis=0) + p_new * new_refs[gi][0, 1]
        parts.append((m, l, acc))
    mm = jnp.maximum(jnp.maximum(parts[0][0], parts[1][0]), parts[2][0])
    num = jnp.zeros((DWA_HEADS, DWA_DH), F32)
    den = jnp.zeros((DWA_HEADS, 1), F32)
    for m, l, acc in parts:
        e = jnp.exp(m - mm)
        num = num + e * acc
        den = den + e * l
    o_ref[0] = num / den


def _dwa_sample(q, kv_new, caches):
    N = q.shape[0]
    span = DWA_BLOCK
    in_specs = [pl.BlockSpec((1, len(DWA_GROUPS) * DWA_HEADS, DWA_DH), lambda b: (b, 0, 0))]
    in_specs += [pl.BlockSpec((1, 2, DWA_HEADS, DWA_DH), lambda b: (b, 0, 0, 0))] * 3
    views = []
    for (win, dil), cache in zip(DWA_GROUPS, caches):
        L = cache.shape[1]
        assert L == win and L // dil == span
        views.append(cache.reshape(N, span, dil, 2, DWA_HEADS, DWA_DH))
        in_specs.append(pl.BlockSpec((1, span, 1, 2, DWA_HEADS, DWA_DH), lambda b: (b, 0, 0, 0, 0, 0)))
    return pl.pallas_call(
        _dwa_step_kernel,
        out_shape=jax.ShapeDtypeStruct((N, DWA_HEADS, DWA_DH), F32),
        grid=(N,),
        in_specs=in_specs,
        out_specs=pl.BlockSpec((1, DWA_HEADS, DWA_DH), lambda b: (b, 0, 0)),
        compiler_params=_params("parallel"),
        name="dwa_step",
    )(q, *kv_new, *views)


def _mix_out_kernel(oa_ref, ob_ref, ga_ref, gb_ref, h_ref, gt_ref, nsh_ref, nsc_ref, ngain_ref,
                    wa_ref, wb_ref, wo_ref, *rest, n_cast):
    out_ref, nxt_ref = rest[n_cast:n_cast + 2]
    _cast_blocks(rest[:n_cast], rest[n_cast + 2:])
    ta = _dot(oa_ref[0], wa_ref[...])
    tb = _dot(ob_ref[0], wb_ref[...])
    merged = _sigmoid(ga_ref[0].astype(F32)) * ta + _sigmoid(gb_ref[0].astype(F32)) * tb
    h = h_ref[0] + gt_ref[0] * _dot(merged.astype(BF16), wo_ref[...])
    out_ref[0] = h
    nxt_ref[0] = _modulated(h, ngain_ref[...], nsc_ref[0], nsh_ref[0])


MIX_TM = 512


def _mix_out(oa, ob, gates, h, mod3, wa, wb, wo, next_gain, cast=()):
    G, R, _ = h.shape
    tm = min(MIX_TM, R)
    c_in, c_out, c_shapes = _cast_jobs(cast, lambda g, i: g * (R // tm) + i, G * (R // tm))

    def rows(width, col=0):
        return pl.BlockSpec((1, tm, width), lambda g, i: (g, i, col))

    def whole(shape):
        return pl.BlockSpec(shape, lambda g, i: (0, 0), pipeline_mode=pl.Buffered(1))

    res = pl.pallas_call(
        functools.partial(_mix_out_kernel, n_cast=len(cast)),
        out_shape=[jax.ShapeDtypeStruct((G, R, D_MODEL), F32), jax.ShapeDtypeStruct((G, R, D_MODEL), BF16)]
        + c_shapes,
        grid=(G, R // tm),
        in_specs=[rows(GDN_QK), rows(DWA_GW), rows(D_MODEL, 0), rows(D_MODEL, 1), rows(D_MODEL),
                  _mod_spec(mod3, R, tm, 5), _mod_spec(mod3, R, tm, 6), _mod_spec(mod3, R, tm, 7),
                  pl.BlockSpec((1, D_MODEL), lambda g, i: (0, 0)),
                  whole(wa.shape), whole(wb.shape), whole(wo.shape)] + c_in,
        out_specs=[rows(D_MODEL), rows(D_MODEL)] + c_out,
        compiler_params=_params(*(("arbitrary",) * 2 if cast else ("parallel",) * 2)),
        name="mix_out",
    )(oa, ob, gates, gates, h, mod3, mod3, mod3, next_gain.reshape(1, D_MODEL), wa, wb, wo,
      *[a for a, _, _, _ in cast])
    return res[0], res[1], res[2:]


_W_IN_COLS = _C_GB + D_MODEL
_W_IN_ALIGNED = _W_IN_COLS // CAST_BLOCK * CAST_BLOCK


def _mixer_weights(wt_f32, wt_bf, a_log, dt_bias):
    w = jnp.concatenate([wt_bf, wt_f32[_W_IN_ALIGNED:].astype(BF16)], axis=0)
    order = np.array(_HEAD_ORDER)
    ba_rows = np.concatenate([_C_B + order, _C_A + order])
    par = jnp.zeros((8, LANES), F32)
    par = par.at[0, 8:16].set(a_log[0][order]).at[1, 8:16].set(dt_bias[0][order])
    return dict(
        az=w[:_C_B],
        ba=jnp.pad(w[ba_rows], ((0, LANES - 2 * GDN_HEADS), (0, 0))),
        attn=jnp.concatenate(
            [w[c + g * DWA_GW:c + (g + 1) * DWA_GW] for g in range(len(DWA_GROUPS)) for c in (_C_K, _C_V)]
            + [w[_C_Q:_C_K]], axis=0),
        gates=w[_C_GA:],
        par=par,
    )


def _trunk(h1, xm, mod3, pos, W, norms, ffn2, *, sample_state=None):
    G, R, _ = h1.shape
    up2, down2 = ffn2
    tables = _rope_tables(pos)
    ba = _proj(xm, W["ba"], F32, tn=LANES, name="proj_ba")
    gates = _proj(xm, W["gates"], BF16, tn=1024, name="proj_gates")
    kvs, q = _attn_proj(xm, W["attn"], tables)
    if sample_state is None:
        az, conv_rows = _proj(xm, W["az"], BF16, tn=1024, tail=True, name="proj_az")
        o_a, s_new = _gdn_prompt(az, ba, W["conv"], W["par"], W["gn"])
        o_b, (up2,) = _dwa_prompt(q, kvs, cast=((up2, 1, 2 * D_FF, 4 * CAST_BLOCK),))
        down_cast = ((down2, 0, D_FF, 2 * CAST_BLOCK),)
    else:
        hist, s0, caches = sample_state
        conv_rows = _proj(xm, W["az"], F32, tn=1024, name="proj_az").reshape(R, 1, -1)
        o_a, s_new = _gdn_sample(conv_rows, ba.reshape(R, 1, LANES), hist, s0, W["conv"], W["par"], W["gn"])
        o_b = _dwa_sample(q.reshape(R, len(DWA_GROUPS) * DWA_HEADS, DWA_DH),
                          [kv.reshape(R, 2, DWA_HEADS, DWA_DH) for kv in kvs], caches)
        o_a = o_a.reshape(1, R, -1)
        o_b = o_b.reshape(1, R, DWA_GW).astype(BF16)
        down_cast = ()
    h2, xm2, cast_out = _mix_out(o_a, o_b, gates, h1, mod3, W["wa"], W["wb"], W["wo"], norms["ffn2"],
                                 cast=down_cast)
    if down_cast:
        (down2,) = cast_out
    y = _ffn_last(h2, xm2, mod3, up2, down2, norms["final"])
    return y, conv_rows[:, :, :CONV_CH], s_new, kvs, (up2, down2)


def kernel(x_prompt, x_sample, state_conv, state_delta, cache_kv_w128, cache_kv_w512, cache_kv_w2048, c_prompt, c_sample, w_ada, b_ada, norm_ffn1, w_ffn1_up, w_ffn1_down, norm_mix, w_in, conv_w, a_log, dt_bias, gdn_norm, w_proj_a, w_proj_b, w_out, norm_ffn2, w_ffn2_up, w_ffn2_down, norm_final):
    B, T, _ = x_prompt.shape
    N, S, _ = x_sample.shape
    assert S == 1 and T % (DWA_BLOCK * DWA_GROUPS[-1][1]) == 0
    norms = dict(ffn1=norm_ffn1[0], mix=norm_mix[0], ffn2=norm_ffn2[0], final=norm_final)

    mod = _ada(jnp.concatenate([c_prompt, c_sample], axis=0), w_ada[0], b_ada)
    mod_p = mod[:B].reshape(B, 1, N_MOD * D_MODEL)
    mod_s = mod[B:].reshape(1, N, N_MOD * D_MODEL)

    up1, down1 = w_ffn1_up[0].astype(BF16), w_ffn1_down[0].astype(BF16)
    w_in_t = jnp.swapaxes(w_in[0], 0, 1)
    h1_p, xm_p, w_in_bf = _ffn_first(x_prompt, mod_p, norms["ffn1"], up1, down1, norms["mix"],
                                     cast=((w_in_t, 0, _W_IN_ALIGNED, CAST_BLOCK),))
    W = _mixer_weights(w_in_t, w_in_bf, a_log, dt_bias)
    W.update(conv=conv_w[0], gn=gdn_norm,
             wa=w_proj_a[0].astype(BF16), wb=w_proj_b[0].astype(BF16), wo=w_out[0].astype(BF16))
    h1_s, xm_s = _ffn_first(x_sample.reshape(1, N, D_MODEL), mod_s, norms["ffn1"], up1, down1, norms["mix"])

    y_p, rows_p, s_p, kv_p, ffn2 = _trunk(h1_p, xm_p, mod_p, jnp.arange(T, dtype=jnp.int32), W, norms,
                                          (w_ffn2_up[0], w_ffn2_down[0]))
    caches = (cache_kv_w128[0], cache_kv_w512[0], cache_kv_w2048[0])
    y_s, rows_s, s_s, kv_s, _ = _trunk(h1_s, xm_s, mod_s, jnp.full((N,), PAST_LEN, dtype=jnp.int32), W, norms,
                                       ffn2, sample_state=(state_conv[0], state_delta[0], caches))

    keep = GDN_CONV - 1
    conv_p = rows_p[:, TAIL_ROWS - keep:][None]
    conv_s = jnp.concatenate([state_conv[0], rows_s], axis=1)[:, -keep:][None]
    kv_out_p = []
    for (win, _), kv in zip(DWA_GROUPS, kv_p):
        k = min(win, T)
        kv_out_p.append(kv[:, T - k:].reshape(1, B, k, 2, DWA_HEADS, DWA_DH))
    kv_out_s = [kv.reshape(1, N, 1, 2, DWA_HEADS, DWA_DH) for kv in kv_s]
    return (y_p, y_s.reshape(N, 1, D_MODEL), conv_p, s_p[None], *kv_out_p,
            conv_s, s_s[None], *kv_out_s)
```

```python
import functools

import jax
import jax.numpy as jnp
import numpy as np
from jax import lax
from jax.experimental import pallas as pl
from jax.experimental.pallas import tpu as pltpu

F32 = jnp.float32
BF16 = jnp.bfloat16

D_MODEL = 2048
D_FF = 5632
N_MOD = 9
NORM_EPS = 1e-6
PAST_LEN = 16384

GDN_HEADS = 8
GDN_DK = 128
GDN_CONV = 4
GDN_CHUNK = 64
GDN_QK = GDN_HEADS * GDN_DK
CONV_CH = 3 * GDN_QK

DWA_GROUPS = ((128, 1), (512, 4), (2048, 16))
DWA_HEADS = 4
DWA_DH = 128
DWA_GW = DWA_HEADS * DWA_DH
DWA_WIDTH = len(DWA_GROUPS) * DWA_GW
DWA_BLOCK = 128
ROPE_THETA = 10000.0

LANES = 128
MXU_COLS = 256
VMEM_LIMIT = 56 * 1024 * 1024

_C_Z = CONV_CH
_C_B = _C_Z + GDN_QK
_C_A = _C_B + GDN_HEADS
_C_Q = _C_A + GDN_HEADS
_C_K = _C_Q + DWA_WIDTH
_C_V = _C_K + DWA_WIDTH
_C_GA = _C_V + DWA_WIDTH
_C_GB = _C_GA + D_MODEL

_HEAD_ORDER = (0, 2, 4, 6, 1, 3, 5, 7)
_N_PAIRS = GDN_HEADS // 2


def _sigmoid(x):
    return 0.5 * jnp.tanh(0.5 * x) + 0.5


def _silu(x):
    return x * _sigmoid(x)


def _dot(a, b):
    return jnp.dot(a, b, preferred_element_type=F32)


def _dot_nt(a, b):
    return lax.dot_general(a, b, (((1,), (1,)), ((), ())), preferred_element_type=F32)


def _params(*sem):
    return pltpu.CompilerParams(dimension_semantics=sem, vmem_limit_bytes=VMEM_LIMIT)


def _ada_kernel(c_ref, w_ref, b_ref, o_ref):
    a = _silu(c_ref[...]).astype(BF16)
    o_ref[...] = _dot(a, w_ref[...].astype(BF16)) + b_ref[...]


def _ada(c, w, b):
    m, n = c.shape[0], w.shape[1]
    tn = 1024
    return pl.pallas_call(
        _ada_kernel,
        out_shape=jax.ShapeDtypeStruct((m, n), F32),
        grid=(n // tn,),
        in_specs=[pl.BlockSpec((m, D_MODEL), lambda j: (0, 0)),
                  pl.BlockSpec((D_MODEL, tn), lambda j: (0, j)),
                  pl.BlockSpec((1, tn), lambda j: (0, j))],
        out_specs=pl.BlockSpec((m, tn), lambda j: (0, j)),
        compiler_params=_params("arbitrary"),
        name="ada_mod",
    )(c, w, b)


def _rms(x):
    return x * lax.rsqrt(jnp.mean(x * x, axis=-1, keepdims=True) + NORM_EPS)


def _modulated(h, gain, scale, shift):
    return ((_rms(h) * gain) * (1.0 + scale) + shift).astype(BF16)


ROW_CHUNK = 32


def _row_chunks(n_rows):
    return [slice(r, min(r + ROW_CHUNK, n_rows)) for r in range(0, n_rows, ROW_CHUNK)]


def _rows_of(v, rows):
    return v if v.shape[0] == 1 else v[rows]


def _swiglu_step(xm, wg_ref, wu_ref, wd_ref, acc_sc):
    g = _dot(xm, wg_ref[...])
    u = _dot(xm, wu_ref[...])
    acc_sc[...] += _dot((_silu(g) * u).astype(BF16), wd_ref[...])


def _ffn_first_kernel(x_ref, sh_ref, sc_ref, gt_ref, gain_ref, wg_ref, wu_ref, wd_ref,
                      nsh_ref, nsc_ref, ngain_ref, *rest, n_cast):
    cast_in = rest[:n_cast]
    out_ref, nxt_ref = rest[n_cast:n_cast + 2]
    cast_out = rest[n_cast + 2:2 * n_cast + 2]
    xm_sc, acc_sc = rest[2 * n_cast + 2:]
    j = pl.program_id(2)

    tm = x_ref.shape[1]

    @pl.when(j == 0)
    def _():
        sc, sh = sc_ref[0], sh_ref[0]
        for rows in _row_chunks(tm):
            xm_sc[rows, :] = _modulated(x_ref[0, rows, :], gain_ref[...], _rows_of(sc, rows), _rows_of(sh, rows))
        acc_sc[...] = jnp.zeros_like(acc_sc)

    _cast_blocks(cast_in, cast_out)
    _swiglu_step(xm_sc[...], wg_ref, wu_ref, wd_ref, acc_sc)

    @pl.when(j == pl.num_programs(2) - 1)
    def _():
        half_gate, nsc, nsh = 0.5 * gt_ref[0], nsc_ref[0], nsh_ref[0]
        for rows in _row_chunks(tm):
            h = x_ref[0, rows, :] + _rows_of(half_gate, rows) * acc_sc[rows, :]
            out_ref[0, rows, :] = h
            nxt_ref[0, rows, :] = _modulated(h, ngain_ref[...], _rows_of(nsc, rows), _rows_of(nsh, rows))


def _ffn_last_kernel(x_ref, xm_ref, gt_ref, wg_ref, wu_ref, wd_ref, ngain_ref, out_ref, acc_sc):
    j = pl.program_id(2)

    @pl.when(j == 0)
    def _():
        acc_sc[...] = jnp.zeros_like(acc_sc)

    _swiglu_step(xm_ref[0], wg_ref, wu_ref, wd_ref, acc_sc)

    @pl.when(j == pl.num_programs(2) - 1)
    def _():
        half_gate = 0.5 * gt_ref[0]
        for rows in _row_chunks(x_ref.shape[1]):
            h = x_ref[0, rows, :] + _rows_of(half_gate, rows) * acc_sc[rows, :]
            out_ref[0, rows, :] = _rms(h) * ngain_ref[...]


FFN_TM = 512
FFN_TF = 512


def _mod_spec(mod3, R, tm, k):
    if mod3.shape[1] == R:
        return pl.BlockSpec((1, tm, D_MODEL), lambda g, i, *_: (g, i, k))
    return pl.BlockSpec((1, 1, D_MODEL), lambda g, i, *_: (g, 0, k))


def _ffn_specs(R):
    tm = min(FFN_TM, R)
    nff = D_FF // FFN_TF
    rows = pl.BlockSpec((1, tm, D_MODEL), lambda g, i, j: (g, i, 0))
    vec = pl.BlockSpec((1, D_MODEL), lambda g, i, j: (0, 0))
    weights = [pl.BlockSpec((D_MODEL, FFN_TF), lambda g, i, j: (0, j)),
               pl.BlockSpec((D_MODEL, FFN_TF), lambda g, i, j: (0, j + nff)),
               pl.BlockSpec((FFN_TF, D_MODEL), lambda g, i, j: (j, 0))]
    return tm, nff, rows, vec, weights


CAST_BLOCK = 128


def _cast_jobs(arrays, step_of, total_steps):
    in_specs, out_specs, out_shapes = [], [], []
    start = 0
    for arr, axis, extent, slab in arrays:
        assert extent % slab == 0 and slab % CAST_BLOCK == 0
        nblk = extent // slab
        shape = tuple(slab if d == axis else n for d, n in enumerate(arr.shape))

        def index(*grid, start=start, nblk=nblk, axis=axis):
            blk = jnp.clip(step_of(*grid) - start, 0, nblk - 1)
            return (blk, 0) if axis == 0 else (0, blk)

        in_specs.append(pl.BlockSpec(shape, index))
        out_specs.append(pl.BlockSpec(shape, index))
        out_shapes.append(jax.ShapeDtypeStruct(
            tuple(extent if d == axis else n for d, n in enumerate(arr.shape)), BF16))
        start += nblk
    assert start <= total_steps, (start, total_steps)
    return in_specs, out_specs, out_shapes


def _cast_blocks(cast_in, cast_out):
    for src, dst in zip(cast_in, cast_out):
        dst[...] = src[...].astype(BF16)


def _ffn_first(x3, mod3, gain, w_up, w_down, next_gain, cast=()):
    G, R, _ = x3.shape
    tm, nff, rows, vec, weights = _ffn_specs(R)
    ms = [_mod_spec(mod3, R, tm, k) for k in range(5)]
    c_in, c_out, c_shapes = _cast_jobs(cast, lambda g, i, j: (g * (R // tm) + i) * nff + j, G * (R // tm) * nff)
    sem = ("arbitrary",) * 3 if cast else ("parallel", "parallel", "arbitrary")
    return pl.pallas_call(
        functools.partial(_ffn_first_kernel, n_cast=len(cast)),
        out_shape=[jax.ShapeDtypeStruct((G, R, D_MODEL), F32), jax.ShapeDtypeStruct((G, R, D_MODEL), BF16)]
        + c_shapes,
        grid=(G, R // tm, nff),
        in_specs=[rows, ms[0], ms[1], ms[2], vec] + weights + [ms[3], ms[4], vec] + c_in,
        out_specs=[rows, rows] + c_out,
        scratch_shapes=[pltpu.VMEM((tm, D_MODEL), BF16), pltpu.VMEM((tm, D_MODEL), F32)],
        compiler_params=_params(*sem),
        name="ffn_first",
    )(x3, mod3, mod3, mod3, gain.reshape(1, D_MODEL), w_up, w_up, w_down, mod3, mod3,
      next_gain.reshape(1, D_MODEL), *[a for a, _, _, _ in cast])


def _ffn_last(x3, xm3, mod3, w_up, w_down, final_gain):
    G, R, _ = x3.shape
    tm, nff, rows, vec, weights = _ffn_specs(R)
    return pl.pallas_call(
        _ffn_last_kernel,
        out_shape=jax.ShapeDtypeStruct((G, R, D_MODEL), F32),
        grid=(G, R // tm, nff),
        in_specs=[rows, rows, _mod_spec(mod3, R, tm, 8)] + weights + [vec],
        out_specs=rows,
        scratch_shapes=[pltpu.VMEM((tm, D_MODEL), F32)],
        compiler_params=_params("parallel", "parallel", "arbitrary"),
        name="ffn_last",
    )(x3, xm3, mod3, w_up, w_up, w_down, final_gain.reshape(1, D_MODEL))


def _rope_tile(x, cosf, sinf):
    return x * cosf + pltpu.roll(x, DWA_DH // 2, axis=1) * sinf


TAIL_ROWS = 8


def _proj_kernel(a_ref, wt_ref, *rest, tail):
    acc = _dot_nt(a_ref[0], wt_ref[...])
    rest[0][0] = acc.astype(rest[0].dtype)
    if tail:
        rest[1][0] = acc[acc.shape[0] - TAIL_ROWS:]


PROJ_TM = 1024


def _proj(xm3, wt, out_dtype, *, tn=512, tail=False, name="proj"):
    G, R, _ = xm3.shape
    n = wt.shape[0]
    tm = min(PROJ_TM, R)
    out_shape = jax.ShapeDtypeStruct((G, R, n), out_dtype)
    out_specs = pl.BlockSpec((1, tm, tn), lambda g, i, j: (g, i, j))
    if tail:
        out_shape = [out_shape, jax.ShapeDtypeStruct((G, R // tm * TAIL_ROWS, n), F32)]
        out_specs = [out_specs, pl.BlockSpec((1, TAIL_ROWS, tn), lambda g, i, j: (g, i, j))]
    res = pl.pallas_call(
        functools.partial(_proj_kernel, tail=tail),
        out_shape=out_shape,
        grid=(G, R // tm, n // tn),
        in_specs=[pl.BlockSpec((1, tm, D_MODEL), lambda g, i, j: (g, i, 0)),
                  pl.BlockSpec((tn, D_MODEL), lambda g, i, j: (j, 0))],
        out_specs=out_specs,
        compiler_params=_params("parallel", "parallel", "arbitrary"),
        name=name,
    )(xm3, wt)
    if tail:
        return res[0], res[1][:, -TAIL_ROWS:]
    return res


ATTN_TILES = tuple((2 * g, 2 * g + 2) for g in range(len(DWA_GROUPS))) + ((2 * len(DWA_GROUPS), 3 * len(DWA_GROUPS)),)


def _attn_proj_kernel(a_ref, wt_ref, cos_ref, sin_ref, *o_refs):
    j = pl.program_id(2)
    acc = _dot_nt(a_ref[0], wt_ref[...])
    cosf, sinf = cos_ref[...], sin_ref[...]
    for o_ref, (lo, hi) in zip(o_refs, ATTN_TILES):
        is_q = lo == ATTN_TILES[-1][0]

        @pl.when((j >= lo) & (j < hi))
        def _(o_ref=o_ref, lo=lo, is_q=is_q):
            for h in range(0, DWA_GW, DWA_DH):
                x = acc[:, h:h + DWA_DH]
                r = _rope_tile(x, cosf, sinf)
                if not is_q:
                    r = jnp.where(j == lo, r, x)
                o_ref[0, :, h:h + DWA_DH] = r


def _attn_proj(xm3, wt, tables):
    G, R, _ = xm3.shape
    tm = min(PROJ_TM, R)
    tab = pl.BlockSpec((tm, DWA_DH), lambda g, i, j: (i, 0))
    out_shape, out_specs = [], []
    for lo, hi in ATTN_TILES:
        out_shape.append(jax.ShapeDtypeStruct((G, R, (hi - lo) * DWA_GW), F32))
        out_specs.append(pl.BlockSpec((1, tm, DWA_GW),
                                      lambda g, i, j, lo=lo, hi=hi: (g, i, jnp.clip(j - lo, 0, hi - lo - 1))))
    res = pl.pallas_call(
        _attn_proj_kernel,
        out_shape=out_shape,
        grid=(G, R // tm, ATTN_TILES[-1][1]),
        in_specs=[pl.BlockSpec((1, tm, D_MODEL), lambda g, i, j: (g, i, 0)),
                  pl.BlockSpec((DWA_GW, D_MODEL), lambda g, i, j: (j, 0)), tab, tab],
        out_specs=out_specs,
        compiler_params=_params("parallel", "arbitrary", "arbitrary"),
        name="proj_attn",
    )(xm3, wt, *tables)
    return res[:-1], res[-1]


def _rope_tables(pos):
    half = DWA_DH // 2
    inv = jnp.power(ROPE_THETA, -jnp.arange(half, dtype=F32) / half)
    ang = pos.astype(F32)[:, None] * inv[None, :]
    cos, sin = jnp.cos(ang), jnp.sin(ang)
    return jnp.concatenate([cos, cos], axis=-1), jnp.concatenate([-sin, sin], axis=-1)


def _gdn_gates(ba, par):
    beta = _sigmoid(ba)
    x = ba + par[1:2]
    softplus = jnp.maximum(x, 0.0) + jnp.log1p(jnp.exp(-jnp.abs(x)))
    return beta, -jnp.exp(par[0:1]) * softplus


def _gated_out(o, z, gn):
    return (_rms(o) * gn) * _silu(z.astype(F32))


def _l2norm(t):
    return t * lax.rsqrt(jnp.sum(t * t, axis=-1, keepdims=True) + NORM_EPS)


GDN_HIST = 16


def _gdn_frontend(hist, raw, cw_ref, buf_sc, x_out):
    C = GDN_CHUNK
    buf_sc[0:GDN_HIST, :] = hist
    buf_sc[GDN_HIST:GDN_HIST + C, :] = raw
    taps = GDN_CONV - 1
    ri = lax.broadcasted_iota(jnp.int32, (taps * C, buf_sc.shape[0]), 0)
    ci = lax.broadcasted_iota(jnp.int32, (taps * C, buf_sc.shape[0]), 1)
    shift = jnp.where(ci == GDN_HIST + ri % C - (ri // C + 1), 1.0, 0.0).astype(BF16)
    back = _dot(shift, buf_sc[...])
    acc = raw.astype(F32) * cw_ref[taps:taps + 1, :]
    for s in range(taps):
        acc = acc + back[s * C:(s + 1) * C] * cw_ref[taps - 1 - s:taps - s, :]
    x = _silu(acc)
    for h in range(2 * GDN_HEADS):
        sl = slice(h * GDN_DK, (h + 1) * GDN_DK)
        x_out[:, sl] = _l2norm(x[:, sl]) * (GDN_DK ** -0.5 if h < GDN_HEADS else 1.0)
    x_out[:, 2 * GDN_QK:] = x[:, 2 * GDN_QK:]


def _gdn_chunk_kernel(cur_ref, nxt_ref, z_ref, ba_ref, cw_ref, par_ref, gn_ref, o_ref, s_ref,
                      buf_sc, xa_sc, xb_sc, S_sc):
    C = GDN_CHUNK
    c = pl.program_id(1)

    @pl.when(c == 0)
    def _():
        S_sc[...] = jnp.zeros_like(S_sc)
        buf_sc[...] = jnp.zeros_like(buf_sc)
        _gdn_frontend(jnp.zeros((GDN_HIST, CONV_CH), BF16), cur_ref[0], cw_ref, buf_sc, xa_sc)

    def step(x_sc, x_next):
        _gdn_frontend(cur_ref[0, C - GDN_HIST:C, :], nxt_ref[0], cw_ref, buf_sc, x_next)
        _gdn_chain(x_sc, z_ref, ba_ref, par_ref, gn_ref, o_ref, S_sc)

    pl.when(c % 2 == 0)(lambda: step(xa_sc, xb_sc))
    pl.when(c % 2 == 1)(lambda: step(xb_sc, xa_sc))

    @pl.when(c == pl.num_programs(1) - 1)
    def _():
        s_ref[0] = S_sc[...]


def _gdn_chain(x_sc, z_ref, ba_ref, par_ref, gn_ref, o_ref, S_sc):
    C = GDN_CHUNK
    beta_t, g_t = _gdn_gates(ba_ref[0], par_ref[...])
    row = lax.broadcasted_iota(jnp.int32, (C, LANES), 0)
    G = g_t
    s = 1
    while s < C:
        G = G + jnp.where(row >= s, pltpu.roll(G, s, axis=0), 0.0)
        s *= 2
    shift = LANES - _N_PAIRS
    Gs = jnp.concatenate([G, pltpu.roll(G, shift, axis=1)], axis=0)
    Bs = jnp.concatenate([beta_t, pltpu.roll(beta_t, shift, axis=1)], axis=0)
    GT = Gs.T
    r2 = lax.broadcasted_iota(jnp.int32, (2 * C, LANES), 0)
    Glast = jnp.where(r2 < C, Gs[C - 1:C, :], Gs[2 * C - 1:2 * C, :])
    eG = jnp.exp(Gs)
    eGl = jnp.exp(Glast - Gs)
    egl = jnp.exp(Glast)

    ii = lax.broadcasted_iota(jnp.int32, (2 * C, 2 * C), 0)
    jj = lax.broadcasted_iota(jnp.int32, (2 * C, 2 * C), 1)
    same = (ii // C) == (jj // C)
    strict = same & (ii > jj)
    diag = ii == jj
    blk = (ii // 16) == (jj // 16)
    top = r2 < C
    gn = gn_ref[...]

    P = range(_N_PAIRS)
    C2 = 2 * C

    def pair(p, off):
        a = x_sc[:, off + (2 * p) * GDN_DK: off + (2 * p + 1) * GDN_DK]
        b = x_sc[:, off + (2 * p + 1) * GDN_DK: off + (2 * p + 2) * GDN_DK]
        return jnp.concatenate([a, b], axis=0)

    def col(t, p, base=0):
        return t[:, base + p:base + p + 1]

    q2 = [pair(p, 0) for p in P]
    k2 = [pair(p, GDN_QK) for p in P]
    kb = [k2[p] * col(Bs, p) for p in P]
    vb = [pair(p, 2 * GDN_QK) * col(Bs, p) for p in P]
    dec = [jnp.where(strict, jnp.exp(jnp.where(strict, col(Gs, p, 8) - GT[8 + p:9 + p, :], 0.0)), 0.0) for p in P]
    kq = [_dot_nt(jnp.concatenate([kb[p], q2[p]], axis=0).astype(BF16), k2[p].astype(BF16)) for p in P]
    A = [kq[p][:C2] * dec[p] for p in P]
    qk = [(kq[p][C2:] * (dec[p] + jnp.where(diag, 1.0, 0.0))).astype(BF16) for p in P]

    Dg = [jnp.where(blk, A[p], 0.0) for p in P]
    E = [A[p] - Dg[p] for p in P]
    Q = [-Dg[p] for p in P]
    Dgb = [Dg[p].astype(BF16) for p in P]
    Dp = [_dot(Dgb[p], Dgb[p]) for p in P]
    for _ in range(2):
        Dpb = [Dp[p].astype(BF16) for p in P]
        st = [_dot(jnp.concatenate([Dpb[p], Q[p].astype(BF16)], axis=0), Dpb[p]) for p in P]
        Q = [Q[p] + Dp[p] + st[p][C2:] for p in P]
        Dp = [st[p][:C2] for p in P]
    Q = [Q[p] + Dp[p] + _dot(Q[p].astype(BF16), Dp[p].astype(BF16)) for p in P]
    rhs = [jnp.concatenate([kb[p] * col(eG, p, 8), vb[p]], axis=1) for p in P]
    er = [_dot(Q[p].astype(BF16), jnp.concatenate([E[p], rhs[p]], axis=1).astype(BF16)) for p in P]
    N = [E[p] + er[p][:, :C2] for p in P]
    y = [rhs[p] + er[p][:, C2:] for p in P]
    Nb = [N[p].astype(BF16) for p in P]
    ny = [_dot(Nb[p], jnp.concatenate([N[p], y[p]], axis=1).astype(BF16)) for p in P]
    zz = [y[p] - ny[p][:, C2:] for p in P]
    sol = [zz[p] + _dot(ny[p][:, :C2].astype(BF16), zz[p].astype(BF16)) for p in P]

    wq = [jnp.concatenate([sol[p][:, :GDN_DK], q2[p] * col(eG, p, 8)], axis=0).astype(BF16) for p in P]
    S_old = [S_sc[h] for h in range(GDN_HEADS)]
    rS = [_dot(wq[h // 2], S_old[h].astype(BF16)) for h in range(GDN_HEADS)]
    vn = [(sol[p][:, GDN_DK:] - jnp.where(top, rS[2 * p][:C2], rS[2 * p + 1][:C2])).astype(BF16) for p in P]
    kdT = [(k2[p] * col(eGl, p, 8)).T.astype(BF16) for p in P]
    zero = jnp.zeros((C2, GDN_DK), BF16)
    for p in P:
        o2 = jnp.where(top, rS[2 * p][C2:], rS[2 * p + 1][C2:]) + _dot(qk[p], vn[p])
        S_sc[2 * p] = S_old[2 * p] * egl[0:1, 8 + p:9 + p] + _dot(kdT[p], jnp.where(top, vn[p], zero))
        S_sc[2 * p + 1] = S_old[2 * p + 1] * egl[C:C + 1, 8 + p:9 + p] + _dot(kdT[p], jnp.where(top, zero, vn[p]))
        for e in range(2):
            h = 2 * p + e
            sl = slice(h * GDN_DK, (h + 1) * GDN_DK)
            o_ref[0, :, sl] = _gated_out(o2[e * C:(e + 1) * C], z_ref[0, :, sl], gn).astype(o_ref.dtype)


def _gdn_prompt(az, ba, conv_w, par, gn):
    B, T, _ = az.shape
    C = GDN_CHUNK
    return pl.pallas_call(
        _gdn_chunk_kernel,
        out_shape=[jax.ShapeDtypeStruct((B, T, GDN_QK), BF16),
                   jax.ShapeDtypeStruct((B, GDN_HEADS, GDN_DK, GDN_DK), F32)],
        grid=(B, T // C),
        in_specs=[pl.BlockSpec((1, C, CONV_CH), lambda b, c: (b, c, 0)),
                  pl.BlockSpec((1, C, CONV_CH), lambda b, c: (b, jnp.minimum(c + 1, T // C - 1), 0)),
                  pl.BlockSpec((1, C, GDN_QK), lambda b, c: (b, c, CONV_CH // GDN_QK)),
                  pl.BlockSpec((1, C, LANES), lambda b, c: (b, c, 0)),
                  pl.BlockSpec((GDN_CONV, CONV_CH), lambda b, c: (0, 0)),
                  pl.BlockSpec((8, LANES), lambda b, c: (0, 0)),
                  pl.BlockSpec((1, GDN_DK), lambda b, c: (0, 0))],
        out_specs=[pl.BlockSpec((1, C, GDN_QK), lambda b, c: (b, c, 0)),
                   pl.BlockSpec((1, GDN_HEADS, GDN_DK, GDN_DK), lambda b, c: (b, 0, 0, 0))],
        scratch_shapes=[pltpu.VMEM((2 * C, CONV_CH), BF16), pltpu.VMEM((C, CONV_CH), F32),
                        pltpu.VMEM((C, CONV_CH), F32), pltpu.VMEM((GDN_HEADS, GDN_DK, GDN_DK), F32)],
        compiler_params=_params("parallel", "arbitrary"),
        name="gdn_chunked",
    )(az, az, az, ba, conv_w, par, gn)


def _gdn_step_kernel(qkv_ref, hist_ref, z_ref, ba_ref, s0_ref, cw_ref, par_ref, gn_ref, o_ref, s_ref):
    acc = qkv_ref[0] * cw_ref[GDN_CONV - 1:GDN_CONV, :]
    for j in range(GDN_CONV - 1):
        acc = acc + hist_ref[0, j:j + 1, :] * cw_ref[j:j + 1, :]
    x = _silu(acc)
    beta_t, g_t = _gdn_gates(ba_ref[0], par_ref[...])
    eg_t = jnp.exp(g_t)

    def head(off, h):
        return x[:, off + h * GDN_DK: off + (h + 1) * GDN_DK]

    rows = []
    for off, scale in ((GDN_QK, 1.0), (0, GDN_DK ** -0.5)):
        for h in range(GDN_HEADS):
            t = head(off, h)
            rows.append(t * lax.rsqrt(jnp.sum(t * t, axis=-1, keepdims=True) + NORM_EPS) * scale)
    ri = lax.broadcasted_iota(jnp.int32, (LANES, LANES), 0)
    tile = jnp.zeros((LANES, LANES), F32)
    for r, t in enumerate(rows):
        tile = jnp.where(ri == r, t, tile)
    cols = tile.T
    gn = gn_ref[...]
    for h in range(GDN_HEADS):
        lane = _HEAD_ORDER.index(h)
        kcol = cols[:, h:h + 1]
        qcol = cols[:, GDN_HEADS + h:GDN_HEADS + h + 1]
        S = s0_ref[0, h] * eg_t[:, 8 + lane:9 + lane]
        kv = jnp.sum(kcol * S, axis=0, keepdims=True)
        delta = beta_t[:, lane:lane + 1] * (head(2 * GDN_QK, h) - kv)
        S = S + kcol * delta
        s_ref[0, h] = S
        o = jnp.sum(qcol * S, axis=0, keepdims=True)
        sl = slice(h * GDN_DK, (h + 1) * GDN_DK)
        o_ref[0, :, sl] = _gated_out(o, z_ref[0, :, sl], gn).astype(o_ref.dtype)


def _gdn_sample(az, ba, hist, s0, conv_w, par, gn):
    N = az.shape[0]
    return pl.pallas_call(
        _gdn_step_kernel,
        out_shape=[jax.ShapeDtypeStruct((N, 1, GDN_QK), BF16),
                   jax.ShapeDtypeStruct((N, GDN_HEADS, GDN_DK, GDN_DK), F32)],
        grid=(N,),
        in_specs=[pl.BlockSpec((1, 1, CONV_CH), lambda b: (b, 0, 0)),
                  pl.BlockSpec((1, GDN_CONV - 1, CONV_CH), lambda b: (b, 0, 0)),
                  pl.BlockSpec((1, 1, GDN_QK), lambda b: (b, 0, CONV_CH // GDN_QK)),
                  pl.BlockSpec((1, 1, LANES), lambda b: (b, 0, 0)),
                  pl.BlockSpec((1, GDN_HEADS, GDN_DK, GDN_DK), lambda b: (b, 0, 0, 0)),
                  pl.BlockSpec((GDN_CONV, CONV_CH), lambda b: (0, 0)),
                  pl.BlockSpec((8, LANES), lambda b: (0, 0)),
                  pl.BlockSpec((1, GDN_DK), lambda b: (0, 0))],
        out_specs=[pl.BlockSpec((1, 1, GDN_QK), lambda b: (b, 0, 0)),
                   pl.BlockSpec((1, GDN_HEADS, GDN_DK, GDN_DK), lambda b: (b, 0, 0, 0))],
        compiler_params=_params("parallel"),
        name="gdn_step",
    )(az, hist, az, ba, s0, conv_w, par, gn)


DWA_BATCH = 4


def _dwa_blocks(refs, blocks, masks, first, scratch):
    q_ref, k_ref, v_ref = refs
    m_sc, l_sc, acc_sc = scratch
    cur_ok, prev_ok = masks
    scale = DWA_DH ** -0.5
    BL = DWA_BLOCK
    qb = [q_ref[0, cur, :].astype(BF16) for cur, _, _ in blocks]
    sc = [jnp.where(cur_ok, _dot_nt(qb[i], k_ref[0, cur, :].astype(BF16)) * scale, -jnp.inf)
          for i, (cur, _, _) in enumerate(blocks)]
    sp = [None if prev is None else
          jnp.where(prev_ok & on, _dot_nt(qb[i], k_ref[0, prev, :].astype(BF16)) * scale, -jnp.inf)
          for i, (_, prev, on) in enumerate(blocks)]
    m, l, pc, pp = [], [], [], []
    for i in range(len(blocks)):
        mi = jnp.max(sc[i], axis=-1, keepdims=True)
        if sp[i] is not None:
            mi = jnp.maximum(mi, jnp.max(sp[i], axis=-1, keepdims=True))
        p = jnp.exp(sc[i] - mi)
        li = jnp.sum(p, axis=-1, keepdims=True)
        pc.append(p.astype(BF16))
        if sp[i] is not None:
            p = jnp.exp(sp[i] - mi)
            li = li + jnp.sum(p, axis=-1, keepdims=True)
            pp.append(p.astype(BF16))
        else:
            pp.append(None)
        m.append(mi)
        l.append(li)
    acc = [_dot(pc[i], v_ref[0, cur, :].astype(BF16)) for i, (cur, _, _) in enumerate(blocks)]
    acc = [a if pp[i] is None else a + _dot(pp[i], v_ref[0, blocks[i][1], :].astype(BF16))
           for i, a in enumerate(acc)]
    for i, (cur, _, _) in enumerate(blocks):
        if first:
            m_sc[cur, :] = jnp.broadcast_to(m[i], (BL, LANES))
            l_sc[cur, :] = jnp.broadcast_to(l[i], (BL, LANES))
            acc_sc[cur, :] = acc[i]
        else:
            m_old = m_sc[cur, :]
            m_new = jnp.maximum(m_old, m[i])
            a = jnp.exp(m_old - m_new)
            b = jnp.exp(m[i] - m_new)
            m_sc[cur, :] = m_new
            l_sc[cur, :] = a * l_sc[cur, :] + b * l[i]
            acc_sc[cur, :] = a * acc_sc[cur, :] + b * acc[i]


def _dwa_kernel(q0_ref, q1_ref, q2_ref, k0_ref, v0_ref, k1_ref, v1_ref, k2_ref, v2_ref, *rest, n_cast):
    o_ref = rest[n_cast]
    m_sc, l_sc, acc_sc = rest[2 * n_cast + 1:]
    _cast_blocks(rest[:n_cast], rest[n_cast + 1:2 * n_cast + 1])
    BL = DWA_BLOCK
    T = o_ref.shape[1]
    ri = lax.broadcasted_iota(jnp.int32, (BL, BL), 0)
    ci = lax.broadcasted_iota(jnp.int32, (BL, BL), 1)
    masks = (ri >= ci, ci >= ri)
    q_refs, k_refs, v_refs = (q0_ref, q1_ref, q2_ref), (k0_ref, k1_ref, k2_ref), (v0_ref, v1_ref, v2_ref)
    scratch = (m_sc, l_sc, acc_sc)

    for gi, (_, dil) in enumerate(DWA_GROUPS):
        refs = (q_refs[gi], k_refs[gi], v_refs[gi])
        nb = T // dil // BL

        def rows(blk, r, dil=dil):
            start = blk * (BL * dil) + r
            if dil == 1:
                return pl.ds(pl.multiple_of(start, BL), BL)
            return pl.ds(start, BL, stride=dil)

        def block(n, r, nb=nb, rows=rows):
            if nb == 1:
                return rows(n, r), None, None
            return rows(n, r), rows(jnp.maximum(n - 1, 0), r), n > 0

        if dil >= DWA_BATCH:
            for r0 in range(0, dil, DWA_BATCH):
                def body(n, carry, r0=r0, refs=refs, block=block, first=gi == 0):
                    _dwa_blocks(refs, [block(n, r0 + t) for t in range(DWA_BATCH)], masks, first, scratch)
                    return carry
                if nb == 1:
                    body(0, 0)
                else:
                    lax.fori_loop(0, nb, body, 0)
        else:
            assert dil == 1 and nb % DWA_BATCH == 0

            def body(i, carry, refs=refs, block=block, first=gi == 0):
                _dwa_blocks(refs, [block(i * DWA_BATCH + t, 0) for t in range(DWA_BATCH)], masks, first, scratch)
                return carry
            lax.fori_loop(0, nb // DWA_BATCH, body, 0)

    o_ref[0] = (acc_sc[...] / l_sc[...]).astype(o_ref.dtype)


def _dwa_prompt(q, kvs, cast=()):
    B, T, _ = q.shape
    c_in, c_out, c_shapes = _cast_jobs(cast, lambda b, h: b * DWA_HEADS + h, B * DWA_HEADS)

    def col(c):
        return pl.BlockSpec((1, T, DWA_DH), lambda b, h: (b, 0, c(h)))

    ng = len(DWA_GROUPS)
    in_specs = [col(lambda h, g=g: g * DWA_HEADS + h) for g in range(ng)]
    args = [q] * ng
    for g in range(ng):
        in_specs += [col(lambda h: h), col(lambda h: DWA_HEADS + h)]
        args += [kvs[g], kvs[g]]
    args += [a for a, _, _, _ in cast]
    res = pl.pallas_call(
        functools.partial(_dwa_kernel, n_cast=len(cast)),
        out_shape=[jax.ShapeDtypeStruct((B, T, DWA_GW), BF16)] + c_shapes,
        grid=(B, DWA_HEADS),
        in_specs=in_specs + c_in,
        out_specs=[col(lambda h: h)] + c_out,
        scratch_shapes=[pltpu.VMEM((T, LANES), F32), pltpu.VMEM((T, LANES), F32), pltpu.VMEM((T, DWA_DH), F32)],
        compiler_params=_params(*(("arbitrary",) * 2 if cast else ("parallel",) * 2)),
        name="dwa_prompt",
    )(*args)
    return res[0], res[1:]


def _dwa_step_kernel(q_ref, n0_ref, n1_ref, n2_ref, c0_ref, c1_ref, c2_ref, o_ref):
    new_refs = (n0_ref, n1_ref, n2_ref)
    cache_refs = (c0_ref, c1_ref, c2_ref)
    scale = DWA_DH ** -0.5
    parts = []
    for gi in range(len(DWA_GROUPS)):
        q = q_ref[0, gi * DWA_HEADS:(gi + 1) * DWA_HEADS, :]
        k = cache_refs[gi][0, :, 0, 0]
        v = cache_refs[gi][0, :, 0, 1]
        s = jnp.sum(k * q[None], axis=-1, keepdims=True) * scale
        s_new = jnp.sum(new_refs[gi][0, 0] * q, axis=-1, keepdims=True) * scale
        m = jnp.maximum(jnp.max(s, axis=0), s_new)
        p = jnp.exp(s - m[None])
        p_new = jnp.exp(s_new - m)
        l = jnp.sum(p, axis=0) + p_new
        acc = jnp.sum(p * v, axis=0) + p_new * new_refs[gi][0, 1]
        parts.append((m, l, acc))
    mm = jnp.maximum(jnp.maximum(parts[0][0], parts[1][0]), parts[2][0])
    num = jnp.zeros((DWA_HEADS, DWA_DH), F32)
    den = jnp.zeros((DWA_HEADS, 1), F32)
    for m, l, acc in parts:
        e = jnp.exp(m - mm)
        num = num + e * acc
        den = den + e * l
    o_ref[0] = num / den


def _dwa_sample(q, kv_new, caches):
    N = q.shape[0]
    span = DWA_BLOCK
    in_specs = [pl.BlockSpec((1, len(DWA_GROUPS) * DWA_HEADS, DWA_DH), lambda b: (b, 0, 0))]
    in_specs += [pl.BlockSpec((1, 2, DWA_HEADS, DWA_DH), lambda b: (b, 0, 0, 0))] * 3
    views = []
    for (win, dil), cache in zip(DWA_GROUPS, caches):
        L = cache.shape[1]
        assert L == win and L // dil == span
        views.append(cache.reshape(N, span, dil, 2, DWA_HEADS, DWA_DH))
        in_specs.append(pl.BlockSpec((1, span, 1, 2, DWA_HEADS, DWA_DH), lambda b: (b, 0, 0, 0, 0, 0)))
    return pl.pallas_call(
        _dwa_step_kernel,
        out_shape=jax.ShapeDtypeStruct((N, DWA_HEADS, DWA_DH), F32),
        grid=(N,),
        in_specs=in_specs,
        out_specs=pl.BlockSpec((1, DWA_HEADS, DWA_DH), lambda b: (b, 0, 0)),
        compiler_params=_params("parallel"),
        name="dwa_step",
    )(q, *kv_new, *views)


def _mix_out_kernel(oa_ref, ob_ref, ga_ref, gb_ref, h_ref, gt_ref, nsh_ref, nsc_ref, ngain_ref,
                    wa_ref, wb_ref, wo_ref, *rest, n_cast):
    out_ref, nxt_ref = rest[n_cast:n_cast + 2]
    _cast_blocks(rest[:n_cast], rest[n_cast + 2:])
    ta = _dot(oa_ref[0], wa_ref[...])
    tb = _dot(ob_ref[0], wb_ref[...])
    merged = _sigmoid(ga_ref[0].astype(F32)) * ta + _sigmoid(gb_ref[0].astype(F32)) * tb
    y = _dot(merged.astype(BF16), wo_ref[...])
    gate, nsc, nsh = gt_ref[0], nsc_ref[0], nsh_ref[0]
    for rows in _row_chunks(y.shape[0]):
        h = h_ref[0, rows, :] + _rows_of(gate, rows) * y[rows]
        out_ref[0, rows, :] = h
        nxt_ref[0, rows, :] = _modulated(h, ngain_ref[...], _rows_of(nsc, rows), _rows_of(nsh, rows))


MIX_TM = 512


def _mix_out(oa, ob, gates, h, mod3, wa, wb, wo, next_gain, cast=()):
    G, R, _ = h.shape
    tm = min(MIX_TM, R)
    c_in, c_out, c_shapes = _cast_jobs(cast, lambda g, i: g * (R // tm) + i, G * (R // tm))

    def rows(width, col=0):
        return pl.BlockSpec((1, tm, width), lambda g, i: (g, i, col))

    def whole(shape):
        return pl.BlockSpec(shape, lambda g, i: (0, 0), pipeline_mode=pl.Buffered(1))

    res = pl.pallas_call(
        functools.partial(_mix_out_kernel, n_cast=len(cast)),
        out_shape=[jax.ShapeDtypeStruct((G, R, D_MODEL), F32), jax.ShapeDtypeStruct((G, R, D_MODEL), BF16)]
        + c_shapes,
        grid=(G, R // tm),
        in_specs=[rows(GDN_QK), rows(DWA_GW), rows(D_MODEL, 0), rows(D_MODEL, 1), rows(D_MODEL),
                  _mod_spec(mod3, R, tm, 5), _mod_spec(mod3, R, tm, 6), _mod_spec(mod3, R, tm, 7),
                  pl.BlockSpec((1, D_MODEL), lambda g, i: (0, 0)),
                  whole(wa.shape), whole(wb.shape), whole(wo.shape)] + c_in,
        out_specs=[rows(D_MODEL), rows(D_MODEL)] + c_out,
        compiler_params=_params(*(("arbitrary",) * 2 if cast else ("parallel",) * 2)),
        name="mix_out",
    )(oa, ob, gates, gates, h, mod3, mod3, mod3, next_gain.reshape(1, D_MODEL), wa, wb, wo,
      *[a for a, _, _, _ in cast])
    return res[0], res[1], res[2:]


_W_IN_COLS = _C_GB + D_MODEL
_W_IN_ALIGNED = _W_IN_COLS // CAST_BLOCK * CAST_BLOCK


def _mixer_weights(wt_f32, wt_bf, a_log, dt_bias):
    w = jnp.concatenate([wt_bf, wt_f32[_W_IN_ALIGNED:].astype(BF16)], axis=0)
    order = np.array(_HEAD_ORDER)
    ba_rows = np.concatenate([_C_B + order, _C_A + order])
    par = jnp.zeros((8, LANES), F32)
    par = par.at[0, 8:16].set(a_log[0][order]).at[1, 8:16].set(dt_bias[0][order])
    return dict(
        az=w[:_C_B],
        ba=jnp.pad(w[ba_rows], ((0, LANES - 2 * GDN_HEADS), (0, 0))),
        attn=jnp.concatenate(
            [w[c + g * DWA_GW:c + (g + 1) * DWA_GW] for g in range(len(DWA_GROUPS)) for c in (_C_K, _C_V)]
            + [w[_C_Q:_C_K]], axis=0),
        gates=w[_C_GA:],
        par=par,
    )


def _trunk(h1, xm, mod3, pos, W, norms, ffn2, *, sample_state=None):
    G, R, _ = h1.shape
    up2, down2 = ffn2
    tables = _rope_tables(pos)
    ba = _proj(xm, W["ba"], F32, tn=LANES, name="proj_ba")
    gates = _proj(xm, W["gates"], BF16, tn=1024, name="proj_gates")
    kvs, q = _attn_proj(xm, W["attn"], tables)
    if sample_state is None:
        az, conv_rows = _proj(xm, W["az"], BF16, tn=1024, tail=True, name="proj_az")
        o_a, s_new = _gdn_prompt(az, ba, W["conv"], W["par"], W["gn"])
        o_b, (up2,) = _dwa_prompt(q, kvs, cast=((up2, 1, 2 * D_FF, 4 * CAST_BLOCK),))
        down_cast = ((down2, 0, D_FF, 2 * CAST_BLOCK),)
    else:
        hist, s0, caches = sample_state
        conv_rows = _proj(xm, W["az"], F32, tn=1024, name="proj_az").reshape(R, 1, -1)
        o_a, s_new = _gdn_sample(conv_rows, ba.reshape(R, 1, LANES), hist, s0, W["conv"], W["par"], W["gn"])
        o_b = _dwa_sample(q.reshape(R, len(DWA_GROUPS) * DWA_HEADS, DWA_DH),
                          [kv.reshape(R, 2, DWA_HEADS, DWA_DH) for kv in kvs], caches)
        o_a = o_a.reshape(1, R, -1)
        o_b = o_b.reshape(1, R, DWA_GW).astype(BF16)
        down_cast = ()
    h2, xm2, cast_out = _mix_out(o_a, o_b, gates, h1, mod3, W["wa"], W["wb"], W["wo"], norms["ffn2"],
                                 cast=down_cast)
    if down_cast:
        (down2,) = cast_out
    y = _ffn_last(h2, xm2, mod3, up2, down2, norms["final"])
    return y, conv_rows[:, :, :CONV_CH], s_new, kvs, (up2, down2)


def kernel(x_prompt, x_sample, state_conv, state_delta, cache_kv_w128, cache_kv_w512, cache_kv_w2048, c_prompt, c_sample, w_ada, b_ada, norm_ffn1, w_ffn1_up, w_ffn1_down, norm_mix, w_in, conv_w, a_log, dt_bias, gdn_norm, w_proj_a, w_proj_b, w_out, norm_ffn2, w_ffn2_up, w_ffn2_down, norm_final):
    B, T, _ = x_prompt.shape
    N, S, _ = x_sample.shape
    assert S == 1 and T % (DWA_BLOCK * DWA_GROUPS[-1][1]) == 0
    norms = dict(ffn1=norm_ffn1[0], mix=norm_mix[0], ffn2=norm_ffn2[0], final=norm_final)

    mod = _ada(jnp.concatenate([c_prompt, c_sample], axis=0), w_ada[0], b_ada)
    mod_p = mod[:B].reshape(B, 1, N_MOD * D_MODEL)
    mod_s = mod[B:].reshape(1, N, N_MOD * D_MODEL)

    up1, down1 = w_ffn1_up[0].astype(BF16), w_ffn1_down[0].astype(BF16)
    w_in_t = jnp.swapaxes(w_in[0], 0, 1)
    h1_p, xm_p, w_in_bf = _ffn_first(x_prompt, mod_p, norms["ffn1"], up1, down1, norms["mix"],
                                     cast=((w_in_t, 0, _W_IN_ALIGNED, CAST_BLOCK),))
    W = _mixer_weights(w_in_t, w_in_bf, a_log, dt_bias)
    W.update(conv=conv_w[0], gn=gdn_norm,
             wa=w_proj_a[0].astype(BF16), wb=w_proj_b[0].astype(BF16), wo=w_out[0].astype(BF16))
    h1_s, xm_s = _ffn_first(x_sample.reshape(1, N, D_MODEL), mod_s, norms["ffn1"], up1, down1, norms["mix"])

    y_p, rows_p, s_p, kv_p, ffn2 = _trunk(h1_p, xm_p, mod_p, jnp.arange(T, dtype=jnp.int32), W, norms,
                                          (w_ffn2_up[0], w_ffn2_down[0]))
    caches = (cache_kv_w128[0], cache_kv_w512[0], cache_kv_w2048[0])
    y_s, rows_s, s_s, kv_s, _ = _trunk(h1_s, xm_s, mod_s, jnp.full((N,), PAST_LEN, dtype=jnp.int32), W, norms,
                                       ffn2, sample_state=(state_conv[0], state_delta[0], caches))

    keep = GDN_CONV - 1
    conv_p = rows_p[:, TAIL_ROWS - keep:][None]
    conv_s = jnp.concatenate([state_conv[0], rows_s], axis=1)[:, -keep:][None]
    kv_out_p = []
    for (win, _), kv in zip(DWA_GROUPS, kv_p):
        k = min(win, T)
        kv_out_p.append(kv[:, T - k:].reshape(1, B, k, 2, DWA_HEADS, DWA_DH))
    kv_out_s = [kv.reshape(1, N, 1, 2, DWA_HEADS, DWA_DH) for kv in kv_s]
    return (y_p, y_s.reshape(N, 1, D_MODEL), conv_p, s_p[None], *kv_out_p,
            conv_s, s_s[None], *kv_out_s)
```

```python
import functools

import jax
import jax.numpy as jnp
import numpy as np
from jax import lax
from jax.experimental import pallas as pl
from jax.experimental.pallas import tpu as pltpu

F32 = jnp.float32
BF16 = jnp.bfloat16

D_MODEL = 2048
D_FF = 5632
N_MOD = 9
NORM_EPS = 1e-6
PAST_LEN = 16384

GDN_HEADS = 8
GDN_DK = 128
GDN_CONV = 4
GDN_CHUNK = 64
GDN_QK = GDN_HEADS * GDN_DK
CONV_CH = 3 * GDN_QK

DWA_GROUPS = ((128, 1), (512, 4), (2048, 16))
DWA_HEADS = 4
DWA_DH = 128
DWA_GW = DWA_HEADS * DWA_DH
DWA_WIDTH = len(DWA_GROUPS) * DWA_GW
DWA_BLOCK = 128
ROPE_THETA = 10000.0

LANES = 128
MXU_COLS = 256
VMEM_LIMIT = 56 * 1024 * 1024

_C_Z = CONV_CH
_C_B = _C_Z + GDN_QK
_C_A = _C_B + GDN_HEADS
_C_Q = _C_A + GDN_HEADS
_C_K = _C_Q + DWA_WIDTH
_C_V = _C_K + DWA_WIDTH
_C_GA = _C_V + DWA_WIDTH
_C_GB = _C_GA + D_MODEL

_HEAD_ORDER = (0, 2, 4, 6, 1, 3, 5, 7)
_N_PAIRS = GDN_HEADS // 2


def _sigmoid(x):
    return 0.5 * jnp.tanh(0.5 * x) + 0.5


def _silu(x):
    return x * _sigmoid(x)


def _dot(a, b):
    return jnp.dot(a, b, preferred_element_type=F32)


def _dot_nt(a, b):
    return lax.dot_general(a, b, (((1,), (1,)), ((), ())), preferred_element_type=F32)


def _params(*sem):
    return pltpu.CompilerParams(dimension_semantics=sem, vmem_limit_bytes=VMEM_LIMIT)


def _ada_kernel(c_ref, w_ref, b_ref, o_ref):
    a = _silu(c_ref[...]).astype(BF16)
    o_ref[...] = _dot(a, w_ref[...].astype(BF16)) + b_ref[...]


def _ada(c, w, b):
    m, n = c.shape[0], w.shape[1]
    tn = 1024
    return pl.pallas_call(
        _ada_kernel,
        out_shape=jax.ShapeDtypeStruct((m, n), F32),
        grid=(n // tn,),
        in_specs=[pl.BlockSpec((m, D_MODEL), lambda j: (0, 0)),
                  pl.BlockSpec((D_MODEL, tn), lambda j: (0, j)),
                  pl.BlockSpec((1, tn), lambda j: (0, j))],
        out_specs=pl.BlockSpec((m, tn), lambda j: (0, j)),
        compiler_params=_params("arbitrary"),
        name="ada_mod",
    )(c, w, b)


def _rms(x):
    return x * lax.rsqrt(jnp.mean(x * x, axis=-1, keepdims=True) + NORM_EPS)


def _modulated(h, gain, scale, shift):
    return ((_rms(h) * gain) * (1.0 + scale) + shift).astype(BF16)


ROW_CHUNK = 32


def _row_chunks(n_rows):
    return [slice(r, min(r + ROW_CHUNK, n_rows)) for r in range(0, n_rows, ROW_CHUNK)]


def _rows_of(v, rows):
    return v if v.shape[0] == 1 else v[rows]


def _swiglu_step(xm, wg_ref, wu_ref, wd_ref, acc_sc):
    g = _dot(xm, wg_ref[...])
    u = _dot(xm, wu_ref[...])
    acc_sc[...] += _dot((_silu(g) * u).astype(BF16), wd_ref[...])


def _ffn_first_kernel(x_ref, sh_ref, sc_ref, gt_ref, gain_ref, wg_ref, wu_ref, wd_ref,
                      nsh_ref, nsc_ref, ngain_ref, *rest, n_cast):
    cast_in = rest[:n_cast]
    out_ref, nxt_ref = rest[n_cast:n_cast + 2]
    cast_out = rest[n_cast + 2:2 * n_cast + 2]
    xm_sc, acc_sc = rest[2 * n_cast + 2:]
    j = pl.program_id(2)

    tm = x_ref.shape[1]

    @pl.when(j == 0)
    def _():
        sc, sh = sc_ref[0], sh_ref[0]
        for rows in _row_chunks(tm):
            xm_sc[rows, :] = _modulated(x_ref[0, rows, :], gain_ref[...], _rows_of(sc, rows), _rows_of(sh, rows))
        acc_sc[...] = jnp.zeros_like(acc_sc)

    _cast_blocks(cast_in, cast_out)
    _swiglu_step(xm_sc[...], wg_ref, wu_ref, wd_ref, acc_sc)

    @pl.when(j == pl.num_programs(2) - 1)
    def _():
        half_gate, nsc, nsh = 0.5 * gt_ref[0], nsc_ref[0], nsh_ref[0]
        for rows in _row_chunks(tm):
            h = x_ref[0, rows, :] + _rows_of(half_gate, rows) * acc_sc[rows, :]
            out_ref[0, rows, :] = h
            nxt_ref[0, rows, :] = _modulated(h, ngain_ref[...], _rows_of(nsc, rows), _rows_of(nsh, rows))


def _ffn_last_kernel(x_ref, xm_ref, gt_ref, wg_ref, wu_ref, wd_ref, ngain_ref, out_ref, acc_sc):
    j = pl.program_id(2)

    @pl.when(j == 0)
    def _():
        acc_sc[...] = jnp.zeros_like(acc_sc)

    _swiglu_step(xm_ref[0], wg_ref, wu_ref, wd_ref, acc_sc)

    @pl.when(j == pl.num_programs(2) - 1)
    def _():
        half_gate = 0.5 * gt_ref[0]
        for rows in _row_chunks(x_ref.shape[1]):
            h = x_ref[0, rows, :] + _rows_of(half_gate, rows) * acc_sc[rows, :]
            out_ref[0, rows, :] = _rms(h) * ngain_ref[...]


FFN_TM = 512
FFN_TF = 512


def _mod_spec(mod3, R, tm, k):
    if mod3.shape[1] == R:
        return pl.BlockSpec((1, tm, D_MODEL), lambda g, i, *_: (g, i, k))
    return pl.BlockSpec((1, 1, D_MODEL), lambda g, i, *_: (g, 0, k))


def _ffn_specs(R):
    tm = min(FFN_TM, R)
    nff = D_FF // FFN_TF
    rows = pl.BlockSpec((1, tm, D_MODEL), lambda g, i, j: (g, i, 0))
    vec = pl.BlockSpec((1, D_MODEL), lambda g, i, j: (0, 0))
    weights = [pl.BlockSpec((D_MODEL, FFN_TF), lambda g, i, j: (0, j)),
               pl.BlockSpec((D_MODEL, FFN_TF), lambda g, i, j: (0, j + nff)),
               pl.BlockSpec((FFN_TF, D_MODEL), lambda g, i, j: (j, 0))]
    return tm, nff, rows, vec, weights


CAST_BLOCK = 128


def _cast_jobs(arrays, step_of, total_steps):
    in_specs, out_specs, out_shapes = [], [], []
    start = 0
    for arr, axis, extent, slab in arrays:
        assert extent % slab == 0 and slab % CAST_BLOCK == 0
        nblk = extent // slab
        shape = tuple(slab if d == axis else n for d, n in enumerate(arr.shape))

        def index(*grid, start=start, nblk=nblk, axis=axis):
            blk = jnp.clip(step_of(*grid) - start, 0, nblk - 1)
            return (blk, 0) if axis == 0 else (0, blk)

        in_specs.append(pl.BlockSpec(shape, index))
        out_specs.append(pl.BlockSpec(shape, index))
        out_shapes.append(jax.ShapeDtypeStruct(
            tuple(extent if d == axis else n for d, n in enumerate(arr.shape)), BF16))
        start += nblk
    assert start <= total_steps, (start, total_steps)
    return in_specs, out_specs, out_shapes


def _cast_blocks(cast_in, cast_out):
    for src, dst in zip(cast_in, cast_out):
        dst[...] = src[...].astype(BF16)


def _ffn_first(x3, mod3, gain, w_up, w_down, next_gain, cast=()):
    G, R, _ = x3.shape
    tm, nff, rows, vec, weights = _ffn_specs(R)
    ms = [_mod_spec(mod3, R, tm, k) for k in range(5)]
    c_in, c_out, c_shapes = _cast_jobs(cast, lambda g, i, j: (g * (R // tm) + i) * nff + j, G * (R // tm) * nff)
    sem = ("arbitrary",) * 3 if cast else ("parallel", "parallel", "arbitrary")
    return pl.pallas_call(
        functools.partial(_ffn_first_kernel, n_cast=len(cast)),
        out_shape=[jax.ShapeDtypeStruct((G, R, D_MODEL), F32), jax.ShapeDtypeStruct((G, R, D_MODEL), BF16)]
        + c_shapes,
        grid=(G, R // tm, nff),
        in_specs=[rows, ms[0], ms[1], ms[2], vec] + weights + [ms[3], ms[4], vec] + c_in,
        out_specs=[rows, rows] + c_out,
        scratch_shapes=[pltpu.VMEM((tm, D_MODEL), BF16), pltpu.VMEM((tm, D_MODEL), F32)],
        compiler_params=_params(*sem),
        name="ffn_first",
    )(x3, mod3, mod3, mod3, gain.reshape(1, D_MODEL), w_up, w_up, w_down, mod3, mod3,
      next_gain.reshape(1, D_MODEL), *[a for a, *_ in cast])


def _ffn_last(x3, xm3, mod3, w_up, w_down, final_gain):
    G, R, _ = x3.shape
    tm, nff, rows, vec, weights = _ffn_specs(R)
    return pl.pallas_call(
        _ffn_last_kernel,
        out_shape=jax.ShapeDtypeStruct((G, R, D_MODEL), F32),
        grid=(G, R // tm, nff),
        in_specs=[rows, rows, _mod_spec(mod3, R, tm, 8)] + weights + [vec],
        out_specs=rows,
        scratch_shapes=[pltpu.VMEM((tm, D_MODEL), F32)],
        compiler_params=_params("parallel", "parallel", "arbitrary"),
        name="ffn_last",
    )(x3, xm3, mod3, w_up, w_up, w_down, final_gain.reshape(1, D_MODEL))


def _rope_tile(x, cosf, sinf):
    return x * cosf + pltpu.roll(x, DWA_DH // 2, axis=1) * sinf


TAIL_ROWS = 8


def _proj_kernel(a_ref, wt_ref, *rest, tail):
    acc = _dot_nt(a_ref[0], wt_ref[...])
    rest[0][0] = acc.astype(rest[0].dtype)
    if tail:
        rest[1][0] = acc[acc.shape[0] - TAIL_ROWS:]


PROJ_TM = 1024


def _proj(xm3, wt, out_dtype, *, n=None, tn=512, tail=False, name="proj"):
    G, R, _ = xm3.shape
    n = wt.shape[0] if n is None else n
    tm = min(PROJ_TM, R)
    out_shape = jax.ShapeDtypeStruct((G, R, n), out_dtype)
    out_specs = pl.BlockSpec((1, tm, tn), lambda g, i, j: (g, i, j))
    if tail:
        out_shape = [out_shape, jax.ShapeDtypeStruct((G, R // tm * TAIL_ROWS, n), F32)]
        out_specs = [out_specs, pl.BlockSpec((1, TAIL_ROWS, tn), lambda g, i, j: (g, i, j))]
    res = pl.pallas_call(
        functools.partial(_proj_kernel, tail=tail),
        out_shape=out_shape,
        grid=(G, R // tm, n // tn),
        in_specs=[pl.BlockSpec((1, tm, D_MODEL), lambda g, i, j: (g, i, 0)),
                  pl.BlockSpec((tn, D_MODEL), lambda g, i, j: (j, 0))],
        out_specs=out_specs,
        compiler_params=_params("parallel", "parallel", "arbitrary"),
        name=name,
    )(xm3, wt)
    if tail:
        return res[0], res[1][:, -TAIL_ROWS:]
    return res


ATTN_TILES = tuple((2 * g, 2 * g + 2) for g in range(len(DWA_GROUPS))) + ((2 * len(DWA_GROUPS), 3 * len(DWA_GROUPS)),)


def _attn_proj_kernel(a_ref, wt_ref, cos_ref, sin_ref, *o_refs):
    j = pl.program_id(2)

    def tile(o_ref, rotate):
        a = a_ref[0]
        for c in range(0, DWA_GW, MXU_COLS):
            acc = _dot_nt(a, wt_ref[c:c + MXU_COLS, :])
            for h in range(0, MXU_COLS, DWA_DH):
                x = acc[:, h:h + DWA_DH]
                o_ref[0, :, c + h:c + h + DWA_DH] = _rope_tile(x, cos_ref[...], sin_ref[...]) if rotate else x

    for o_ref, (lo, hi) in zip(o_refs[:-1], ATTN_TILES[:-1]):
        pl.when(j == lo)(functools.partial(tile, o_ref, True))
        pl.when(j == lo + 1)(functools.partial(tile, o_ref, False))
    pl.when(j >= ATTN_TILES[-1][0])(functools.partial(tile, o_refs[-1], True))


def _attn_proj(xm3, wt, tables):
    G, R, _ = xm3.shape
    tm = min(PROJ_TM, R)
    tab = pl.BlockSpec((tm, DWA_DH), lambda g, i, j: (i, 0))
    out_shape, out_specs = [], []
    for lo, hi in ATTN_TILES:
        out_shape.append(jax.ShapeDtypeStruct((G, R, (hi - lo) * DWA_GW), F32))
        out_specs.append(pl.BlockSpec((1, tm, DWA_GW),
                                      lambda g, i, j, lo=lo, hi=hi: (g, i, jnp.clip(j - lo, 0, hi - lo - 1))))
    res = pl.pallas_call(
        _attn_proj_kernel,
        out_shape=out_shape,
        grid=(G, R // tm, ATTN_TILES[-1][1]),
        in_specs=[pl.BlockSpec((1, tm, D_MODEL), lambda g, i, j: (g, i, 0)),
                  pl.BlockSpec((DWA_GW, D_MODEL), lambda g, i, j: (j, 0)), tab, tab],
        out_specs=out_specs,
        compiler_params=_params("parallel", "arbitrary", "arbitrary"),
        name="proj_attn",
    )(xm3, wt, *tables)
    return res[:-1], res[-1]


def _rope_tables(pos):
    half = DWA_DH // 2
    inv = jnp.power(ROPE_THETA, -jnp.arange(half, dtype=F32) / half)
    ang = pos.astype(F32)[:, None] * inv[None, :]
    cos, sin = jnp.cos(ang), jnp.sin(ang)
    return jnp.concatenate([cos, cos], axis=-1), jnp.concatenate([-sin, sin], axis=-1)


def _gdn_gates(ba, par):
    beta = _sigmoid(ba)
    x = ba + par[1:2]
    softplus = jnp.maximum(x, 0.0) + jnp.log1p(jnp.exp(-jnp.abs(x)))
    return beta, -jnp.exp(par[0:1]) * softplus


def _gated_out(o, z, gn):
    return (_rms(o) * gn) * _silu(z.astype(F32))


def _l2norm(t):
    return t * lax.rsqrt(jnp.sum(t * t, axis=-1, keepdims=True) + NORM_EPS)


GDN_HIST = 16


def _gdn_frontend(hist, raw, cw_ref, buf_sc, x_out):
    C = GDN_CHUNK
    buf_sc[0:GDN_HIST, :] = hist
    buf_sc[GDN_HIST:GDN_HIST + C, :] = raw
    taps = GDN_CONV - 1
    ri = lax.broadcasted_iota(jnp.int32, (taps * C, buf_sc.shape[0]), 0)
    ci = lax.broadcasted_iota(jnp.int32, (taps * C, buf_sc.shape[0]), 1)
    shift = jnp.where(ci == GDN_HIST + ri % C - (ri // C + 1), 1.0, 0.0).astype(BF16)
    back = _dot(shift, buf_sc[...])
    acc = raw.astype(F32) * cw_ref[taps:taps + 1, :]
    for s in range(taps):
        acc = acc + back[s * C:(s + 1) * C] * cw_ref[taps - 1 - s:taps - s, :]
    x = _silu(acc)
    for h in range(2 * GDN_HEADS):
        sl = slice(h * GDN_DK, (h + 1) * GDN_DK)
        x_out[:, sl] = _l2norm(x[:, sl]) * (GDN_DK ** -0.5 if h < GDN_HEADS else 1.0)
    x_out[:, 2 * GDN_QK:] = x[:, 2 * GDN_QK:]


def _gdn_chunk_kernel(cur_ref, nxt_ref, z_ref, ba_ref, cw_ref, par_ref, gn_ref, o_ref, s_ref,
                      buf_sc, xa_sc, xb_sc, S_sc):
    C = GDN_CHUNK
    c = pl.program_id(1)

    @pl.when(c == 0)
    def _():
        S_sc[...] = jnp.zeros_like(S_sc)
        buf_sc[...] = jnp.zeros_like(buf_sc)
        _gdn_frontend(jnp.zeros((GDN_HIST, CONV_CH), BF16), cur_ref[0], cw_ref, buf_sc, xa_sc)

    def step(x_sc, x_next):
        _gdn_frontend(cur_ref[0, C - GDN_HIST:C, :], nxt_ref[0], cw_ref, buf_sc, x_next)
        _gdn_chain(x_sc, z_ref, ba_ref, par_ref, gn_ref, o_ref, S_sc)

    pl.when(c % 2 == 0)(lambda: step(xa_sc, xb_sc))
    pl.when(c % 2 == 1)(lambda: step(xb_sc, xa_sc))

    @pl.when(c == pl.num_programs(1) - 1)
    def _():
        s_ref[0] = S_sc[...]


def _gdn_chain(x_sc, z_ref, ba_ref, par_ref, gn_ref, o_ref, S_sc):
    C = GDN_CHUNK
    beta_t, g_t = _gdn_gates(ba_ref[0], par_ref[...])
    row = lax.broadcasted_iota(jnp.int32, (C, LANES), 0)
    G = g_t
    s = 1
    while s < C:
        G = G + jnp.where(row >= s, pltpu.roll(G, s, axis=0), 0.0)
        s *= 2
    shift = LANES - _N_PAIRS
    Gs = jnp.concatenate([G, pltpu.roll(G, shift, axis=1)], axis=0)
    Bs = jnp.concatenate([beta_t, pltpu.roll(beta_t, shift, axis=1)], axis=0)
    GT = Gs.T
    r2 = lax.broadcasted_iota(jnp.int32, (2 * C, LANES), 0)
    Glast = jnp.where(r2 < C, Gs[C - 1:C, :], Gs[2 * C - 1:2 * C, :])
    eG = jnp.exp(Gs)
    eGl = jnp.exp(Glast - Gs)
    egl = jnp.exp(Glast)

    ii = lax.broadcasted_iota(jnp.int32, (2 * C, 2 * C), 0)
    jj = lax.broadcasted_iota(jnp.int32, (2 * C, 2 * C), 1)
    same = (ii // C) == (jj // C)
    strict = same & (ii > jj)
    diag = ii == jj
    blk = (ii // 16) == (jj // 16)
    top = r2 < C
    gn = gn_ref[...]

    P = range(_N_PAIRS)
    C2 = 2 * C

    def pair(p, off):
        a = x_sc[:, off + (2 * p) * GDN_DK: off + (2 * p + 1) * GDN_DK]
        b = x_sc[:, off + (2 * p + 1) * GDN_DK: off + (2 * p + 2) * GDN_DK]
        return jnp.concatenate([a, b], axis=0)

    def col(t, p, base=0):
        return t[:, base + p:base + p + 1]

    q2 = [pair(p, 0) for p in P]
    k2 = [pair(p, GDN_QK) for p in P]
    kb = [k2[p] * col(Bs, p) for p in P]
    vb = [pair(p, 2 * GDN_QK) * col(Bs, p) for p in P]
    dec = [jnp.where(strict, jnp.exp(jnp.where(strict, col(Gs, p, 8) - GT[8 + p:9 + p, :], 0.0)), 0.0) for p in P]
    kq = [_dot_nt(jnp.concatenate([kb[p], q2[p]], axis=0).astype(BF16), k2[p].astype(BF16)) for p in P]
    A = [kq[p][:C2] * dec[p] for p in P]
    qk = [(kq[p][C2:] * (dec[p] + jnp.where(diag, 1.0, 0.0))).astype(BF16) for p in P]

    Dg = [jnp.where(blk, A[p], 0.0) for p in P]
    E = [A[p] - Dg[p] for p in P]
    Q = [-Dg[p] for p in P]
    Dgb = [Dg[p].astype(BF16) for p in P]
    Dp = [_dot(Dgb[p], Dgb[p]) for p in P]
    for _ in range(2):
        Dpb = [Dp[p].astype(BF16) for p in P]
        st = [_dot(jnp.concatenate([Dpb[p], Q[p].astype(BF16)], axis=0), Dpb[p]) for p in P]
        Q = [Q[p] + Dp[p] + st[p][C2:] for p in P]
        Dp = [st[p][:C2] for p in P]
    Q = [Q[p] + Dp[p] + _dot(Q[p].astype(BF16), Dp[p].astype(BF16)) for p in P]
    rhs = [jnp.concatenate([kb[p] * col(eG, p, 8), vb[p]], axis=1) for p in P]
    er = [_dot(Q[p].astype(BF16), jnp.concatenate([E[p], rhs[p]], axis=1).astype(BF16)) for p in P]
    N = [E[p] + er[p][:, :C2] for p in P]
    y = [rhs[p] + er[p][:, C2:] for p in P]
    Nb = [N[p].astype(BF16) for p in P]
    ny = [_dot(Nb[p], jnp.concatenate([N[p], y[p]], axis=1).astype(BF16)) for p in P]
    zz = [y[p] - ny[p][:, C2:] for p in P]
    sol = [zz[p] + _dot(ny[p][:, :C2].astype(BF16), zz[p].astype(BF16)) for p in P]

    wq = [jnp.concatenate([sol[p][:, :GDN_DK], q2[p] * col(eG, p, 8)], axis=0).astype(BF16) for p in P]
    S_old = [S_sc[h] for h in range(GDN_HEADS)]
    rS = [_dot(wq[h // 2], S_old[h].astype(BF16)) for h in range(GDN_HEADS)]
    vn = [(sol[p][:, GDN_DK:] - jnp.where(top, rS[2 * p][:C2], rS[2 * p + 1][:C2])).astype(BF16) for p in P]
    kdT = [(k2[p] * col(eGl, p, 8)).T.astype(BF16) for p in P]
    zero = jnp.zeros((C2, GDN_DK), BF16)
    for p in P:
        o2 = jnp.where(top, rS[2 * p][C2:], rS[2 * p + 1][C2:]) + _dot(qk[p], vn[p])
        S_sc[2 * p] = S_old[2 * p] * egl[0:1, 8 + p:9 + p] + _dot(kdT[p], jnp.where(top, vn[p], zero))
        S_sc[2 * p + 1] = S_old[2 * p + 1] * egl[C:C + 1, 8 + p:9 + p] + _dot(kdT[p], jnp.where(top, zero, vn[p]))
        for e in range(2):
            h = 2 * p + e
            sl = slice(h * GDN_DK, (h + 1) * GDN_DK)
            o_ref[0, :, sl] = _gated_out(o2[e * C:(e + 1) * C], z_ref[0, :, sl], gn).astype(o_ref.dtype)


def _gdn_prompt(az, ba, conv_w, par, gn):
    B, T, _ = az.shape
    C = GDN_CHUNK
    return pl.pallas_call(
        _gdn_chunk_kernel,
        out_shape=[jax.ShapeDtypeStruct((B, T, GDN_QK), BF16),
                   jax.ShapeDtypeStruct((B, GDN_HEADS, GDN_DK, GDN_DK), F32)],
        grid=(B, T // C),
        in_specs=[pl.BlockSpec((1, C, CONV_CH), lambda b, c: (b, c, 0)),
                  pl.BlockSpec((1, C, CONV_CH), lambda b, c: (b, jnp.minimum(c + 1, T // C - 1), 0)),
                  pl.BlockSpec((1, C, GDN_QK), lambda b, c: (b, c, CONV_CH // GDN_QK)),
                  pl.BlockSpec((1, C, LANES), lambda b, c: (b, c, 0)),
                  pl.BlockSpec((GDN_CONV, CONV_CH), lambda b, c: (0, 0)),
                  pl.BlockSpec((8, LANES), lambda b, c: (0, 0)),
                  pl.BlockSpec((1, GDN_DK), lambda b, c: (0, 0))],
        out_specs=[pl.BlockSpec((1, C, GDN_QK), lambda b, c: (b, c, 0)),
                   pl.BlockSpec((1, GDN_HEADS, GDN_DK, GDN_DK), lambda b, c: (b, 0, 0, 0))],
        scratch_shapes=[pltpu.VMEM((2 * C, CONV_CH), BF16), pltpu.VMEM((C, CONV_CH), F32),
                        pltpu.VMEM((C, CONV_CH), F32), pltpu.VMEM((GDN_HEADS, GDN_DK, GDN_DK), F32)],
        compiler_params=_params("parallel", "arbitrary"),
        name="gdn_chunked",
    )(az, az, az, ba, conv_w, par, gn)


def _gdn_step_kernel(qkv_ref, hist_ref, z_ref, ba_ref, s0_ref, cw_ref, par_ref, gn_ref, o_ref, s_ref):
    acc = qkv_ref[0] * cw_ref[GDN_CONV - 1:GDN_CONV, :]
    for j in range(GDN_CONV - 1):
        acc = acc + hist_ref[0, j:j + 1, :] * cw_ref[j:j + 1, :]
    x = _silu(acc)
    beta_t, g_t = _gdn_gates(ba_ref[0], par_ref[...])
    eg_t = jnp.exp(g_t)

    def head(off, h):
        return x[:, off + h * GDN_DK: off + (h + 1) * GDN_DK]

    rows = []
    for off, scale in ((GDN_QK, 1.0), (0, GDN_DK ** -0.5)):
        for h in range(GDN_HEADS):
            t = head(off, h)
            rows.append(t * lax.rsqrt(jnp.sum(t * t, axis=-1, keepdims=True) + NORM_EPS) * scale)
    ri = lax.broadcasted_iota(jnp.int32, (LANES, LANES), 0)
    tile = jnp.zeros((LANES, LANES), F32)
    for r, t in enumerate(rows):
        tile = jnp.where(ri == r, t, tile)
    cols = tile.T
    gn = gn_ref[...]
    for h in range(GDN_HEADS):
        lane = _HEAD_ORDER.index(h)
        kcol = cols[:, h:h + 1]
        qcol = cols[:, GDN_HEADS + h:GDN_HEADS + h + 1]
        S = s0_ref[0, h] * eg_t[:, 8 + lane:9 + lane]
        kv = jnp.sum(kcol * S, axis=0, keepdims=True)
        delta = beta_t[:, lane:lane + 1] * (head(2 * GDN_QK, h) - kv)
        S = S + kcol * delta
        s_ref[0, h] = S
        o = jnp.sum(qcol * S, axis=0, keepdims=True)
        sl = slice(h * GDN_DK, (h + 1) * GDN_DK)
        o_ref[0, :, sl] = _gated_out(o, z_ref[0, :, sl], gn).astype(o_ref.dtype)


def _gdn_sample(az, ba, hist, s0, conv_w, par, gn):
    N = az.shape[0]
    return pl.pallas_call(
        _gdn_step_kernel,
        out_shape=[jax.ShapeDtypeStruct((N, 1, GDN_QK), BF16),
                   jax.ShapeDtypeStruct((N, GDN_HEADS, GDN_DK, GDN_DK), F32)],
        grid=(N,),
        in_specs=[pl.BlockSpec((1, 1, CONV_CH), lambda b: (b, 0, 0)),
                  pl.BlockSpec((1, GDN_CONV - 1, CONV_CH), lambda b: (b, 0, 0)),
                  pl.BlockSpec((1, 1, GDN_QK), lambda b: (b, 0, CONV_CH // GDN_QK)),
                  pl.BlockSpec((1, 1, LANES), lambda b: (b, 0, 0)),
                  pl.BlockSpec((1, GDN_HEADS, GDN_DK, GDN_DK), lambda b: (b, 0, 0, 0)),
                  pl.BlockSpec((GDN_CONV, CONV_CH), lambda b: (0, 0)),
                  pl.BlockSpec((8, LANES), lambda b: (0, 0)),
                  pl.BlockSpec((1, GDN_DK), lambda b: (0, 0))],
        out_specs=[pl.BlockSpec((1, 1, GDN_QK), lambda b: (b, 0, 0)),
                   pl.BlockSpec((1, GDN_HEADS, GDN_DK, GDN_DK), lambda b: (b, 0, 0, 0))],
        compiler_params=_params("parallel"),
        name="gdn_step",
    )(az, hist, az, ba, s0, conv_w, par, gn)


DWA_BATCH = 4


def _dwa_blocks(refs, blocks, masks, first, scratch):
    q_ref, k_ref, v_ref = refs
    m_sc, l_sc, acc_sc = scratch
    cur_ok, prev_ok = masks
    scale = DWA_DH ** -0.5
    BL = DWA_BLOCK
    qb = [q_ref[0, cur, :].astype(BF16) for cur, _, _ in blocks]
    sc = [jnp.where(cur_ok, _dot_nt(qb[i], k_ref[0, cur, :].astype(BF16)) * scale, -jnp.inf)
          for i, (cur, _, _) in enumerate(blocks)]
    sp = [None if prev is None else
          jnp.where(prev_ok & on, _dot_nt(qb[i], k_ref[0, prev, :].astype(BF16)) * scale, -jnp.inf)
          for i, (_, prev, on) in enumerate(blocks)]
    m, l, pc, pp = [], [], [], []
    for i in range(len(blocks)):
        mi = jnp.max(sc[i], axis=-1, keepdims=True)
        if sp[i] is not None:
            mi = jnp.maximum(mi, jnp.max(sp[i], axis=-1, keepdims=True))
        p = jnp.exp(sc[i] - mi)
        li = jnp.sum(p, axis=-1, keepdims=True)
        pc.append(p.astype(BF16))
        if sp[i] is not None:
            p = jnp.exp(sp[i] - mi)
            li = li + jnp.sum(p, axis=-1, keepdims=True)
            pp.append(p.astype(BF16))
        else:
            pp.append(None)
        m.append(mi)
        l.append(li)
    acc = [_dot(pc[i], v_ref[0, cur, :].astype(BF16)) for i, (cur, _, _) in enumerate(blocks)]
    acc = [a if pp[i] is None else a + _dot(pp[i], v_ref[0, blocks[i][1], :].astype(BF16))
           for i, a in enumerate(acc)]
    for i, (cur, _, _) in enumerate(blocks):
        if first:
            m_sc[cur, :] = jnp.broadcast_to(m[i], (BL, LANES))
            l_sc[cur, :] = jnp.broadcast_to(l[i], (BL, LANES))
            acc_sc[cur, :] = acc[i]
        else:
            m_old = m_sc[cur, :]
            m_new = jnp.maximum(m_old, m[i])
            a = jnp.exp(m_old - m_new)
            b = jnp.exp(m[i] - m_new)
            m_sc[cur, :] = m_new
            l_sc[cur, :] = a * l_sc[cur, :] + b * l[i]
            acc_sc[cur, :] = a * acc_sc[cur, :] + b * acc[i]


def _dwa_kernel(q0_ref, q1_ref, q2_ref, k0_ref, v0_ref, k1_ref, v1_ref, k2_ref, v2_ref, *rest, n_cast):
    o_ref = rest[n_cast]
    m_sc, l_sc, acc_sc = rest[2 * n_cast + 1:]
    _cast_blocks(rest[:n_cast], rest[n_cast + 1:2 * n_cast + 1])
    BL = DWA_BLOCK
    T = o_ref.shape[1]
    ri = lax.broadcasted_iota(jnp.int32, (BL, BL), 0)
    ci = lax.broadcasted_iota(jnp.int32, (BL, BL), 1)
    masks = (ri >= ci, ci >= ri)
    q_refs, k_refs, v_refs = (q0_ref, q1_ref, q2_ref), (k0_ref, k1_ref, k2_ref), (v0_ref, v1_ref, v2_ref)
    scratch = (m_sc, l_sc, acc_sc)

    for gi, (_, dil) in enumerate(DWA_GROUPS):
        refs = (q_refs[gi], k_refs[gi], v_refs[gi])
        nb = T // dil // BL

        def rows(blk, r, dil=dil):
            start = blk * (BL * dil) + r
            if dil == 1:
                return pl.ds(pl.multiple_of(start, BL), BL)
            return pl.ds(start, BL, stride=dil)

        def block(n, r, nb=nb, rows=rows):
            if nb == 1:
                return rows(n, r), None, None
            return rows(n, r), rows(jnp.maximum(n - 1, 0), r), n > 0

        if dil >= DWA_BATCH:
            for r0 in range(0, dil, DWA_BATCH):
                def body(n, carry, r0=r0, refs=refs, block=block, first=gi == 0):
                    _dwa_blocks(refs, [block(n, r0 + t) for t in range(DWA_BATCH)], masks, first, scratch)
                    return carry
                if nb == 1:
                    body(0, 0)
                else:
                    lax.fori_loop(0, nb, body, 0)
        else:
            assert dil == 1 and nb % DWA_BATCH == 0

            def body(i, carry, refs=refs, block=block, first=gi == 0):
                _dwa_blocks(refs, [block(i * DWA_BATCH + t, 0) for t in range(DWA_BATCH)], masks, first, scratch)
                return carry
            lax.fori_loop(0, nb // DWA_BATCH, body, 0)

    o_ref[0] = (acc_sc[...] / l_sc[...]).astype(o_ref.dtype)


def _dwa_prompt(q, kvs, cast=()):
    B, T, _ = q.shape
    c_in, c_out, c_shapes = _cast_jobs(cast, lambda b, h: b * DWA_HEADS + h, B * DWA_HEADS)

    def col(c):
        return pl.BlockSpec((1, T, DWA_DH), lambda b, h: (b, 0, c(h)))

    ng = len(DWA_GROUPS)
    in_specs = [col(lambda h, g=g: g * DWA_HEADS + h) for g in range(ng)]
    args = [q] * ng
    for g in range(ng):
        in_specs += [col(lambda h: h), col(lambda h: DWA_HEADS + h)]
        args += [kvs[g], kvs[g]]
    args += [a for a, *_ in cast]
    res = pl.pallas_call(
        functools.partial(_dwa_kernel, n_cast=len(cast)),
        out_shape=[jax.ShapeDtypeStruct((B, T, DWA_GW), BF16)] + c_shapes,
        grid=(B, DWA_HEADS),
        in_specs=in_specs + c_in,
        out_specs=[col(lambda h: h)] + c_out,
        scratch_shapes=[pltpu.VMEM((T, LANES), F32), pltpu.VMEM((T, LANES), F32), pltpu.VMEM((T, DWA_DH), F32)],
        compiler_params=_params(*(("arbitrary",) * 2 if cast else ("parallel",) * 2)),
        name="dwa_prompt",
    )(*args)
    return res[0], res[1:]


def _dwa_step_kernel(q_ref, n0_ref, n1_ref, n2_ref, c0_ref, c1_ref, c2_ref, o_ref):
    new_refs = (n0_ref, n1_ref, n2_ref)
    cache_refs = (c0_ref, c1_ref, c2_ref)
    scale = DWA_DH ** -0.5
    parts = []
    for gi in range(len(DWA_GROUPS)):
        q = q_ref[0, gi * DWA_HEADS:(gi + 1) * DWA_HEADS, :]
        k = cache_refs[gi][0, :, 0, 0]
        v = cache_refs[gi][0, :, 0, 1]
        s = jnp.sum(k * q[None], axis=-1, keepdims=True) * scale
        s_new = jnp.sum(new_refs[gi][0, 0] * q, axis=-1, keepdims=True) * scale
        m = jnp.maximum(jnp.max(s, axis=0), s_new)
        p = jnp.exp(s - m[None])
        p_new = jnp.exp(s_new - m)
        l = jnp.sum(p, axis=0) + p_new
        acc = jnp.sum(p * v, axis=0) + p_new * new_refs[gi][0, 1]
        parts.append((m, l, acc))
    mm = jnp.maximum(jnp.maximum(parts[0][0], parts[1][0]), parts[2][0])
    num = jnp.zeros((DWA_HEADS, DWA_DH), F32)
    den = jnp.zeros((DWA_HEADS, 1), F32)
    for m, l, acc in parts:
        e = jnp.exp(m - mm)
        num = num + e * acc
        den = den + e * l
    o_ref[0] = num / den


def _dwa_sample(q, kv_new, caches):
    N = q.shape[0]
    span = DWA_BLOCK
    in_specs = [pl.BlockSpec((1, len(DWA_GROUPS) * DWA_HEADS, DWA_DH), lambda b: (b, 0, 0))]
    in_specs += [pl.BlockSpec((1, 2, DWA_HEADS, DWA_DH), lambda b: (b, 0, 0, 0))] * 3
    views = []
    for (win, dil), cache in zip(DWA_GROUPS, caches):
        L = cache.shape[1]
        assert L == win and L // dil == span
        views.append(cache.reshape(N, span, dil, 2, DWA_HEADS, DWA_DH))
        in_specs.append(pl.BlockSpec((1, span, 1, 2, DWA_HEADS, DWA_DH), lambda b: (b, 0, 0, 0, 0, 0)))
    return pl.pallas_call(
        _dwa_step_kernel,
        out_shape=jax.ShapeDtypeStruct((N, DWA_HEADS, DWA_DH), F32),
        grid=(N,),
        in_specs=in_specs,
        out_specs=pl.BlockSpec((1, DWA_HEADS, DWA_DH), lambda b: (b, 0, 0)),
        compiler_params=_params("parallel"),
        name="dwa_step",
    )(q, *kv_new, *views)


def _mix_out_kernel(oa_ref, ob_ref, ga_ref, gb_ref, h_ref, gt_ref, nsh_ref, nsc_ref, ngain_ref,
                    wa_ref, wb_ref, wo_ref, *rest, n_cast):
    out_ref, nxt_ref = rest[n_cast:n_cast + 2]
    _cast_blocks(rest[:n_cast], rest[n_cast + 2:])
    ta = _dot(oa_ref[0], wa_ref[...])
    tb = _dot(ob_ref[0], wb_ref[...])
    merged = _sigmoid(ga_ref[0].astype(F32)) * ta + _sigmoid(gb_ref[0].astype(F32)) * tb
    y = _dot(merged.astype(BF16), wo_ref[...])
    gate, nsc, nsh = gt_ref[0], nsc_ref[0], nsh_ref[0]
    for rows in _row_chunks(y.shape[0]):
        h = h_ref[0, rows, :] + _rows_of(gate, rows) * y[rows]
        out_ref[0, rows, :] = h
        nxt_ref[0, rows, :] = _modulated(h, ngain_ref[...], _rows_of(nsc, rows), _rows_of(nsh, rows))


MIX_TM = 512


def _mix_out(oa, ob, gates, h, mod3, wa, wb, wo, next_gain, cast=()):
    G, R, _ = h.shape
    tm = min(MIX_TM, R)
    c_in, c_out, c_shapes = _cast_jobs(cast, lambda g, i: g * (R // tm) + i, G * (R // tm))

    def rows(width, col=0):
        return pl.BlockSpec((1, tm, width), lambda g, i: (g, i, col))

    def whole(shape):
        return pl.BlockSpec(shape, lambda g, i: (0, 0), pipeline_mode=pl.Buffered(1))

    res = pl.pallas_call(
        functools.partial(_mix_out_kernel, n_cast=len(cast)),
        out_shape=[jax.ShapeDtypeStruct((G, R, D_MODEL), F32), jax.ShapeDtypeStruct((G, R, D_MODEL), BF16)]
        + c_shapes,
        grid=(G, R // tm),
        in_specs=[rows(GDN_QK), rows(DWA_GW), rows(D_MODEL, 0), rows(D_MODEL, 1), rows(D_MODEL),
                  _mod_spec(mod3, R, tm, 5), _mod_spec(mod3, R, tm, 6), _mod_spec(mod3, R, tm, 7),
                  pl.BlockSpec((1, D_MODEL), lambda g, i: (0, 0)),
                  whole(wa.shape), whole(wb.shape), whole(wo.shape)] + c_in,
        out_specs=[rows(D_MODEL), rows(D_MODEL)] + c_out,
        compiler_params=_params(*(("arbitrary",) * 2 if cast else ("parallel",) * 2)),
        name="mix_out",
    )(oa, ob, gates, gates, h, mod3, mod3, mod3, next_gain.reshape(1, D_MODEL), wa, wb, wo,
      *[a for a, *_ in cast])
    return res[0], res[1], res[2:]


_W_IN_COLS = _C_GB + D_MODEL
_W_IN_ALIGNED = _W_IN_COLS // CAST_BLOCK * CAST_BLOCK


def _mixer_weights(wt_f32, wt_bf, a_log, dt_bias):
    w = wt_bf
    order = np.array(_HEAD_ORDER)
    ba_rows = np.concatenate([_C_B + order, _C_A + order])
    par = jnp.zeros((8, LANES), F32)
    par = par.at[0, 8:16].set(a_log[0][order]).at[1, 8:16].set(dt_bias[0][order])
    return dict(
        az=w,
        ba=jnp.pad(w[ba_rows], ((0, LANES - 2 * GDN_HEADS), (0, 0))),
        attn=jnp.concatenate(
            [w[c + g * DWA_GW:c + (g + 1) * DWA_GW] for g in range(len(DWA_GROUPS)) for c in (_C_K, _C_V)]
            + [w[_C_Q:_C_K]], axis=0),
        gates=jnp.concatenate([w[_C_GA:], wt_f32[_W_IN_ALIGNED:].astype(BF16)], axis=0),
        par=par,
    )


def _trunk(h1, xm, mod3, pos, W, norms, ffn2, *, sample_state=None):
    G, R, _ = h1.shape
    up2, down2 = ffn2
    tables = _rope_tables(pos)
    ba = _proj(xm, W["ba"], F32, tn=LANES, name="proj_ba")
    gates = _proj(xm, W["gates"], BF16, tn=1024, name="proj_gates")
    kvs, q = _attn_proj(xm, W["attn"], tables)
    if sample_state is None:
        az, conv_rows = _proj(xm, W["az"], BF16, n=_C_B, tn=1024, tail=True, name="proj_az")
        o_a, s_new = _gdn_prompt(az, ba, W["conv"], W["par"], W["gn"])
        o_b, (up2,) = _dwa_prompt(q, kvs, cast=((up2, 1, 2 * D_FF, 4 * CAST_BLOCK),))
        down_cast = ((down2, 0, D_FF, 2 * CAST_BLOCK),)
    else:
        hist, s0, caches = sample_state
        conv_rows = _proj(xm, W["az"], F32, n=_C_B, tn=1024, name="proj_az").reshape(R, 1, -1)
        o_a, s_new = _gdn_sample(conv_rows, ba.reshape(R, 1, LANES), hist, s0, W["conv"], W["par"], W["gn"])
        o_b = _dwa_sample(q.reshape(R, len(DWA_GROUPS) * DWA_HEADS, DWA_DH),
                          [kv.reshape(R, 2, DWA_HEADS, DWA_DH) for kv in kvs], caches)
        o_a = o_a.reshape(1, R, -1)
        o_b = o_b.reshape(1, R, DWA_GW).astype(BF16)
        down_cast = ()
    h2, xm2, cast_out = _mix_out(o_a, o_b, gates, h1, mod3, W["wa"], W["wb"], W["wo"], norms["ffn2"],
                                 cast=down_cast)
    if down_cast:
        (down2,) = cast_out
    y = _ffn_last(h2, xm2, mod3, up2, down2, norms["final"])
    return y, conv_rows[:, :, :CONV_CH], s_new, kvs, (up2, down2)


def kernel(x_prompt, x_sample, state_conv, state_delta, cache_kv_w128, cache_kv_w512, cache_kv_w2048, c_prompt, c_sample, w_ada, b_ada, norm_ffn1, w_ffn1_up, w_ffn1_down, norm_mix, w_in, conv_w, a_log, dt_bias, gdn_norm, w_proj_a, w_proj_b, w_out, norm_ffn2, w_ffn2_up, w_ffn2_down, norm_final):
    B, T, _ = x_prompt.shape
    N, S, _ = x_sample.shape
    assert S == 1 and T % (DWA_BLOCK * DWA_GROUPS[-1][1]) == 0
    norms = dict(ffn1=norm_ffn1[0], mix=norm_mix[0], ffn2=norm_ffn2[0], final=norm_final)

    mod = _ada(jnp.concatenate([c_prompt, c_sample], axis=0), w_ada[0], b_ada)
    mod_p = mod[:B].reshape(B, 1, N_MOD * D_MODEL)
    mod_s = mod[B:].reshape(1, N, N_MOD * D_MODEL)

    up1, down1 = w_ffn1_up[0].astype(BF16), w_ffn1_down[0].astype(BF16)
    w_in_t = jnp.swapaxes(w_in[0], 0, 1)
    h1_p, xm_p, w_in_bf = _ffn_first(x_prompt, mod_p, norms["ffn1"], up1, down1, norms["mix"],
                                     cast=((w_in_t, 0, _W_IN_ALIGNED, CAST_BLOCK),))
    W = _mixer_weights(w_in_t, w_in_bf, a_log, dt_bias)
    W.update(conv=conv_w[0], gn=gdn_norm,
             wa=w_proj_a[0].astype(BF16), wb=w_proj_b[0].astype(BF16), wo=w_out[0].astype(BF16))
    h1_s, xm_s = _ffn_first(x_sample.reshape(1, N, D_MODEL), mod_s, norms["ffn1"], up1, down1, norms["mix"])

    y_p, rows_p, s_p, kv_p, ffn2 = _trunk(h1_p, xm_p, mod_p, jnp.arange(T, dtype=jnp.int32), W, norms,
                                          (w_ffn2_up[0], w_ffn2_down[0]))
    caches = (cache_kv_w128[0], cache_kv_w512[0], cache_kv_w2048[0])
    y_s, rows_s, s_s, kv_s, _ = _trunk(h1_s, xm_s, mod_s, jnp.full((N,), PAST_LEN, dtype=jnp.int32), W, norms,
                                       ffn2, sample_state=(state_conv[0], state_delta[0], caches))

    keep = GDN_CONV - 1
    conv_p = rows_p[:, TAIL_ROWS - keep:][None]
    conv_s = jnp.concatenate([state_conv[0], rows_s], axis=1)[:, -keep:][None]
    kv_out_p = []
    for (win, _), kv in zip(DWA_GROUPS, kv_p):
        k = min(win, T)
        kv_out_p.append(kv[:, T - k:].reshape(1, B, k, 2, DWA_HEADS, DWA_DH))
    kv_out_s = [kv.reshape(1, N, 1, 2, DWA_HEADS, DWA_DH) for kv in kv_s]
    return (y_p, y_s.reshape(N, 1, D_MODEL), conv_p, s_p[None], *kv_out_p,
            conv_s, s_s[None], *kv_out_s)
```

```python
import functools

import jax
import jax.numpy as jnp
import numpy as np
from jax import lax
from jax.experimental import pallas as pl
from jax.experimental.pallas import tpu as pltpu

F32 = jnp.float32
BF16 = jnp.bfloat16

D_MODEL = 2048
D_FF = 5632
N_MOD = 9
NORM_EPS = 1e-6
PAST_LEN = 16384

GDN_HEADS = 8
GDN_DK = 128
GDN_CONV = 4
GDN_CHUNK = 64
GDN_QK = GDN_HEADS * GDN_DK
CONV_CH = 3 * GDN_QK

DWA_GROUPS = ((128, 1), (512, 4), (2048, 16))
DWA_HEADS = 4
DWA_DH = 128
DWA_GW = DWA_HEADS * DWA_DH
DWA_WIDTH = len(DWA_GROUPS) * DWA_GW
DWA_BLOCK = 128
ROPE_THETA = 10000.0

LANES = 128
MXU_COLS = 256
VMEM_LIMIT = 56 * 1024 * 1024

_C_Z = CONV_CH
_C_B = _C_Z + GDN_QK
_C_A = _C_B + GDN_HEADS
_C_Q = _C_A + GDN_HEADS
_C_K = _C_Q + DWA_WIDTH
_C_V = _C_K + DWA_WIDTH
_C_GA = _C_V + DWA_WIDTH
_C_GB = _C_GA + D_MODEL

_HEAD_ORDER = (0, 2, 4, 6, 1, 3, 5, 7)
_N_PAIRS = GDN_HEADS // 2


def _sigmoid(x):
    return 0.5 * jnp.tanh(0.5 * x) + 0.5


def _silu(x):
    return x * _sigmoid(x)


def _dot(a, b):
    return jnp.dot(a, b, preferred_element_type=F32)


def _dot_nt(a, b):
    return lax.dot_general(a, b, (((1,), (1,)), ((), ())), preferred_element_type=F32)


def _params(*sem):
    return pltpu.CompilerParams(dimension_semantics=sem, vmem_limit_bytes=VMEM_LIMIT)


def _ada_kernel(c_ref, w_ref, b_ref, o_ref):
    a = _silu(c_ref[...]).astype(BF16)
    o_ref[...] = _dot(a, w_ref[...].astype(BF16)) + b_ref[...]


def _ada(c, w, b):
    m, n = c.shape[0], w.shape[1]
    tn = 1024
    return pl.pallas_call(
        _ada_kernel,
        out_shape=jax.ShapeDtypeStruct((m, n), F32),
        grid=(n // tn,),
        in_specs=[pl.BlockSpec((m, D_MODEL), lambda j: (0, 0)),
                  pl.BlockSpec((D_MODEL, tn), lambda j: (0, j)),
                  pl.BlockSpec((1, tn), lambda j: (0, j))],
        out_specs=pl.BlockSpec((m, tn), lambda j: (0, j)),
        compiler_params=_params("arbitrary"),
        name="ada_mod",
    )(c, w, b)


def _rms(x):
    return x * lax.rsqrt(jnp.mean(x * x, axis=-1, keepdims=True) + NORM_EPS)


def _modulated(h, gain, scale, shift):
    return ((_rms(h) * gain) * (1.0 + scale) + shift).astype(BF16)


ROW_CHUNK = 32


def _row_chunks(n_rows):
    return [slice(r, min(r + ROW_CHUNK, n_rows)) for r in range(0, n_rows, ROW_CHUNK)]


def _rows_of(v, rows):
    return v if v.shape[0] == 1 else v[rows]


def _swiglu_step(xm, wg_ref, wu_ref, wd_ref, acc_sc):
    g = _dot(xm, wg_ref[...])
    u = _dot(xm, wu_ref[...])
    acc_sc[...] += _dot((_silu(g) * u).astype(BF16), wd_ref[...])


def _ffn_first_kernel(x_ref, sh_ref, sc_ref, gt_ref, gain_ref, wg_ref, wu_ref, wd_ref,
                      nsh_ref, nsc_ref, ngain_ref, *rest, n_cast):
    cast_in = rest[:n_cast]
    out_ref, nxt_ref = rest[n_cast:n_cast + 2]
    cast_out = rest[n_cast + 2:2 * n_cast + 2]
    xm_sc, acc_sc = rest[2 * n_cast + 2:]
    j = pl.program_id(2)

    tm = x_ref.shape[1]

    @pl.when(j == 0)
    def _():
        sc, sh = sc_ref[0], sh_ref[0]
        for rows in _row_chunks(tm):
            xm_sc[rows, :] = _modulated(x_ref[0, rows, :], gain_ref[...], _rows_of(sc, rows), _rows_of(sh, rows))
        acc_sc[...] = jnp.zeros_like(acc_sc)

    _cast_blocks(cast_in, cast_out)
    _swiglu_step(xm_sc[...], wg_ref, wu_ref, wd_ref, acc_sc)

    @pl.when(j == pl.num_programs(2) - 1)
    def _():
        half_gate, nsc, nsh = 0.5 * gt_ref[0], nsc_ref[0], nsh_ref[0]
        for rows in _row_chunks(tm):
            h = x_ref[0, rows, :] + _rows_of(half_gate, rows) * acc_sc[rows, :]
            out_ref[0, rows, :] = h
            nxt_ref[0, rows, :] = _modulated(h, ngain_ref[...], _rows_of(nsc, rows), _rows_of(nsh, rows))


def _ffn_last_kernel(x_ref, xm_ref, gt_ref, wg_ref, wu_ref, wd_ref, ngain_ref, out_ref, acc_sc):
    j = pl.program_id(2)

    @pl.when(j == 0)
    def _():
        acc_sc[...] = jnp.zeros_like(acc_sc)

    _swiglu_step(xm_ref[0], wg_ref, wu_ref, wd_ref, acc_sc)

    @pl.when(j == pl.num_programs(2) - 1)
    def _():
        half_gate = 0.5 * gt_ref[0]
        for rows in _row_chunks(x_ref.shape[1]):
            h = x_ref[0, rows, :] + _rows_of(half_gate, rows) * acc_sc[rows, :]
            out_ref[0, rows, :] = _rms(h) * ngain_ref[...]


FFN_TM = 512
FFN_TF = 512


def _mod_spec(mod3, R, tm, k):
    if mod3.shape[1] == R:
        return pl.BlockSpec((1, tm, D_MODEL), lambda g, i, *_: (g, i, k))
    return pl.BlockSpec((1, 1, D_MODEL), lambda g, i, *_: (g, 0, k))


def _ffn_specs(R):
    tm = min(FFN_TM, R)
    nff = D_FF // FFN_TF
    rows = pl.BlockSpec((1, tm, D_MODEL), lambda g, i, j: (g, i, 0))
    vec = pl.BlockSpec((1, D_MODEL), lambda g, i, j: (0, 0))
    weights = [pl.BlockSpec((D_MODEL, FFN_TF), lambda g, i, j: (0, j)),
               pl.BlockSpec((D_MODEL, FFN_TF), lambda g, i, j: (0, j + nff)),
               pl.BlockSpec((FFN_TF, D_MODEL), lambda g, i, j: (j, 0))]
    return tm, nff, rows, vec, weights


CAST_BLOCK = 128


def _cast_jobs(arrays, step_of, total_steps):
    in_specs, out_specs, out_shapes = [], [], []
    start = 0
    for arr, axis, extent, slab in arrays:
        assert extent % slab == 0 and slab % CAST_BLOCK == 0
        nblk = extent // slab
        shape = tuple(slab if d == axis else n for d, n in enumerate(arr.shape))

        def index(*grid, start=start, nblk=nblk, axis=axis):
            blk = jnp.clip(step_of(*grid) - start, 0, nblk - 1)
            return (blk, 0) if axis == 0 else (0, blk)

        in_specs.append(pl.BlockSpec(shape, index))
        out_specs.append(pl.BlockSpec(shape, index))
        out_shapes.append(jax.ShapeDtypeStruct(
            tuple(extent if d == axis else n for d, n in enumerate(arr.shape)), BF16))
        start += nblk
    assert start <= total_steps, (start, total_steps)
    return in_specs, out_specs, out_shapes


def _cast_blocks(cast_in, cast_out):
    for src, dst in zip(cast_in, cast_out):
        dst[...] = src[...].astype(BF16)


def _ffn_first(x3, mod3, gain, w_up, w_down, next_gain, cast=()):
    G, R, _ = x3.shape
    tm, nff, rows, vec, weights = _ffn_specs(R)
    ms = [_mod_spec(mod3, R, tm, k) for k in range(5)]
    c_in, c_out, c_shapes = _cast_jobs(cast, lambda g, i, j: (g * (R // tm) + i) * nff + j, G * (R // tm) * nff)
    sem = ("arbitrary",) * 3 if cast else ("parallel", "parallel", "arbitrary")
    return pl.pallas_call(
        functools.partial(_ffn_first_kernel, n_cast=len(cast)),
        out_shape=[jax.ShapeDtypeStruct((G, R, D_MODEL), F32), jax.ShapeDtypeStruct((G, R, D_MODEL), BF16)]
        + c_shapes,
        grid=(G, R // tm, nff),
        in_specs=[rows, ms[0], ms[1], ms[2], vec] + weights + [ms[3], ms[4], vec] + c_in,
        out_specs=[rows, rows] + c_out,
        scratch_shapes=[pltpu.VMEM((tm, D_MODEL), BF16), pltpu.VMEM((tm, D_MODEL), F32)],
        compiler_params=_params(*sem),
        name="ffn_first",
    )(x3, mod3, mod3, mod3, gain.reshape(1, D_MODEL), w_up, w_up, w_down, mod3, mod3,
      next_gain.reshape(1, D_MODEL), *[a for a, *_ in cast])


def _ffn_last(x3, xm3, mod3, w_up, w_down, final_gain):
    G, R, _ = x3.shape
    tm, nff, rows, vec, weights = _ffn_specs(R)
    return pl.pallas_call(
        _ffn_last_kernel,
        out_shape=jax.ShapeDtypeStruct((G, R, D_MODEL), F32),
        grid=(G, R // tm, nff),
        in_specs=[rows, rows, _mod_spec(mod3, R, tm, 8)] + weights + [vec],
        out_specs=rows,
        scratch_shapes=[pltpu.VMEM((tm, D_MODEL), F32)],
        compiler_params=_params("parallel", "parallel", "arbitrary"),
        name="ffn_last",
    )(x3, xm3, mod3, w_up, w_up, w_down, final_gain.reshape(1, D_MODEL))


def _rope_tile(x, cosf, sinf):
    return x * cosf + pltpu.roll(x, DWA_DH // 2, axis=1) * sinf


TAIL_ROWS = 8


def _proj_kernel(a_ref, wt_ref, *rest, tail):
    acc = _dot_nt(a_ref[0], wt_ref[...])
    rest[0][0] = acc.astype(rest[0].dtype)
    if tail:
        rest[1][0] = acc[acc.shape[0] - TAIL_ROWS:]


PROJ_TM = 1024


def _proj(xm3, wt, out_dtype, *, n=None, tn=512, tail=False, name="proj"):
    G, R, _ = xm3.shape
    n = wt.shape[0] if n is None else n
    tm = min(PROJ_TM, R)
    out_shape = jax.ShapeDtypeStruct((G, R, n), out_dtype)
    out_specs = pl.BlockSpec((1, tm, tn), lambda g, i, j: (g, i, j))
    if tail:
        out_shape = [out_shape, jax.ShapeDtypeStruct((G, R // tm * TAIL_ROWS, n), F32)]
        out_specs = [out_specs, pl.BlockSpec((1, TAIL_ROWS, tn), lambda g, i, j: (g, i, j))]
    res = pl.pallas_call(
        functools.partial(_proj_kernel, tail=tail),
        out_shape=out_shape,
        grid=(G, R // tm, n // tn),
        in_specs=[pl.BlockSpec((1, tm, D_MODEL), lambda g, i, j: (g, i, 0)),
                  pl.BlockSpec((tn, D_MODEL), lambda g, i, j: (j, 0))],
        out_specs=out_specs,
        compiler_params=_params("parallel", "parallel", "arbitrary"),
        name=name,
    )(xm3, wt)
    if tail:
        return res[0], res[1][:, -TAIL_ROWS:]
    return res


ATTN_TILES = tuple((2 * g, 2 * g + 2) for g in range(len(DWA_GROUPS))) + ((2 * len(DWA_GROUPS), 3 * len(DWA_GROUPS)),)


def _attn_proj_kernel(a_ref, wt_ref, cos_ref, sin_ref, *o_refs):
    j = pl.program_id(2)

    def tile(o_ref, rotate):
        a = a_ref[0]
        for c in range(0, DWA_GW, MXU_COLS):
            acc = _dot_nt(a, wt_ref[c:c + MXU_COLS, :])
            for h in range(0, MXU_COLS, DWA_DH):
                x = acc[:, h:h + DWA_DH]
                o_ref[0, :, c + h:c + h + DWA_DH] = _rope_tile(x, cos_ref[...], sin_ref[...]) if rotate else x

    for o_ref, (lo, hi) in zip(o_refs[:-1], ATTN_TILES[:-1]):
        pl.when(j == lo)(functools.partial(tile, o_ref, True))
        pl.when(j == lo + 1)(functools.partial(tile, o_ref, False))
    pl.when(j >= ATTN_TILES[-1][0])(functools.partial(tile, o_refs[-1], True))


def _attn_proj(xm3, wt, tables):
    G, R, _ = xm3.shape
    tm = min(PROJ_TM, R)
    tab = pl.BlockSpec((tm, DWA_DH), lambda g, i, j: (i, 0))
    out_shape, out_specs = [], []
    for lo, hi in ATTN_TILES:
        out_shape.append(jax.ShapeDtypeStruct((G, R, (hi - lo) * DWA_GW), F32))
        out_specs.append(pl.BlockSpec((1, tm, DWA_GW),
                                      lambda g, i, j, lo=lo, hi=hi: (g, i, jnp.clip(j - lo, 0, hi - lo - 1))))
    res = pl.pallas_call(
        _attn_proj_kernel,
        out_shape=out_shape,
        grid=(G, R // tm, ATTN_TILES[-1][1]),
        in_specs=[pl.BlockSpec((1, tm, D_MODEL), lambda g, i, j: (g, i, 0)),
                  pl.BlockSpec((DWA_GW, D_MODEL), lambda g, i, j: (j, 0)), tab, tab],
        out_specs=out_specs,
        compiler_params=_params("parallel", "arbitrary", "arbitrary"),
        name="proj_attn",
    )(xm3, wt, *tables)
    return res[:-1], res[-1]


def _rope_tables(pos):
    half = DWA_DH // 2
    inv = jnp.power(ROPE_THETA, -jnp.arange(half, dtype=F32) / half)
    ang = pos.astype(F32)[:, None] * inv[None, :]
    cos, sin = jnp.cos(ang), jnp.sin(ang)
    return jnp.concatenate([cos, cos], axis=-1), jnp.concatenate([-sin, sin], axis=-1)


def _gdn_gates(ba, par):
    beta = _sigmoid(ba)
    x = ba + par[1:2]
    softplus = jnp.maximum(x, 0.0) + jnp.log1p(jnp.exp(-jnp.abs(x)))
    return beta, -jnp.exp(par[0:1]) * softplus


def _gated_out(o, z, gn):
    return (_rms(o) * gn) * _silu(z.astype(F32))


def _l2norm(t):
    return t * lax.rsqrt(jnp.sum(t * t, axis=-1, keepdims=True) + NORM_EPS)


GDN_HIST = 16


def _gdn_frontend(hist, raw, cw_ref, buf_sc, x_out):
    C = GDN_CHUNK
    buf_sc[0:GDN_HIST, :] = hist
    buf_sc[GDN_HIST:GDN_HIST + C, :] = raw
    taps = GDN_CONV - 1
    ri = lax.broadcasted_iota(jnp.int32, (taps * C, buf_sc.shape[0]), 0)
    ci = lax.broadcasted_iota(jnp.int32, (taps * C, buf_sc.shape[0]), 1)
    shift = jnp.where(ci == GDN_HIST + ri % C - (ri // C + 1), 1.0, 0.0).astype(BF16)
    back = _dot(shift, buf_sc[...])
    acc = raw.astype(F32) * cw_ref[taps:taps + 1, :]
    for s in range(taps):
        acc = acc + back[s * C:(s + 1) * C] * cw_ref[taps - 1 - s:taps - s, :]
    x = _silu(acc)
    for h in range(2 * GDN_HEADS):
        sl = slice(h * GDN_DK, (h + 1) * GDN_DK)
        x_out[:, sl] = _l2norm(x[:, sl]) * (GDN_DK ** -0.5 if h < GDN_HEADS else 1.0)
    x_out[:, 2 * GDN_QK:] = x[:, 2 * GDN_QK:]


def _gdn_chunk_kernel(cur_ref, nxt_ref, z_ref, ba_ref, cw_ref, par_ref, gn_ref, o_ref, s_ref,
                      buf_sc, xa_sc, xb_sc, S_sc):
    C = GDN_CHUNK
    c = pl.program_id(1)

    @pl.when(c == 0)
    def _():
        S_sc[...] = jnp.zeros_like(S_sc)
        buf_sc[...] = jnp.zeros_like(buf_sc)
        _gdn_frontend(jnp.zeros((GDN_HIST, CONV_CH), BF16), cur_ref[0, 0:C, :], cw_ref, buf_sc, xa_sc)

    refs = (z_ref, ba_ref, par_ref, gn_ref, o_ref, S_sc)
    _gdn_frontend(cur_ref[0, C - GDN_HIST:C, :], cur_ref[0, C:2 * C, :], cw_ref, buf_sc, xb_sc)
    _gdn_chain(xa_sc, 0, *refs)
    _gdn_frontend(cur_ref[0, 2 * C - GDN_HIST:2 * C, :], nxt_ref[0], cw_ref, buf_sc, xa_sc)
    _gdn_chain(xb_sc, C, *refs)

    @pl.when(c == pl.num_programs(1) - 1)
    def _():
        s_ref[0] = S_sc[...]


def _gdn_chain(x_sc, r0, z_ref, ba_ref, par_ref, gn_ref, o_ref, S_sc):
    C = GDN_CHUNK
    beta_t, g_t = _gdn_gates(ba_ref[0, r0:r0 + C, :], par_ref[...])
    row = lax.broadcasted_iota(jnp.int32, (C, LANES), 0)
    G = g_t
    s = 1
    while s < C:
        G = G + jnp.where(row >= s, pltpu.roll(G, s, axis=0), 0.0)
        s *= 2
    shift = LANES - _N_PAIRS
    Gs = jnp.concatenate([G, pltpu.roll(G, shift, axis=1)], axis=0)
    Bs = jnp.concatenate([beta_t, pltpu.roll(beta_t, shift, axis=1)], axis=0)
    GT = Gs.T
    r2 = lax.broadcasted_iota(jnp.int32, (2 * C, LANES), 0)
    Glast = jnp.where(r2 < C, Gs[C - 1:C, :], Gs[2 * C - 1:2 * C, :])
    eG = jnp.exp(Gs)
    eGl = jnp.exp(Glast - Gs)
    egl = jnp.exp(Glast)

    ii = lax.broadcasted_iota(jnp.int32, (2 * C, 2 * C), 0)
    jj = lax.broadcasted_iota(jnp.int32, (2 * C, 2 * C), 1)
    same = (ii // C) == (jj // C)
    strict = same & (ii > jj)
    diag = ii == jj
    blk = (ii // 16) == (jj // 16)
    top = r2 < C
    gn = gn_ref[...]

    P = range(_N_PAIRS)
    C2 = 2 * C

    def pair(p, off):
        a = x_sc[:, off + (2 * p) * GDN_DK: off + (2 * p + 1) * GDN_DK]
        b = x_sc[:, off + (2 * p + 1) * GDN_DK: off + (2 * p + 2) * GDN_DK]
        return jnp.concatenate([a, b], axis=0)

    def col(t, p, base=0):
        return t[:, base + p:base + p + 1]

    q2 = [pair(p, 0) for p in P]
    k2 = [pair(p, GDN_QK) for p in P]
    kb = [k2[p] * col(Bs, p) for p in P]
    vb = [pair(p, 2 * GDN_QK) * col(Bs, p) for p in P]
    dec = [jnp.where(strict, jnp.exp(jnp.where(strict, col(Gs, p, 8) - GT[8 + p:9 + p, :], 0.0)), 0.0) for p in P]
    kq = [_dot_nt(jnp.concatenate([kb[p], q2[p]], axis=0).astype(BF16), k2[p].astype(BF16)) for p in P]
    A = [kq[p][:C2] * dec[p] for p in P]
    qk = [(kq[p][C2:] * (dec[p] + jnp.where(diag, 1.0, 0.0))).astype(BF16) for p in P]

    Dg = [jnp.where(blk, A[p], 0.0) for p in P]
    E = [A[p] - Dg[p] for p in P]
    Q = [-Dg[p] for p in P]
    Dgb = [Dg[p].astype(BF16) for p in P]
    Dp = [_dot(Dgb[p], Dgb[p]) for p in P]
    for _ in range(2):
        Dpb = [Dp[p].astype(BF16) for p in P]
        st = [_dot(jnp.concatenate([Dpb[p], Q[p].astype(BF16)], axis=0), Dpb[p]) for p in P]
        Q = [Q[p] + Dp[p] + st[p][C2:] for p in P]
        Dp = [st[p][:C2] for p in P]
    Q = [Q[p] + Dp[p] + _dot(Q[p].astype(BF16), Dp[p].astype(BF16)) for p in P]
    rhs = [jnp.concatenate([kb[p] * col(eG, p, 8), vb[p]], axis=1) for p in P]
    er = [_dot(Q[p].astype(BF16), jnp.concatenate([E[p], rhs[p]], axis=1).astype(BF16)) for p in P]
    N = [E[p] + er[p][:, :C2] for p in P]
    y = [rhs[p] + er[p][:, C2:] for p in P]
    Nb = [N[p].astype(BF16) for p in P]
    ny = [_dot(Nb[p], jnp.concatenate([N[p], y[p]], axis=1).astype(BF16)) for p in P]
    zz = [y[p] - ny[p][:, C2:] for p in P]
    sol = [zz[p] + _dot(ny[p][:, :C2].astype(BF16), zz[p].astype(BF16)) for p in P]

    wq = [jnp.concatenate([sol[p][:, :GDN_DK], q2[p] * col(eG, p, 8)], axis=0).astype(BF16) for p in P]
    S_old = [S_sc[h] for h in range(GDN_HEADS)]
    rS = [_dot(wq[h // 2], S_old[h].astype(BF16)) for h in range(GDN_HEADS)]
    vn = [(sol[p][:, GDN_DK:] - jnp.where(top, rS[2 * p][:C2], rS[2 * p + 1][:C2])).astype(BF16) for p in P]
    kdT = [(k2[p] * col(eGl, p, 8)).T.astype(BF16) for p in P]
    zero = jnp.zeros((C2, GDN_DK), BF16)
    for p in P:
        o2 = jnp.where(top, rS[2 * p][C2:], rS[2 * p + 1][C2:]) + _dot(qk[p], vn[p])
        S_sc[2 * p] = S_old[2 * p] * egl[0:1, 8 + p:9 + p] + _dot(kdT[p], jnp.where(top, vn[p], zero))
        S_sc[2 * p + 1] = S_old[2 * p + 1] * egl[C:C + 1, 8 + p:9 + p] + _dot(kdT[p], jnp.where(top, zero, vn[p]))
        for e in range(2):
            h = 2 * p + e
            sl = slice(h * GDN_DK, (h + 1) * GDN_DK)
            o_ref[0, r0:r0 + C, sl] = _gated_out(o2[e * C:(e + 1) * C], z_ref[0, r0:r0 + C, sl],
                                                 gn).astype(o_ref.dtype)


def _gdn_prompt(az, ba, conv_w, par, gn):
    B, T, _ = az.shape
    C = GDN_CHUNK
    return pl.pallas_call(
        _gdn_chunk_kernel,
        out_shape=[jax.ShapeDtypeStruct((B, T, GDN_QK), BF16),
                   jax.ShapeDtypeStruct((B, GDN_HEADS, GDN_DK, GDN_DK), F32)],
        grid=(B, T // (2 * C)),
        in_specs=[pl.BlockSpec((1, 2 * C, CONV_CH), lambda b, c: (b, c, 0)),
                  pl.BlockSpec((1, C, CONV_CH), lambda b, c: (b, jnp.minimum(2 * c + 2, T // C - 1), 0)),
                  pl.BlockSpec((1, 2 * C, GDN_QK), lambda b, c: (b, c, CONV_CH // GDN_QK)),
                  pl.BlockSpec((1, 2 * C, LANES), lambda b, c: (b, c, 0)),
                  pl.BlockSpec((GDN_CONV, CONV_CH), lambda b, c: (0, 0)),
                  pl.BlockSpec((8, LANES), lambda b, c: (0, 0)),
                  pl.BlockSpec((1, GDN_DK), lambda b, c: (0, 0))],
        out_specs=[pl.BlockSpec((1, 2 * C, GDN_QK), lambda b, c: (b, c, 0)),
                   pl.BlockSpec((1, GDN_HEADS, GDN_DK, GDN_DK), lambda b, c: (b, 0, 0, 0))],
        scratch_shapes=[pltpu.VMEM((2 * C, CONV_CH), BF16), pltpu.VMEM((C, CONV_CH), F32),
                        pltpu.VMEM((C, CONV_CH), F32), pltpu.VMEM((GDN_HEADS, GDN_DK, GDN_DK), F32)],
        compiler_params=_params("parallel", "arbitrary"),
        name="gdn_chunked",
    )(az, az, az, ba, conv_w, par, gn)


def _gdn_step_kernel(qkv_ref, hist_ref, z_ref, ba_ref, s0_ref, cw_ref, par_ref, gn_ref, o_ref, s_ref):
    acc = qkv_ref[0] * cw_ref[GDN_CONV - 1:GDN_CONV, :]
    for j in range(GDN_CONV - 1):
        acc = acc + hist_ref[0, j:j + 1, :] * cw_ref[j:j + 1, :]
    x = _silu(acc)
    beta_t, g_t = _gdn_gates(ba_ref[0], par_ref[...])
    eg_t = jnp.exp(g_t)

    def head(off, h):
        return x[:, off + h * GDN_DK: off + (h + 1) * GDN_DK]

    rows = []
    for off, scale in ((GDN_QK, 1.0), (0, GDN_DK ** -0.5)):
        for h in range(GDN_HEADS):
            t = head(off, h)
            rows.append(t * lax.rsqrt(jnp.sum(t * t, axis=-1, keepdims=True) + NORM_EPS) * scale)
    ri = lax.broadcasted_iota(jnp.int32, (LANES, LANES), 0)
    tile = jnp.zeros((LANES, LANES), F32)
    for r, t in enumerate(rows):
        tile = jnp.where(ri == r, t, tile)
    cols = tile.T
    gn = gn_ref[...]
    for h in range(GDN_HEADS):
        lane = _HEAD_ORDER.index(h)
        kcol = cols[:, h:h + 1]
        qcol = cols[:, GDN_HEADS + h:GDN_HEADS + h + 1]
        S = s0_ref[0, h] * eg_t[:, 8 + lane:9 + lane]
        kv = jnp.sum(kcol * S, axis=0, keepdims=True)
        delta = beta_t[:, lane:lane + 1] * (head(2 * GDN_QK, h) - kv)
        S = S + kcol * delta
        s_ref[0, h] = S
        o = jnp.sum(qcol * S, axis=0, keepdims=True)
        sl = slice(h * GDN_DK, (h + 1) * GDN_DK)
        o_ref[0, :, sl] = _gated_out(o, z_ref[0, :, sl], gn).astype(o_ref.dtype)


def _gdn_sample(az, ba, hist, s0, conv_w, par, gn):
    N = az.shape[0]
    return pl.pallas_call(
        _gdn_step_kernel,
        out_shape=[jax.ShapeDtypeStruct((N, 1, GDN_QK), BF16),
                   jax.ShapeDtypeStruct((N, GDN_HEADS, GDN_DK, GDN_DK), F32)],
        grid=(N,),
        in_specs=[pl.BlockSpec((1, 1, CONV_CH), lambda b: (b, 0, 0)),
                  pl.BlockSpec((1, GDN_CONV - 1, CONV_CH), lambda b: (b, 0, 0)),
                  pl.BlockSpec((1, 1, GDN_QK), lambda b: (b, 0, CONV_CH // GDN_QK)),
                  pl.BlockSpec((1, 1, LANES), lambda b: (b, 0, 0)),
                  pl.BlockSpec((1, GDN_HEADS, GDN_DK, GDN_DK), lambda b: (b, 0, 0, 0)),
                  pl.BlockSpec((GDN_CONV, CONV_CH), lambda b: (0, 0)),
                  pl.BlockSpec((8, LANES), lambda b: (0, 0)),
                  pl.BlockSpec((1, GDN_DK), lambda b: (0, 0))],
        out_specs=[pl.BlockSpec((1, 1, GDN_QK), lambda b: (b, 0, 0)),
                   pl.BlockSpec((1, GDN_HEADS, GDN_DK, GDN_DK), lambda b: (b, 0, 0, 0))],
        compiler_params=_params("parallel"),
        name="gdn_step",
    )(az, hist, az, ba, s0, conv_w, par, gn)


DWA_BATCH = 8


def _dwa_blocks(refs, blocks, masks, first, scratch):
    q_ref, k_ref, v_ref = refs
    m_sc, l_sc, acc_sc = scratch
    cur_ok, prev_ok = masks
    scale = DWA_DH ** -0.5
    BL = DWA_BLOCK
    qb = [q_ref[0, cur, :].astype(BF16) for cur, _, _ in blocks]
    sc = [jnp.where(cur_ok, _dot_nt(qb[i], k_ref[0, cur, :].astype(BF16)) * scale, -jnp.inf)
          for i, (cur, _, _) in enumerate(blocks)]
    sp = [None if prev is None else
          jnp.where(prev_ok & on, _dot_nt(qb[i], k_ref[0, prev, :].astype(BF16)) * scale, -jnp.inf)
          for i, (_, prev, on) in enumerate(blocks)]
    m, l, pc, pp = [], [], [], []
    for i in range(len(blocks)):
        mi = jnp.max(sc[i], axis=-1, keepdims=True)
        if sp[i] is not None:
            mi = jnp.maximum(mi, jnp.max(sp[i], axis=-1, keepdims=True))
        p = jnp.exp(sc[i] - mi)
        li = jnp.sum(p, axis=-1, keepdims=True)
        pc.append(p.astype(BF16))
        if sp[i] is not None:
            p = jnp.exp(sp[i] - mi)
            li = li + jnp.sum(p, axis=-1, keepdims=True)
            pp.append(p.astype(BF16))
        else:
            pp.append(None)
        m.append(mi)
        l.append(li)
    acc = [_dot(pc[i], v_ref[0, cur, :].astype(BF16)) for i, (cur, _, _) in enumerate(blocks)]
    acc = [a if pp[i] is None else a + _dot(pp[i], v_ref[0, blocks[i][1], :].astype(BF16))
           for i, a in enumerate(acc)]
    for i, (cur, _, _) in enumerate(blocks):
        if first:
            m_sc[cur, :] = jnp.broadcast_to(m[i], (BL, LANES))
            l_sc[cur, :] = jnp.broadcast_to(l[i], (BL, LANES))
            acc_sc[cur, :] = acc[i]
        else:
            m_old = m_sc[cur, :]
            m_new = jnp.maximum(m_old, m[i])
            a = jnp.exp(m_old - m_new)
            b = jnp.exp(m[i] - m_new)
            m_sc[cur, :] = m_new
            l_sc[cur, :] = a * l_sc[cur, :] + b * l[i]
            acc_sc[cur, :] = a * acc_sc[cur, :] + b * acc[i]


def _dwa_kernel(q0_ref, q1_ref, q2_ref, k0_ref, v0_ref, k1_ref, v1_ref, k2_ref, v2_ref, *rest, n_cast):
    o_ref = rest[n_cast]
    m_sc, l_sc, acc_sc = rest[2 * n_cast + 1:]
    _cast_blocks(rest[:n_cast], rest[n_cast + 1:2 * n_cast + 1])
    BL = DWA_BLOCK
    T = o_ref.shape[1]
    ri = lax.broadcasted_iota(jnp.int32, (BL, BL), 0)
    ci = lax.broadcasted_iota(jnp.int32, (BL, BL), 1)
    masks = (ri >= ci, ci >= ri)
    q_refs, k_refs, v_refs = (q0_ref, q1_ref, q2_ref), (k0_ref, k1_ref, k2_ref), (v0_ref, v1_ref, v2_ref)
    scratch = (m_sc, l_sc, acc_sc)

    for gi, (_, dil) in enumerate(DWA_GROUPS):
        refs = (q_refs[gi], k_refs[gi], v_refs[gi])
        nb = T // dil // BL

        def rows(blk, r, dil=dil):
            start = blk * (BL * dil) + r
            if dil == 1:
                return pl.ds(pl.multiple_of(start, BL), BL)
            return pl.ds(start, BL, stride=dil)

        def block(n, r, nb=nb, rows=rows):
            if nb == 1:
                return rows(n, r), None, None
            return rows(n, r), rows(jnp.maximum(n - 1, 0), r), n > 0

        ns = min(dil, DWA_BATCH)
        nc = min(DWA_BATCH // ns, nb)
        assert dil % ns == 0 and nb % nc == 0
        for r0 in range(0, dil, ns):
            def body(i, carry, r0=r0, refs=refs, block=block, first=gi == 0, ns=ns, nc=nc):
                batch = [block(i * nc + u, r0 + t) for t in range(ns) for u in range(nc)]
                _dwa_blocks(refs, batch, masks, first, scratch)
                return carry
            if nb == nc:
                body(0, 0)
            else:
                lax.fori_loop(0, nb // nc, body, 0)

    o_ref[0] = (acc_sc[...] / l_sc[...]).astype(o_ref.dtype)


def _dwa_prompt(q, kvs, cast=()):
    B, T, _ = q.shape
    c_in, c_out, c_shapes = _cast_jobs(cast, lambda b, h: b * DWA_HEADS + h, B * DWA_HEADS)

    def col(c):
        return pl.BlockSpec((1, T, DWA_DH), lambda b, h: (b, 0, c(h)))

    ng = len(DWA_GROUPS)
    in_specs = [col(lambda h, g=g: g * DWA_HEADS + h) for g in range(ng)]
    args = [q] * ng
    for g in range(ng):
        in_specs += [col(lambda h: h), col(lambda h: DWA_HEADS + h)]
        args += [kvs[g], kvs[g]]
    args += [a for a, *_ in cast]
    res = pl.pallas_call(
        functools.partial(_dwa_kernel, n_cast=len(cast)),
        out_shape=[jax.ShapeDtypeStruct((B, T, DWA_GW), BF16)] + c_shapes,
        grid=(B, DWA_HEADS),
        in_specs=in_specs + c_in,
        out_specs=[col(lambda h: h)] + c_out,
        scratch_shapes=[pltpu.VMEM((T, LANES), F32), pltpu.VMEM((T, LANES), F32), pltpu.VMEM((T, DWA_DH), F32)],
        compiler_params=_params(*(("arbitrary",) * 2 if cast else ("parallel",) * 2)),
        name="dwa_prompt",
    )(*args)
    return res[0], res[1:]


def _dwa_step_kernel(q_ref, n0_ref, n1_ref, n2_ref, c0_ref, c1_ref, c2_ref, o_ref):
    new_refs = (n0_ref, n1_ref, n2_ref)
    cache_refs = (c0_ref, c1_ref, c2_ref)
    scale = DWA_DH ** -0.5
    parts = []
    for gi in range(len(DWA_GROUPS)):
        q = q_ref[0, gi * DWA_HEADS:(gi + 1) * DWA_HEADS, :]
        k = cache_refs[gi][0, :, 0, 0]
        v = cache_refs[gi][0, :, 0, 1]
        s = jnp.sum(k * q[None], axis=-1, keepdims=True) * scale
        s_new = jnp.sum(new_refs[gi][0, 0] * q, axis=-1, keepdims=True) * scale
        m = jnp.maximum(jnp.max(s, axis=0), s_new)
        p = jnp.exp(s - m[None])
        p_new = jnp.exp(s_new - m)
        l = jnp.sum(p, axis=0) + p_new
        acc = jnp.sum(p * v, axis=0) + p_new * new_refs[gi][0, 1]
        parts.append((m, l, acc))
    mm = jnp.maximum(jnp.maximum(parts[0][0], parts[1][0]), parts[2][0])
    num = jnp.zeros((DWA_HEADS, DWA_DH), F32)
    den = jnp.zeros((DWA_HEADS, 1), F32)
    for m, l, acc in parts:
        e = jnp.exp(m - mm)
        num = num + e * acc
        den = den + e * l
    o_ref[0] = num / den


def _dwa_sample(q, kv_new, caches):
    N = q.shape[0]
    span = DWA_BLOCK
    in_specs = [pl.BlockSpec((1, len(DWA_GROUPS) * DWA_HEADS, DWA_DH), lambda b: (b, 0, 0))]
    in_specs += [pl.BlockSpec((1, 2, DWA_HEADS, DWA_DH), lambda b: (b, 0, 0, 0))] * 3
    views = []
    for (win, dil), cache in zip(DWA_GROUPS, caches):
        L = cache.shape[1]
        assert L == win and L // dil == span
        views.append(cache.reshape(N, span, dil, 2, DWA_HEADS, DWA_DH))
        in_specs.append(pl.BlockSpec((1, span, 1, 2, DWA_HEADS, DWA_DH), lambda b: (b, 0, 0, 0, 0, 0)))
    return pl.pallas_call(
        _dwa_step_kernel,
        out_shape=jax.ShapeDtypeStruct((N, DWA_HEADS, DWA_DH), F32),
        grid=(N,),
        in_specs=in_specs,
        out_specs=pl.BlockSpec((1, DWA_HEADS, DWA_DH), lambda b: (b, 0, 0)),
        compiler_params=_params("parallel"),
        name="dwa_step",
    )(q, *kv_new, *views)


def _mix_out_kernel(oa_ref, ob_ref, ga_ref, gb_ref, h_ref, gt_ref, nsh_ref, nsc_ref, ngain_ref,
                    wa_ref, wb_ref, wo_ref, *rest, n_cast):
    out_ref, nxt_ref = rest[n_cast:n_cast + 2]
    _cast_blocks(rest[:n_cast], rest[n_cast + 2:])
    ta = _dot(oa_ref[0], wa_ref[...])
    tb = _dot(ob_ref[0], wb_ref[...])
    merged = _sigmoid(ga_ref[0].astype(F32)) * ta + _sigmoid(gb_ref[0].astype(F32)) * tb
    y = _dot(merged.astype(BF16), wo_ref[...])
    gate, nsc, nsh = gt_ref[0], nsc_ref[0], nsh_ref[0]
    for rows in _row_chunks(y.shape[0]):
        h = h_ref[0, rows, :] + _rows_of(gate, rows) * y[rows]
        out_ref[0, rows, :] = h
        nxt_ref[0, rows, :] = _modulated(h, ngain_ref[...], _rows_of(nsc, rows), _rows_of(nsh, rows))


MIX_TM = 512


def _mix_out(oa, ob, gates, h, mod3, wa, wb, wo, next_gain, cast=()):
    G, R, _ = h.shape
    tm = min(MIX_TM, R)
    c_in, c_out, c_shapes = _cast_jobs(cast, lambda g, i: g * (R // tm) + i, G * (R // tm))

    def rows(width, col=0):
        return pl.BlockSpec((1, tm, width), lambda g, i: (g, i, col))

    def whole(shape):
        return pl.BlockSpec(shape, lambda g, i: (0, 0), pipeline_mode=pl.Buffered(1))

    res = pl.pallas_call(
        functools.partial(_mix_out_kernel, n_cast=len(cast)),
        out_shape=[jax.ShapeDtypeStruct((G, R, D_MODEL), F32), jax.ShapeDtypeStruct((G, R, D_MODEL), BF16)]
        + c_shapes,
        grid=(G, R // tm),
        in_specs=[rows(GDN_QK), rows(DWA_GW), rows(D_MODEL, 0), rows(D_MODEL, 1), rows(D_MODEL),
                  _mod_spec(mod3, R, tm, 5), _mod_spec(mod3, R, tm, 6), _mod_spec(mod3, R, tm, 7),
                  pl.BlockSpec((1, D_MODEL), lambda g, i: (0, 0)),
                  whole(wa.shape), whole(wb.shape), whole(wo.shape)] + c_in,
        out_specs=[rows(D_MODEL), rows(D_MODEL)] + c_out,
        compiler_params=_params(*(("arbitrary",) * 2 if cast else ("parallel",) * 2)),
        name="mix_out",
    )(oa, ob, gates, gates, h, mod3, mod3, mod3, next_gain.reshape(1, D_MODEL), wa, wb, wo,
      *[a for a, *_ in cast])
    return res[0], res[1], res[2:]


_W_IN_COLS = _C_GB + D_MODEL
_W_IN_ALIGNED = _W_IN_COLS // CAST_BLOCK * CAST_BLOCK


def _mixer_weights(wt_f32, wt_bf, a_log, dt_bias):
    w = wt_bf
    order = np.array(_HEAD_ORDER)
    ba_rows = np.concatenate([_C_B + order, _C_A + order])
    par = jnp.zeros((8, LANES), F32)
    par = par.at[0, 8:16].set(a_log[0][order]).at[1, 8:16].set(dt_bias[0][order])
    return dict(
        az=w,
        ba=jnp.pad(w[ba_rows], ((0, LANES - 2 * GDN_HEADS), (0, 0))),
        attn=jnp.concatenate(
            [w[c + g * DWA_GW:c + (g + 1) * DWA_GW] for g in range(len(DWA_GROUPS)) for c in (_C_K, _C_V)]
            + [w[_C_Q:_C_K]], axis=0),
        gates=jnp.concatenate([w[_C_GA:], wt_f32[_W_IN_ALIGNED:].astype(BF16)], axis=0),
        par=par,
    )


def _trunk(h1, xm, mod3, pos, W, norms, ffn2, *, sample_state=None):
    G, R, _ = h1.shape
    up2, down2 = ffn2
    tables = _rope_tables(pos)
    ba = _proj(xm, W["ba"], F32, tn=LANES, name="proj_ba")
    gates = _proj(xm, W["gates"], BF16, tn=1024, name="proj_gates")
    kvs, q = _attn_proj(xm, W["attn"], tables)
    if sample_state is None:
        az, conv_rows = _proj(xm, W["az"], BF16, n=_C_B, tn=1024, tail=True, name="proj_az")
        o_a, s_new = _gdn_prompt(az, ba, W["conv"], W["par"], W["gn"])
        o_b, (up2,) = _dwa_prompt(q, kvs, cast=((up2, 1, 2 * D_FF, 4 * CAST_BLOCK),))
        down_cast = ((down2, 0, D_FF, 2 * CAST_BLOCK),)
    else:
        hist, s0, caches = sample_state
        conv_rows = _proj(xm, W["az"], F32, n=_C_B, tn=1024, name="proj_az").reshape(R, 1, -1)
        o_a, s_new = _gdn_sample(conv_rows, ba.reshape(R, 1, LANES), hist, s0, W["conv"], W["par"], W["gn"])
        o_b = _dwa_sample(q.reshape(R, len(DWA_GROUPS) * DWA_HEADS, DWA_DH),
                          [kv.reshape(R, 2, DWA_HEADS, DWA_DH) for kv in kvs], caches)
        o_a = o_a.reshape(1, R, -1)
        o_b = o_b.reshape(1, R, DWA_GW).astype(BF16)
        down_cast = ()
    h2, xm2, cast_out = _mix_out(o_a, o_b, gates, h1, mod3, W["wa"], W["wb"], W["wo"], norms["ffn2"],
                                 cast=down_cast)
    if down_cast:
        (down2,) = cast_out
    y = _ffn_last(h2, xm2, mod3, up2, down2, norms["final"])
    return y, conv_rows[:, :, :CONV_CH], s_new, kvs, (up2, down2)


def kernel(x_prompt, x_sample, state_conv, state_delta, cache_kv_w128, cache_kv_w512, cache_kv_w2048, c_prompt, c_sample, w_ada, b_ada, norm_ffn1, w_ffn1_up, w_ffn1_down, norm_mix, w_in, conv_w, a_log, dt_bias, gdn_norm, w_proj_a, w_proj_b, w_out, norm_ffn2, w_ffn2_up, w_ffn2_down, norm_final):
    B, T, _ = x_prompt.shape
    N, S, _ = x_sample.shape
    assert S == 1 and T % (DWA_BLOCK * DWA_GROUPS[-1][1]) == 0
    norms = dict(ffn1=norm_ffn1[0], mix=norm_mix[0], ffn2=norm_ffn2[0], final=norm_final)

    mod = _ada(jnp.concatenate([c_prompt, c_sample], axis=0), w_ada[0], b_ada)
    mod_p = mod[:B].reshape(B, 1, N_MOD * D_MODEL)
    mod_s = mod[B:].reshape(1, N, N_MOD * D_MODEL)

    up1, down1 = w_ffn1_up[0].astype(BF16), w_ffn1_down[0].astype(BF16)
    w_in_t = jnp.swapaxes(w_in[0], 0, 1)
    h1_p, xm_p, w_in_bf = _ffn_first(x_prompt, mod_p, norms["ffn1"], up1, down1, norms["mix"],
                                     cast=((w_in_t, 0, _W_IN_ALIGNED, CAST_BLOCK),))
    W = _mixer_weights(w_in_t, w_in_bf, a_log, dt_bias)
    W.update(conv=conv_w[0], gn=gdn_norm,
             wa=w_proj_a[0].astype(BF16), wb=w_proj_b[0].astype(BF16), wo=w_out[0].astype(BF16))
    h1_s, xm_s = _ffn_first(x_sample.reshape(1, N, D_MODEL), mod_s, norms["ffn1"], up1, down1, norms["mix"])

    y_p, rows_p, s_p, kv_p, ffn2 = _trunk(h1_p, xm_p, mod_p, jnp.arange(T, dtype=jnp.int32), W, norms,
                                          (w_ffn2_up[0], w_ffn2_down[0]))
    caches = (cache_kv_w128[0], cache_kv_w512[0], cache_kv_w2048[0])
    y_s, rows_s, s_s, kv_s, _ = _trunk(h1_s, xm_s, mod_s, jnp.full((N,), PAST_LEN, dtype=jnp.int32), W, norms,
                                       ffn2, sample_state=(state_conv[0], state_delta[0], caches))

    keep = GDN_CONV - 1
    conv_p = rows_p[:, TAIL_ROWS - keep:][None]
    conv_s = jnp.concatenate([state_conv[0], rows_s], axis=1)[:, -keep:][None]
    kv_out_p = []
    for (win, _), kv in zip(DWA_GROUPS, kv_p):
        k = min(win, T)
        kv_out_p.append(kv[:, T - k:].reshape(1, B, k, 2, DWA_HEADS, DWA_DH))
    kv_out_s = [kv.reshape(1, N, 1, 2, DWA_HEADS, DWA_DH) for kv in kv_s]
    return (y_p, y_s.reshape(N, 1, D_MODEL), conv_p, s_p[None], *kv_out_p,
            conv_s, s_s[None], *kv_out_s)
```

```python
import functools

import jax
import jax.numpy as jnp
import numpy as np
from jax import lax
from jax.experimental import pallas as pl
from jax.experimental.pallas import tpu as pltpu

F32 = jnp.float32
BF16 = jnp.bfloat16

D_MODEL = 2048
D_FF = 5632
N_MOD = 9
NORM_EPS = 1e-6
PAST_LEN = 16384

GDN_HEADS = 8
GDN_DK = 128
GDN_CONV = 4
GDN_CHUNK = 64
GDN_QK = GDN_HEADS * GDN_DK
CONV_CH = 3 * GDN_QK

DWA_GROUPS = ((128, 1), (512, 4), (2048, 16))
DWA_HEADS = 4
DWA_DH = 128
DWA_GW = DWA_HEADS * DWA_DH
DWA_WIDTH = len(DWA_GROUPS) * DWA_GW
DWA_BLOCK = 128
ROPE_THETA = 10000.0

LANES = 128
MXU_COLS = 256
VMEM_LIMIT = 56 * 1024 * 1024

_C_Z = CONV_CH
_C_B = _C_Z + GDN_QK
_C_A = _C_B + GDN_HEADS
_C_Q = _C_A + GDN_HEADS
_C_K = _C_Q + DWA_WIDTH
_C_V = _C_K + DWA_WIDTH
_C_GA = _C_V + DWA_WIDTH
_C_GB = _C_GA + D_MODEL

_HEAD_ORDER = (0, 2, 4, 6, 1, 3, 5, 7)
_N_PAIRS = GDN_HEADS // 2


def _sigmoid(x):
    return 0.5 * jnp.tanh(0.5 * x) + 0.5


def _silu(x):
    return x * _sigmoid(x)


def _dot(a, b):
    return jnp.dot(a, b, preferred_element_type=F32)


def _dot_nt(a, b):
    return lax.dot_general(a, b, (((1,), (1,)), ((), ())), preferred_element_type=F32)


def _params(*sem):
    return pltpu.CompilerParams(dimension_semantics=sem, vmem_limit_bytes=VMEM_LIMIT)


def _ada_kernel(c_ref, w_ref, b_ref, o_ref):
    a = _silu(c_ref[...]).astype(BF16)
    o_ref[...] = _dot(a, w_ref[...].astype(BF16)) + b_ref[...]


def _ada(c, w, b):
    m, n = c.shape[0], w.shape[1]
    tn = 1024
    return pl.pallas_call(
        _ada_kernel,
        out_shape=jax.ShapeDtypeStruct((m, n), F32),
        grid=(n // tn,),
        in_specs=[pl.BlockSpec((m, D_MODEL), lambda j: (0, 0)),
                  pl.BlockSpec((D_MODEL, tn), lambda j: (0, j)),
                  pl.BlockSpec((1, tn), lambda j: (0, j))],
        out_specs=pl.BlockSpec((m, tn), lambda j: (0, j)),
        compiler_params=_params("arbitrary"),
        name="ada_mod",
    )(c, w, b)


def _rms(x):
    return x * lax.rsqrt(jnp.mean(x * x, axis=-1, keepdims=True) + NORM_EPS)


def _modulated(h, gain, scale, shift):
    return ((_rms(h) * gain) * (1.0 + scale) + shift).astype(BF16)


ROW_CHUNK = 32


def _row_chunks(n_rows):
    return [slice(r, min(r + ROW_CHUNK, n_rows)) for r in range(0, n_rows, ROW_CHUNK)]


def _rows_of(v, rows):
    return v if v.shape[0] == 1 else v[rows]


def _swiglu_step(xm, wg_ref, wu_ref, wd_ref, acc_sc):
    g = _dot(xm, wg_ref[...])
    u = _dot(xm, wu_ref[...])
    acc_sc[...] += _dot((_silu(g) * u).astype(BF16), wd_ref[...])


def _ffn_first_kernel(x_ref, sh_ref, sc_ref, gt_ref, gain_ref, wg_ref, wu_ref, wd_ref,
                      nsh_ref, nsc_ref, ngain_ref, *rest, n_cast):
    cast_in = rest[:n_cast]
    out_ref, nxt_ref = rest[n_cast:n_cast + 2]
    cast_out = rest[n_cast + 2:2 * n_cast + 2]
    xm_sc, acc_sc = rest[2 * n_cast + 2:]
    j = pl.program_id(2)

    tm = x_ref.shape[1]

    @pl.when(j == 0)
    def _():
        sc, sh = sc_ref[0], sh_ref[0]
        for rows in _row_chunks(tm):
            xm_sc[rows, :] = _modulated(x_ref[0, rows, :], gain_ref[...], _rows_of(sc, rows), _rows_of(sh, rows))
        acc_sc[...] = jnp.zeros_like(acc_sc)

    _cast_blocks(cast_in, cast_out)
    _swiglu_step(xm_sc[...], wg_ref, wu_ref, wd_ref, acc_sc)

    @pl.when(j == pl.num_programs(2) - 1)
    def _():
        half_gate, nsc, nsh = 0.5 * gt_ref[0], nsc_ref[0], nsh_ref[0]
        for rows in _row_chunks(tm):
            h = x_ref[0, rows, :] + _rows_of(half_gate, rows) * acc_sc[rows, :]
            out_ref[0, rows, :] = h
            nxt_ref[0, rows, :] = _modulated(h, ngain_ref[...], _rows_of(nsc, rows), _rows_of(nsh, rows))


def _ffn_last_kernel(x_ref, xm_ref, gt_ref, wg_ref, wu_ref, wd_ref, ngain_ref, out_ref, acc_sc):
    j = pl.program_id(2)

    @pl.when(j == 0)
    def _():
        acc_sc[...] = jnp.zeros_like(acc_sc)

    _swiglu_step(xm_ref[0], wg_ref, wu_ref, wd_ref, acc_sc)

    @pl.when(j == pl.num_programs(2) - 1)
    def _():
        half_gate = 0.5 * gt_ref[0]
        for rows in _row_chunks(x_ref.shape[1]):
            h = x_ref[0, rows, :] + _rows_of(half_gate, rows) * acc_sc[rows, :]
            out_ref[0, rows, :] = _rms(h) * ngain_ref[...]


FFN_TM = 512
FFN_TF = 512


def _mod_spec(mod3, R, tm, k):
    if mod3.shape[1] == R:
        return pl.BlockSpec((1, tm, D_MODEL), lambda g, i, *_: (g, i, k))
    return pl.BlockSpec((1, 1, D_MODEL), lambda g, i, *_: (g, 0, k))


def _ffn_specs(R):
    tm = min(FFN_TM, R)
    nff = D_FF // FFN_TF
    rows = pl.BlockSpec((1, tm, D_MODEL), lambda g, i, j: (g, i, 0))
    vec = pl.BlockSpec((1, D_MODEL), lambda g, i, j: (0, 0))
    weights = [pl.BlockSpec((D_MODEL, FFN_TF), lambda g, i, j: (0, j)),
               pl.BlockSpec((D_MODEL, FFN_TF), lambda g, i, j: (0, j + nff)),
               pl.BlockSpec((FFN_TF, D_MODEL), lambda g, i, j: (j, 0))]
    return tm, nff, rows, vec, weights


CAST_BLOCK = 128


def _cast_jobs(arrays, step_of, total_steps):
    in_specs, out_specs, out_shapes = [], [], []
    start = 0
    for arr, axis, extent, slab in arrays:
        assert extent % slab == 0 and slab % CAST_BLOCK == 0
        nblk = extent // slab
        shape = tuple(slab if d == axis else n for d, n in enumerate(arr.shape))

        def index(*grid, start=start, nblk=nblk, axis=axis):
            blk = jnp.clip(step_of(*grid) - start, 0, nblk - 1)
            return (blk, 0) if axis == 0 else (0, blk)

        in_specs.append(pl.BlockSpec(shape, index))
        out_specs.append(pl.BlockSpec(shape, index))
        out_shapes.append(jax.ShapeDtypeStruct(
            tuple(extent if d == axis else n for d, n in enumerate(arr.shape)), BF16))
        start += nblk
    assert start <= total_steps, (start, total_steps)
    return in_specs, out_specs, out_shapes


def _cast_blocks(cast_in, cast_out):
    for src, dst in zip(cast_in, cast_out):
        dst[...] = src[...].astype(BF16)


def _ffn_first(x3, mod3, gain, w_up, w_down, next_gain, cast=()):
    G, R, _ = x3.shape
    tm, nff, rows, vec, weights = _ffn_specs(R)
    ms = [_mod_spec(mod3, R, tm, k) for k in range(5)]
    c_in, c_out, c_shapes = _cast_jobs(cast, lambda g, i, j: (g * (R // tm) + i) * nff + j, G * (R // tm) * nff)
    sem = ("arbitrary",) * 3 if cast else ("parallel", "parallel", "arbitrary")
    return pl.pallas_call(
        functools.partial(_ffn_first_kernel, n_cast=len(cast)),
        out_shape=[jax.ShapeDtypeStruct((G, R, D_MODEL), F32), jax.ShapeDtypeStruct((G, R, D_MODEL), BF16)]
        + c_shapes,
        grid=(G, R // tm, nff),
        in_specs=[rows, ms[0], ms[1], ms[2], vec] + weights + [ms[3], ms[4], vec] + c_in,
        out_specs=[rows, rows] + c_out,
        scratch_shapes=[pltpu.VMEM((tm, D_MODEL), BF16), pltpu.VMEM((tm, D_MODEL), F32)],
        compiler_params=_params(*sem),
        name="ffn_first",
    )(x3, mod3, mod3, mod3, gain.reshape(1, D_MODEL), w_up, w_up, w_down, mod3, mod3,
      next_gain.reshape(1, D_MODEL), *[a for a, *_ in cast])


def _ffn_last(x3, xm3, mod3, w_up, w_down, final_gain):
    G, R, _ = x3.shape
    tm, nff, rows, vec, weights = _ffn_specs(R)
    return pl.pallas_call(
        _ffn_last_kernel,
        out_shape=jax.ShapeDtypeStruct((G, R, D_MODEL), F32),
        grid=(G, R // tm, nff),
        in_specs=[rows, rows, _mod_spec(mod3, R, tm, 8)] + weights + [vec],
        out_specs=rows,
        scratch_shapes=[pltpu.VMEM((tm, D_MODEL), F32)],
        compiler_params=_params("parallel", "parallel", "arbitrary"),
        name="ffn_last",
    )(x3, xm3, mod3, w_up, w_up, w_down, final_gain.reshape(1, D_MODEL))


def _rope_tile(x, cosf, sinf):
    return x * cosf + pltpu.roll(x, DWA_DH // 2, axis=1) * sinf


TAIL_ROWS = 8


def _proj_kernel(a_ref, wt_ref, *rest, tail):
    acc = _dot_nt(a_ref[0], wt_ref[...])
    rest[0][0] = acc.astype(rest[0].dtype)
    if tail:
        rest[1][0] = acc[acc.shape[0] - TAIL_ROWS:]


PROJ_TM = 1024


def _proj(xm3, wt, out_dtype, *, n=None, tn=512, tail=False, name="proj"):
    G, R, _ = xm3.shape
    n = wt.shape[0] if n is None else n
    tm = min(PROJ_TM, R)
    out_shape = jax.ShapeDtypeStruct((G, R, n), out_dtype)
    out_specs = pl.BlockSpec((1, tm, tn), lambda g, i, j: (g, i, j))
    if tail:
        out_shape = [out_shape, jax.ShapeDtypeStruct((G, R // tm * TAIL_ROWS, n), F32)]
        out_specs = [out_specs, pl.BlockSpec((1, TAIL_ROWS, tn), lambda g, i, j: (g, i, j))]
    res = pl.pallas_call(
        functools.partial(_proj_kernel, tail=tail),
        out_shape=out_shape,
        grid=(G, R // tm, n // tn),
        in_specs=[pl.BlockSpec((1, tm, D_MODEL), lambda g, i, j: (g, i, 0)),
                  pl.BlockSpec((tn, D_MODEL), lambda g, i, j: (j, 0))],
        out_specs=out_specs,
        compiler_params=_params("parallel", "parallel", "arbitrary"),
        name=name,
    )(xm3, wt)
    if tail:
        return res[0], res[1][:, -TAIL_ROWS:]
    return res


ATTN_TILES = tuple((2 * g, 2 * g + 2) for g in range(len(DWA_GROUPS))) + ((2 * len(DWA_GROUPS), 3 * len(DWA_GROUPS)),)


def _attn_proj_kernel(a_ref, wt_ref, cos_ref, sin_ref, *o_refs):
    j = pl.program_id(2)

    def tile(o_ref, rotate):
        a = a_ref[0]
        for c in range(0, DWA_GW, MXU_COLS):
            acc = _dot_nt(a, wt_ref[c:c + MXU_COLS, :])
            for h in range(0, MXU_COLS, DWA_DH):
                x = acc[:, h:h + DWA_DH]
                o_ref[0, :, c + h:c + h + DWA_DH] = _rope_tile(x, cos_ref[...], sin_ref[...]) if rotate else x

    for o_ref, (lo, hi) in zip(o_refs[:-1], ATTN_TILES[:-1]):
        pl.when(j == lo)(functools.partial(tile, o_ref, True))
        pl.when(j == lo + 1)(functools.partial(tile, o_ref, False))
    pl.when(j >= ATTN_TILES[-1][0])(functools.partial(tile, o_refs[-1], True))


def _attn_proj(xm3, wt, tables):
    G, R, _ = xm3.shape
    tm = min(PROJ_TM, R)
    tab = pl.BlockSpec((tm, DWA_DH), lambda g, i, j: (i, 0))
    out_shape, out_specs = [], []
    for lo, hi in ATTN_TILES:
        out_shape.append(jax.ShapeDtypeStruct((G, R, (hi - lo) * DWA_GW), F32))
        out_specs.append(pl.BlockSpec((1, tm, DWA_GW),
                                      lambda g, i, j, lo=lo, hi=hi: (g, i, jnp.clip(j - lo, 0, hi - lo - 1))))
    res = pl.pallas_call(
        _attn_proj_kernel,
        out_shape=out_shape,
        grid=(G, R // tm, ATTN_TILES[-1][1]),
        in_specs=[pl.BlockSpec((1, tm, D_MODEL), lambda g, i, j: (g, i, 0)),
                  pl.BlockSpec((DWA_GW, D_MODEL), lambda g, i, j: (j, 0)), tab, tab],
        out_specs=out_specs,
        compiler_params=_params("parallel", "arbitrary", "arbitrary"),
        name="proj_attn",
    )(xm3, wt, *tables)
    return res[:-1], res[-1]


def _rope_tables(pos):
    half = DWA_DH // 2
    inv = jnp.power(ROPE_THETA, -jnp.arange(half, dtype=F32) / half)
    ang = pos.astype(F32)[:, None] * inv[None, :]
    cos, sin = jnp.cos(ang), jnp.sin(ang)
    return jnp.concatenate([cos, cos], axis=-1), jnp.concatenate([-sin, sin], axis=-1)


def _gdn_gates(ba, par):
    beta = _sigmoid(ba)
    x = ba + par[1:2]
    softplus = jnp.maximum(x, 0.0) + jnp.log1p(jnp.exp(-jnp.abs(x)))
    return beta, -jnp.exp(par[0:1]) * softplus


def _gated_out(o, z, gn):
    return (_rms(o) * gn) * _silu(z.astype(F32))


def _l2norm(t):
    return t * lax.rsqrt(jnp.sum(t * t, axis=-1, keepdims=True) + NORM_EPS)


GDN_HIST = 16
GDN_STEP_CHUNKS = 8


def _gdn_frontend(hist, raw, cw_ref, buf_sc, x_out):
    C = GDN_CHUNK
    buf_sc[0:GDN_HIST, :] = hist
    buf_sc[GDN_HIST:GDN_HIST + C, :] = raw
    taps = GDN_CONV - 1
    ri = lax.broadcasted_iota(jnp.int32, (taps * C, buf_sc.shape[0]), 0)
    ci = lax.broadcasted_iota(jnp.int32, (taps * C, buf_sc.shape[0]), 1)
    shift = jnp.where(ci == GDN_HIST + ri % C - (ri // C + 1), 1.0, 0.0).astype(BF16)
    back = _dot(shift, buf_sc[...])
    acc = raw.astype(F32) * cw_ref[taps:taps + 1, :]
    for s in range(taps):
        acc = acc + back[s * C:(s + 1) * C] * cw_ref[taps - 1 - s:taps - s, :]
    x = _silu(acc)
    for h in range(2 * GDN_HEADS):
        sl = slice(h * GDN_DK, (h + 1) * GDN_DK)
        x_out[:, sl] = _l2norm(x[:, sl]) * (GDN_DK ** -0.5 if h < GDN_HEADS else 1.0)
    x_out[:, 2 * GDN_QK:] = x[:, 2 * GDN_QK:]


def _gdn_chunk_kernel(cur_ref, nxt_ref, z_ref, ba_ref, cw_ref, par_ref, gn_ref, o_ref, s_ref,
                      buf_sc, xa_sc, xb_sc, S_sc):
    C = GDN_CHUNK
    c = pl.program_id(1)
    x_bufs = (xa_sc, xb_sc)

    @pl.when(c == 0)
    def _():
        S_sc[...] = jnp.zeros_like(S_sc)
        buf_sc[...] = jnp.zeros_like(buf_sc)
        _gdn_frontend(jnp.zeros((GDN_HIST, CONV_CH), BF16), cur_ref[0, 0:C, :], cw_ref, buf_sc, xa_sc)

    refs = (z_ref, ba_ref, par_ref, gn_ref, o_ref, S_sc)
    for k in range(GDN_STEP_CHUNKS):
        last = k == GDN_STEP_CHUNKS - 1
        following = nxt_ref[0] if last else cur_ref[0, (k + 1) * C:(k + 2) * C, :]
        _gdn_frontend(cur_ref[0, (k + 1) * C - GDN_HIST:(k + 1) * C, :], following, cw_ref, buf_sc,
                      x_bufs[(k + 1) % 2])
        _gdn_chain(x_bufs[k % 2], k * C, *refs)

    @pl.when(c == pl.num_programs(1) - 1)
    def _():
        s_ref[0] = S_sc[...]


def _gdn_chain(x_sc, r0, z_ref, ba_ref, par_ref, gn_ref, o_ref, S_sc):
    C = GDN_CHUNK
    beta_t, g_t = _gdn_gates(ba_ref[0, r0:r0 + C, :], par_ref[...])
    row = lax.broadcasted_iota(jnp.int32, (C, LANES), 0)
    G = g_t
    s = 1
    while s < C:
        G = G + jnp.where(row >= s, pltpu.roll(G, s, axis=0), 0.0)
        s *= 2
    shift = LANES - _N_PAIRS
    Gs = jnp.concatenate([G, pltpu.roll(G, shift, axis=1)], axis=0)
    Bs = jnp.concatenate([beta_t, pltpu.roll(beta_t, shift, axis=1)], axis=0)
    GT = Gs.T
    r2 = lax.broadcasted_iota(jnp.int32, (2 * C, LANES), 0)
    Glast = jnp.where(r2 < C, Gs[C - 1:C, :], Gs[2 * C - 1:2 * C, :])
    eG = jnp.exp(Gs)
    eGl = jnp.exp(Glast - Gs)
    egl = jnp.exp(Glast)

    ii = lax.broadcasted_iota(jnp.int32, (2 * C, 2 * C), 0)
    jj = lax.broadcasted_iota(jnp.int32, (2 * C, 2 * C), 1)
    same = (ii // C) == (jj // C)
    strict = same & (ii > jj)
    diag = ii == jj
    blk = (ii // 16) == (jj // 16)
    top = r2 < C
    gn = gn_ref[...]

    P = range(_N_PAIRS)
    C2 = 2 * C

    def pair(p, off):
        a = x_sc[:, off + (2 * p) * GDN_DK: off + (2 * p + 1) * GDN_DK]
        b = x_sc[:, off + (2 * p + 1) * GDN_DK: off + (2 * p + 2) * GDN_DK]
        return jnp.concatenate([a, b], axis=0)

    def col(t, p, base=0):
        return t[:, base + p:base + p + 1]

    q2 = [pair(p, 0) for p in P]
    k2 = [pair(p, GDN_QK) for p in P]
    kb = [k2[p] * col(Bs, p) for p in P]
    vb = [pair(p, 2 * GDN_QK) * col(Bs, p) for p in P]
    dec = [jnp.where(strict, jnp.exp(jnp.where(strict, col(Gs, p, 8) - GT[8 + p:9 + p, :], 0.0)), 0.0) for p in P]
    kq = [_dot_nt(jnp.concatenate([kb[p], q2[p]], axis=0).astype(BF16), k2[p].astype(BF16)) for p in P]
    A = [kq[p][:C2] * dec[p] for p in P]
    qk = [(kq[p][C2:] * (dec[p] + jnp.where(diag, 1.0, 0.0))).astype(BF16) for p in P]

    Dg = [jnp.where(blk, A[p], 0.0) for p in P]
    E = [A[p] - Dg[p] for p in P]
    Q = [-Dg[p] for p in P]
    Dgb = [Dg[p].astype(BF16) for p in P]
    Dp = [_dot(Dgb[p], Dgb[p]) for p in P]
    for _ in range(2):
        Dpb = [Dp[p].astype(BF16) for p in P]
        st = [_dot(jnp.concatenate([Dpb[p], Q[p].astype(BF16)], axis=0), Dpb[p]) for p in P]
        Q = [Q[p] + Dp[p] + st[p][C2:] for p in P]
        Dp = [st[p][:C2] for p in P]
    Q = [Q[p] + Dp[p] + _dot(Q[p].astype(BF16), Dp[p].astype(BF16)) for p in P]
    rhs = [jnp.concatenate([kb[p] * col(eG, p, 8), vb[p]], axis=1) for p in P]
    er = [_dot(Q[p].astype(BF16), jnp.concatenate([E[p], rhs[p]], axis=1).astype(BF16)) for p in P]
    N = [E[p] + er[p][:, :C2] for p in P]
    y = [rhs[p] + er[p][:, C2:] for p in P]
    Nb = [N[p].astype(BF16) for p in P]
    ny = [_dot(Nb[p], jnp.concatenate([N[p], y[p]], axis=1).astype(BF16)) for p in P]
    zz = [y[p] - ny[p][:, C2:] for p in P]
    sol = [zz[p] + _dot(ny[p][:, :C2].astype(BF16), zz[p].astype(BF16)) for p in P]

    wq = [jnp.concatenate([sol[p][:, :GDN_DK], q2[p] * col(eG, p, 8)], axis=0).astype(BF16) for p in P]
    S_old = [S_sc[h] for h in range(GDN_HEADS)]
    rS = [_dot(wq[h // 2], S_old[h].astype(BF16)) for h in range(GDN_HEADS)]
    vn = [(sol[p][:, GDN_DK:] - jnp.where(top, rS[2 * p][:C2], rS[2 * p + 1][:C2])).astype(BF16) for p in P]
    kdT = [(k2[p] * col(eGl, p, 8)).T.astype(BF16) for p in P]
    zero = jnp.zeros((C2, GDN_DK), BF16)
    for p in P:
        o2 = jnp.where(top, rS[2 * p][C2:], rS[2 * p + 1][C2:]) + _dot(qk[p], vn[p])
        S_sc[2 * p] = S_old[2 * p] * egl[0:1, 8 + p:9 + p] + _dot(kdT[p], jnp.where(top, vn[p], zero))
        S_sc[2 * p + 1] = S_old[2 * p + 1] * egl[C:C + 1, 8 + p:9 + p] + _dot(kdT[p], jnp.where(top, zero, vn[p]))
        for e in range(2):
            h = 2 * p + e
            sl = slice(h * GDN_DK, (h + 1) * GDN_DK)
            o_ref[0, r0:r0 + C, sl] = _gated_out(o2[e * C:(e + 1) * C], z_ref[0, r0:r0 + C, sl],
                                                 gn).astype(o_ref.dtype)


def _gdn_prompt(az, ba, conv_w, par, gn):
    B, T, _ = az.shape
    C = GDN_CHUNK
    R = GDN_STEP_CHUNKS * C
    assert GDN_STEP_CHUNKS % 2 == 0 and T % R == 0
    return pl.pallas_call(
        _gdn_chunk_kernel,
        out_shape=[jax.ShapeDtypeStruct((B, T, GDN_QK), BF16),
                   jax.ShapeDtypeStruct((B, GDN_HEADS, GDN_DK, GDN_DK), F32)],
        grid=(B, T // R),
        in_specs=[pl.BlockSpec((1, R, CONV_CH), lambda b, c: (b, c, 0)),
                  pl.BlockSpec((1, C, CONV_CH),
                               lambda b, c: (b, jnp.minimum((c + 1) * GDN_STEP_CHUNKS, T // C - 1), 0)),
                  pl.BlockSpec((1, R, GDN_QK), lambda b, c: (b, c, CONV_CH // GDN_QK)),
                  pl.BlockSpec((1, R, LANES), lambda b, c: (b, c, 0)),
                  pl.BlockSpec((GDN_CONV, CONV_CH), lambda b, c: (0, 0)),
                  pl.BlockSpec((8, LANES), lambda b, c: (0, 0)),
                  pl.BlockSpec((1, GDN_DK), lambda b, c: (0, 0))],
        out_specs=[pl.BlockSpec((1, R, GDN_QK), lambda b, c: (b, c, 0)),
                   pl.BlockSpec((1, GDN_HEADS, GDN_DK, GDN_DK), lambda b, c: (b, 0, 0, 0))],
        scratch_shapes=[pltpu.VMEM((2 * C, CONV_CH), BF16), pltpu.VMEM((C, CONV_CH), F32),
                        pltpu.VMEM((C, CONV_CH), F32), pltpu.VMEM((GDN_HEADS, GDN_DK, GDN_DK), F32)],
        compiler_params=_params("parallel", "arbitrary"),
        name="gdn_chunked",
    )(az, az, az, ba, conv_w, par, gn)


def _gdn_step_kernel(qkv_ref, hist_ref, z_ref, ba_ref, s0_ref, cw_ref, par_ref, gn_ref, o_ref, s_ref):
    acc = qkv_ref[0] * cw_ref[GDN_CONV - 1:GDN_CONV, :]
    for j in range(GDN_CONV - 1):
        acc = acc + hist_ref[0, j:j + 1, :] * cw_ref[j:j + 1, :]
    x = _silu(acc)
    beta_t, g_t = _gdn_gates(ba_ref[0], par_ref[...])
    eg_t = jnp.exp(g_t)

    def head(off, h):
        return x[:, off + h * GDN_DK: off + (h + 1) * GDN_DK]

    rows = []
    for off, scale in ((GDN_QK, 1.0), (0, GDN_DK ** -0.5)):
        for h in range(GDN_HEADS):
            t = head(off, h)
            rows.append(t * lax.rsqrt(jnp.sum(t * t, axis=-1, keepdims=True) + NORM_EPS) * scale)
    ri = lax.broadcasted_iota(jnp.int32, (LANES, LANES), 0)
    tile = jnp.zeros((LANES, LANES), F32)
    for r, t in enumerate(rows):
        tile = jnp.where(ri == r, t, tile)
    cols = tile.T
    gn = gn_ref[...]
    for h in range(GDN_HEADS):
        lane = _HEAD_ORDER.index(h)
        kcol = cols[:, h:h + 1]
        qcol = cols[:, GDN_HEADS + h:GDN_HEADS + h + 1]
        S = s0_ref[0, h] * eg_t[:, 8 + lane:9 + lane]
        kv = jnp.sum(kcol * S, axis=0, keepdims=True)
        delta = beta_t[:, lane:lane + 1] * (head(2 * GDN_QK, h) - kv)
        S = S + kcol * delta
        s_ref[0, h] = S
        o = jnp.sum(qcol * S, axis=0, keepdims=True)
        sl = slice(h * GDN_DK, (h + 1) * GDN_DK)
        o_ref[0, :, sl] = _gated_out(o, z_ref[0, :, sl], gn).astype(o_ref.dtype)


def _gdn_sample(az, ba, hist, s0, conv_w, par, gn):
    N = az.shape[0]
    return pl.pallas_call(
        _gdn_step_kernel,
        out_shape=[jax.ShapeDtypeStruct((N, 1, GDN_QK), BF16),
                   jax.ShapeDtypeStruct((N, GDN_HEADS, GDN_DK, GDN_DK), F32)],
        grid=(N,),
        in_specs=[pl.BlockSpec((1, 1, CONV_CH), lambda b: (b, 0, 0)),
                  pl.BlockSpec((1, GDN_CONV - 1, CONV_CH), lambda b: (b, 0, 0)),
                  pl.BlockSpec((1, 1, GDN_QK), lambda b: (b, 0, CONV_CH // GDN_QK)),
                  pl.BlockSpec((1, 1, LANES), lambda b: (b, 0, 0)),
                  pl.BlockSpec((1, GDN_HEADS, GDN_DK, GDN_DK), lambda b: (b, 0, 0, 0)),
                  pl.BlockSpec((GDN_CONV, CONV_CH), lambda b: (0, 0)),
                  pl.BlockSpec((8, LANES), lambda b: (0, 0)),
                  pl.BlockSpec((1, GDN_DK), lambda b: (0, 0))],
        out_specs=[pl.BlockSpec((1, 1, GDN_QK), lambda b: (b, 0, 0)),
                   pl.BlockSpec((1, GDN_HEADS, GDN_DK, GDN_DK), lambda b: (b, 0, 0, 0))],
        compiler_params=_params("parallel"),
        name="gdn_step",
    )(az, hist, az, ba, s0, conv_w, par, gn)


DWA_BATCH = 8


def _dwa_blocks(refs, blocks, masks, first, scratch):
    q_ref, k_ref, v_ref = refs
    m_sc, l_sc, acc_sc = scratch
    cur_ok, prev_ok = masks
    scale = DWA_DH ** -0.5
    BL = DWA_BLOCK
    qb = [q_ref[0, cur, :].astype(BF16) for cur, _, _ in blocks]
    sc = [jnp.where(cur_ok, _dot_nt(qb[i], k_ref[0, cur, :].astype(BF16)) * scale, -jnp.inf)
          for i, (cur, _, _) in enumerate(blocks)]
    sp = [None if prev is None else
          jnp.where(prev_ok & on, _dot_nt(qb[i], k_ref[0, prev, :].astype(BF16)) * scale, -jnp.inf)
          for i, (_, prev, on) in enumerate(blocks)]
    m, l, pc, pp = [], [], [], []
    for i in range(len(blocks)):
        mi = jnp.max(sc[i], axis=-1, keepdims=True)
        if sp[i] is not None:
            mi = jnp.maximum(mi, jnp.max(sp[i], axis=-1, keepdims=True))
        p = jnp.exp(sc[i] - mi)
        li = jnp.sum(p, axis=-1, keepdims=True)
        pc.append(p.astype(BF16))
        if sp[i] is not None:
            p = jnp.exp(sp[i] - mi)
            li = li + jnp.sum(p, axis=-1, keepdims=True)
            pp.append(p.astype(BF16))
        else:
            pp.append(None)
        m.append(mi)
        l.append(li)
    acc = [_dot(pc[i], v_ref[0, cur, :].astype(BF16)) for i, (cur, _, _) in enumerate(blocks)]
    acc = [a if pp[i] is None else a + _dot(pp[i], v_ref[0, blocks[i][1], :].astype(BF16))
           for i, a in enumerate(acc)]
    for i, (cur, _, _) in enumerate(blocks):
        if first:
            m_sc[cur, :] = jnp.broadcast_to(m[i], (BL, LANES))
            l_sc[cur, :] = jnp.broadcast_to(l[i], (BL, LANES))
            acc_sc[cur, :] = acc[i]
        else:
            m_old = m_sc[cur, :]
            m_new = jnp.maximum(m_old, m[i])
            a = jnp.exp(m_old - m_new)
            b = jnp.exp(m[i] - m_new)
            m_sc[cur, :] = m_new
            l_sc[cur, :] = a * l_sc[cur, :] + b * l[i]
            acc_sc[cur, :] = a * acc_sc[cur, :] + b * acc[i]


def _dwa_kernel(q0_ref, q1_ref, q2_ref, k0_ref, v0_ref, k1_ref, v1_ref, k2_ref, v2_ref, *rest, n_cast):
    o_ref = rest[n_cast]
    m_sc, l_sc, acc_sc = rest[2 * n_cast + 1:]
    _cast_blocks(rest[:n_cast], rest[n_cast + 1:2 * n_cast + 1])
    BL = DWA_BLOCK
    T = o_ref.shape[1]
    ri = lax.broadcasted_iota(jnp.int32, (BL, BL), 0)
    ci = lax.broadcasted_iota(jnp.int32, (BL, BL), 1)
    masks = (ri >= ci, ci >= ri)
    q_refs, k_refs, v_refs = (q0_ref, q1_ref, q2_ref), (k0_ref, k1_ref, k2_ref), (v0_ref, v1_ref, v2_ref)
    scratch = (m_sc, l_sc, acc_sc)

    for gi, (_, dil) in enumerate(DWA_GROUPS):
        refs = (q_refs[gi], k_refs[gi], v_refs[gi])
        nb = T // dil // BL

        def rows(blk, r, dil=dil):
            start = blk * (BL * dil) + r
            if dil == 1:
                return pl.ds(pl.multiple_of(start, BL), BL)
            return pl.ds(start, BL, stride=dil)

        def block(n, r, nb=nb, rows=rows):
            if nb == 1:
                return rows(n, r), None, None
            return rows(n, r), rows(jnp.maximum(n - 1, 0), r), n > 0

        ns = min(dil, DWA_BATCH)
        nc = min(DWA_BATCH // ns, nb)
        assert dil % ns == 0 and nb % nc == 0
        for r0 in range(0, dil, ns):
            def body(i, carry, r0=r0, refs=refs, block=block, first=gi == 0, ns=ns, nc=nc):
                batch = [block(i * nc + u, r0 + t) for t in range(ns) for u in range(nc)]
                _dwa_blocks(refs, batch, masks, first, scratch)
                return carry
            if nb == nc:
                body(0, 0)
            else:
                lax.fori_loop(0, nb // nc, body, 0)

    o_ref[0] = (acc_sc[...] / l_sc[...]).astype(o_ref.dtype)


def _dwa_prompt(q, kvs, cast=()):
    B, T, _ = q.shape
    c_in, c_out, c_shapes = _cast_jobs(cast, lambda b, h: b * DWA_HEADS + h, B * DWA_HEADS)

    def col(c):
        return pl.BlockSpec((1, T, DWA_DH), lambda b, h: (b, 0, c(h)))

    ng = len(DWA_GROUPS)
    in_specs = [col(lambda h, g=g: g * DWA_HEADS + h) for g in range(ng)]
    args = [q] * ng
    for g in range(ng):
        in_specs += [col(lambda h: h), col(lambda h: DWA_HEADS + h)]
        args += [kvs[g], kvs[g]]
    args += [a for a, *_ in cast]
    res = pl.pallas_call(
        functools.partial(_dwa_kernel, n_cast=len(cast)),
        out_shape=[jax.ShapeDtypeStruct((B, T, DWA_GW), BF16)] + c_shapes,
        grid=(B, DWA_HEADS),
        in_specs=in_specs + c_in,
        out_specs=[col(lambda h: h)] + c_out,
        scratch_shapes=[pltpu.VMEM((T, LANES), F32), pltpu.VMEM((T, LANES), F32), pltpu.VMEM((T, DWA_DH), F32)],
        compiler_params=_params(*(("arbitrary",) * 2 if cast else ("parallel",) * 2)),
        name="dwa_prompt",
    )(*args)
    return res[0], res[1:]


def _dwa_step_kernel(q_ref, n0_ref, n1_ref, n2_ref, c0_ref, c1_ref, c2_ref, o_ref):
    new_refs = (n0_ref, n1_ref, n2_ref)
    cache_refs = (c0_ref, c1_ref, c2_ref)
    scale = DWA_DH ** -0.5
    parts = []
    for gi in range(len(DWA_GROUPS)):
        q = q_ref[0, gi * DWA_HEADS:(gi + 1) * DWA_HEADS, :]
        k = cache_refs[gi][0, :, 0, 0]
        v = cache_refs[gi][0, :, 0, 1]
        s = jnp.sum(k * q[None], axis=-1, keepdims=True) * scale
        s_new = jnp.sum(new_refs[gi][0, 0] * q, axis=-1, keepdims=True) * scale
        m = jnp.maximum(jnp.max(s, axis=0), s_new)
        p = jnp.exp(s - m[None])
        p_new = jnp.exp(s_new - m)
        l = jnp.sum(p, axis=0) + p_new
        acc = jnp.sum(p * v, axis=0) + p_new * new_refs[gi][0, 1]
        parts.append((m, l, acc))
    mm = jnp.maximum(jnp.maximum(parts[0][0], parts[1][0]), parts[2][0])
    num = jnp.zeros((DWA_HEADS, DWA_DH), F32)
    den = jnp.zeros((DWA_HEADS, 1), F32)
    for m, l, acc in parts:
        e = jnp.exp(m - mm)
        num = num + e * acc
        den = den + e * l
    o_ref[0] = num / den


def _dwa_sample(q, kv_new, caches):
    N = q.shape[0]
    span = DWA_BLOCK
    in_specs = [pl.BlockSpec((1, len(DWA_GROUPS) * DWA_HEADS, DWA_DH), lambda b: (b, 0, 0))]
    in_specs += [pl.BlockSpec((1, 2, DWA_HEADS, DWA_DH), lambda b: (b, 0, 0, 0))] * 3
    views = []
    for (win, dil), cache in zip(DWA_GROUPS, caches):
        L = cache.shape[1]
        assert L == win and L // dil == span
        views.append(cache.reshape(N, span, dil, 2, DWA_HEADS, DWA_DH))
        in_specs.append(pl.BlockSpec((1, span, 1, 2, DWA_HEADS, DWA_DH), lambda b: (b, 0, 0, 0, 0, 0)))
    return pl.pallas_call(
        _dwa_step_kernel,
        out_shape=jax.ShapeDtypeStruct((N, DWA_HEADS, DWA_DH), F32),
        grid=(N,),
        in_specs=in_specs,
        out_specs=pl.BlockSpec((1, DWA_HEADS, DWA_DH), lambda b: (b, 0, 0)),
        compiler_params=_params("parallel"),
        name="dwa_step",
    )(q, *kv_new, *views)


def _mix_out_kernel(oa_ref, ob_ref, ga_ref, gb_ref, h_ref, gt_ref, nsh_ref, nsc_ref, ngain_ref,
                    wa_ref, wb_ref, wo_ref, *rest, n_cast):
    out_ref, nxt_ref = rest[n_cast:n_cast + 2]
    _cast_blocks(rest[:n_cast], rest[n_cast + 2:])
    ta = _dot(oa_ref[0], wa_ref[...])
    tb = _dot(ob_ref[0], wb_ref[...])
    merged = _sigmoid(ga_ref[0].astype(F32)) * ta + _sigmoid(gb_ref[0].astype(F32)) * tb
    y = _dot(merged.astype(BF16), wo_ref[...])
    gate, nsc, nsh = gt_ref[0], nsc_ref[0], nsh_ref[0]
    for rows in _row_chunks(y.shape[0]):
        h = h_ref[0, rows, :] + _rows_of(gate, rows) * y[rows]
        out_ref[0, rows, :] = h
        nxt_ref[0, rows, :] = _modulated(h, ngain_ref[...], _rows_of(nsc, rows), _rows_of(nsh, rows))


MIX_TM = 512


def _mix_out(oa, ob, gates, h, mod3, wa, wb, wo, next_gain, cast=()):
    G, R, _ = h.shape
    tm = min(MIX_TM, R)
    c_in, c_out, c_shapes = _cast_jobs(cast, lambda g, i: g * (R // tm) + i, G * (R // tm))

    def rows(width, col=0):
        return pl.BlockSpec((1, tm, width), lambda g, i: (g, i, col))

    def whole(shape):
        return pl.BlockSpec(shape, lambda g, i: (0, 0), pipeline_mode=pl.Buffered(1))

    res = pl.pallas_call(
        functools.partial(_mix_out_kernel, n_cast=len(cast)),
        out_shape=[jax.ShapeDtypeStruct((G, R, D_MODEL), F32), jax.ShapeDtypeStruct((G, R, D_MODEL), BF16)]
        + c_shapes,
        grid=(G, R // tm),
        in_specs=[rows(GDN_QK), rows(DWA_GW), rows(D_MODEL, 0), rows(D_MODEL, 1), rows(D_MODEL),
                  _mod_spec(mod3, R, tm, 5), _mod_spec(mod3, R, tm, 6), _mod_spec(mod3, R, tm, 7),
                  pl.BlockSpec((1, D_MODEL), lambda g, i: (0, 0)),
                  whole(wa.shape), whole(wb.shape), whole(wo.shape)] + c_in,
        out_specs=[rows(D_MODEL), rows(D_MODEL)] + c_out,
        compiler_params=_params(*(("arbitrary",) * 2 if cast else ("parallel",) * 2)),
        name="mix_out",
    )(oa, ob, gates, gates, h, mod3, mod3, mod3, next_gain.reshape(1, D_MODEL), wa, wb, wo,
      *[a for a, *_ in cast])
    return res[0], res[1], res[2:]


_W_IN_COLS = _C_GB + D_MODEL
_W_IN_ALIGNED = _W_IN_COLS // CAST_BLOCK * CAST_BLOCK


def _mixer_weights(wt_f32, wt_bf, a_log, dt_bias):
    w = wt_bf
    order = np.array(_HEAD_ORDER)
    ba_rows = np.concatenate([_C_B + order, _C_A + order])
    par = jnp.zeros((8, LANES), F32)
    par = par.at[0, 8:16].set(a_log[0][order]).at[1, 8:16].set(dt_bias[0][order])
    return dict(
        az=w,
        ba=jnp.pad(w[ba_rows], ((0, LANES - 2 * GDN_HEADS), (0, 0))),
        attn=jnp.concatenate(
            [w[c + g * DWA_GW:c + (g + 1) * DWA_GW] for g in range(len(DWA_GROUPS)) for c in (_C_K, _C_V)]
            + [w[_C_Q:_C_K]], axis=0),
        gates=jnp.concatenate([w[_C_GA:], wt_f32[_W_IN_ALIGNED:].astype(BF16)], axis=0),
        par=par,
    )


def _trunk(h1, xm, mod3, pos, W, norms, ffn2, *, sample_state=None):
    G, R, _ = h1.shape
    up2, down2 = ffn2
    tables = _rope_tables(pos)
    ba = _proj(xm, W["ba"], F32, tn=LANES, name="proj_ba")
    gates = _proj(xm, W["gates"], BF16, tn=2048, name="proj_gates")
    kvs, q = _attn_proj(xm, W["attn"], tables)
    if sample_state is None:
        az, conv_rows = _proj(xm, W["az"], BF16, n=_C_B, tn=2048, tail=True, name="proj_az")
        o_a, s_new = _gdn_prompt(az, ba, W["conv"], W["par"], W["gn"])
        o_b, (up2,) = _dwa_prompt(q, kvs, cast=((up2, 1, 2 * D_FF, 4 * CAST_BLOCK),))
        down_cast = ((down2, 0, D_FF, 2 * CAST_BLOCK),)
    else:
        hist, s0, caches = sample_state
        conv_rows = _proj(xm, W["az"], F32, n=_C_B, tn=2048, name="proj_az").reshape(R, 1, -1)
        o_a, s_new = _gdn_sample(conv_rows, ba.reshape(R, 1, LANES), hist, s0, W["conv"], W["par"], W["gn"])
        o_b = _dwa_sample(q.reshape(R, len(DWA_GROUPS) * DWA_HEADS, DWA_DH),
                          [kv.reshape(R, 2, DWA_HEADS, DWA_DH) for kv in kvs], caches)
        o_a = o_a.reshape(1, R, -1)
        o_b = o_b.reshape(1, R, DWA_GW).astype(BF16)
        down_cast = ()
    h2, xm2, cast_out = _mix_out(o_a, o_b, gates, h1, mod3, W["wa"], W["wb"], W["wo"], norms["ffn2"],
                                 cast=down_cast)
    if down_cast:
        (down2,) = cast_out
    y = _ffn_last(h2, xm2, mod3, up2, down2, norms["final"])
    return y, conv_rows[:, :, :CONV_CH], s_new, kvs, (up2, down2)


def kernel(x_prompt, x_sample, state_conv, state_delta, cache_kv_w128, cache_kv_w512, cache_kv_w2048, c_prompt, c_sample, w_ada, b_ada, norm_ffn1, w_ffn1_up, w_ffn1_down, norm_mix, w_in, conv_w, a_log, dt_bias, gdn_norm, w_proj_a, w_proj_b, w_out, norm_ffn2, w_ffn2_up, w_ffn2_down, norm_final):
    B, T, _ = x_prompt.shape
    N, S, _ = x_sample.shape
    assert S == 1 and T % (DWA_BLOCK * DWA_GROUPS[-1][1]) == 0
    norms = dict(ffn1=norm_ffn1[0], mix=norm_mix[0], ffn2=norm_ffn2[0], final=norm_final)

    mod = _ada(jnp.concatenate([c_prompt, c_sample], axis=0), w_ada[0], b_ada)
    mod_p = mod[:B].reshape(B, 1, N_MOD * D_MODEL)
    mod_s = mod[B:].reshape(1, N, N_MOD * D_MODEL)

    up1, down1 = w_ffn1_up[0].astype(BF16), w_ffn1_down[0].astype(BF16)
    w_in_t = jnp.swapaxes(w_in[0], 0, 1)
    h1_p, xm_p, w_in_bf = _ffn_first(x_prompt, mod_p, norms["ffn1"], up1, down1, norms["mix"],
                                     cast=((w_in_t, 0, _W_IN_ALIGNED, CAST_BLOCK),))
    W = _mixer_weights(w_in_t, w_in_bf, a_log, dt_bias)
    W.update(conv=conv_w[0], gn=gdn_norm,
             wa=w_proj_a[0].astype(BF16), wb=w_proj_b[0].astype(BF16), wo=w_out[0].astype(BF16))
    h1_s, xm_s = _ffn_first(x_sample.reshape(1, N, D_MODEL), mod_s, norms["ffn1"], up1, down1, norms["mix"])

    y_p, rows_p, s_p, kv_p, ffn2 = _trunk(h1_p, xm_p, mod_p, jnp.arange(T, dtype=jnp.int32), W, norms,
                                          (w_ffn2_up[0], w_ffn2_down[0]))
    caches = (cache_kv_w128[0], cache_kv_w512[0], cache_kv_w2048[0])
    y_s, rows_s, s_s, kv_s, _ = _trunk(h1_s, xm_s, mod_s, jnp.full((N,), PAST_LEN, dtype=jnp.int32), W, norms,
                                       ffn2, sample_state=(state_conv[0], state_delta[0], caches))

    keep = GDN_CONV - 1
    conv_p = rows_p[:, TAIL_ROWS - keep:][None]
    conv_s = jnp.concatenate([state_conv[0], rows_s], axis=1)[:, -keep:][None]
    kv_out_p = []
    for (win, _), kv in zip(DWA_GROUPS, kv_p):
        k = min(win, T)
        kv_out_p.append(kv[:, T - k:].reshape(1, B, k, 2, DWA_HEADS, DWA_DH))
    kv_out_s = [kv.reshape(1, N, 1, 2, DWA_HEADS, DWA_DH) for kv in kv_s]
    return (y_p, y_s.reshape(N, 1, D_MODEL), conv_p, s_p[None], *kv_out_p,
            conv_s, s_s[None], *kv_out_s)
```

```python
import functools

import jax
import jax.numpy as jnp
import numpy as np
from jax import lax
from jax.experimental import pallas as pl
from jax.experimental.pallas import tpu as pltpu

F32 = jnp.float32
BF16 = jnp.bfloat16

D_MODEL = 2048
D_FF = 5632
N_MOD = 9
NORM_EPS = 1e-6
PAST_LEN = 16384

GDN_HEADS = 8
GDN_DK = 128
GDN_CONV = 4
GDN_CHUNK = 64
GDN_QK = GDN_HEADS * GDN_DK
CONV_CH = 3 * GDN_QK

DWA_GROUPS = ((128, 1), (512, 4), (2048, 16))
DWA_HEADS = 4
DWA_DH = 128
DWA_GW = DWA_HEADS * DWA_DH
DWA_WIDTH = len(DWA_GROUPS) * DWA_GW
DWA_BLOCK = 128
ROPE_THETA = 10000.0

LANES = 128
MXU_COLS = 256
VMEM_LIMIT = 56 * 1024 * 1024

_C_Z = CONV_CH
_C_B = _C_Z + GDN_QK
_C_A = _C_B + GDN_HEADS
_C_Q = _C_A + GDN_HEADS
_C_K = _C_Q + DWA_WIDTH
_C_V = _C_K + DWA_WIDTH
_C_GA = _C_V + DWA_WIDTH
_C_GB = _C_GA + D_MODEL

_HEAD_ORDER = (0, 2, 4, 6, 1, 3, 5, 7)
_N_PAIRS = GDN_HEADS // 2


def _sigmoid(x):
    return 0.5 * jnp.tanh(0.5 * x) + 0.5


def _silu(x):
    return x * _sigmoid(x)


def _dot(a, b):
    return jnp.dot(a, b, preferred_element_type=F32)


def _dot_nt(a, b):
    return lax.dot_general(a, b, (((1,), (1,)), ((), ())), preferred_element_type=F32)


def _params(*sem):
    return pltpu.CompilerParams(dimension_semantics=sem, vmem_limit_bytes=VMEM_LIMIT)


def _ada_kernel(c_ref, w_ref, b_ref, o_ref):
    a = _silu(c_ref[...]).astype(BF16)
    o_ref[...] = _dot(a, w_ref[...].astype(BF16)) + b_ref[...]


def _ada(c, w, b):
    m, n = c.shape[0], w.shape[1]
    tn = 1024
    return pl.pallas_call(
        _ada_kernel,
        out_shape=jax.ShapeDtypeStruct((m, n), F32),
        grid=(n // tn,),
        in_specs=[pl.BlockSpec((m, D_MODEL), lambda j: (0, 0)),
                  pl.BlockSpec((D_MODEL, tn), lambda j: (0, j)),
                  pl.BlockSpec((1, tn), lambda j: (0, j))],
        out_specs=pl.BlockSpec((m, tn), lambda j: (0, j)),
        compiler_params=_params("arbitrary"),
        name="ada_mod",
    )(c, w, b)


def _rms(x):
    return x * lax.rsqrt(jnp.mean(x * x, axis=-1, keepdims=True) + NORM_EPS)


def _modulated(h, gain, scale, shift):
    return ((_rms(h) * gain) * (1.0 + scale) + shift).astype(BF16)


ROW_CHUNK = 32


def _row_chunks(n_rows):
    return [slice(r, min(r + ROW_CHUNK, n_rows)) for r in range(0, n_rows, ROW_CHUNK)]


def _rows_of(v, rows):
    return v if v.shape[0] == 1 else v[rows]


def _swiglu_step(xm, wg_ref, wu_ref, wd_ref, acc_sc):
    g = _dot(xm, wg_ref[...])
    u = _dot(xm, wu_ref[...])
    acc_sc[...] += _dot((_silu(g) * u).astype(BF16), wd_ref[...])


def _ffn_first_kernel(x_ref, sh_ref, sc_ref, gt_ref, gain_ref, wg_ref, wu_ref, wd_ref,
                      nsh_ref, nsc_ref, ngain_ref, *rest, n_cast):
    cast_in = rest[:n_cast]
    out_ref, nxt_ref = rest[n_cast:n_cast + 2]
    cast_out = rest[n_cast + 2:2 * n_cast + 2]
    xm_sc, acc_sc = rest[2 * n_cast + 2:]
    j = pl.program_id(2)

    tm = x_ref.shape[1]

    @pl.when(j == 0)
    def _():
        sc, sh = sc_ref[0], sh_ref[0]
        for rows in _row_chunks(tm):
            xm_sc[rows, :] = _modulated(x_ref[0, rows, :], gain_ref[...], _rows_of(sc, rows), _rows_of(sh, rows))
        acc_sc[...] = jnp.zeros_like(acc_sc)

    _cast_blocks(cast_in, cast_out)
    _swiglu_step(xm_sc[...], wg_ref, wu_ref, wd_ref, acc_sc)

    @pl.when(j == pl.num_programs(2) - 1)
    def _():
        half_gate, nsc, nsh = 0.5 * gt_ref[0], nsc_ref[0], nsh_ref[0]
        for rows in _row_chunks(tm):
            h = x_ref[0, rows, :] + _rows_of(half_gate, rows) * acc_sc[rows, :]
            out_ref[0, rows, :] = h
            nxt_ref[0, rows, :] = _modulated(h, ngain_ref[...], _rows_of(nsc, rows), _rows_of(nsh, rows))


def _ffn_last_kernel(x_ref, xm_ref, gt_ref, wg_ref, wu_ref, wd_ref, ngain_ref, out_ref, acc_sc):
    j = pl.program_id(2)

    @pl.when(j == 0)
    def _():
        acc_sc[...] = jnp.zeros_like(acc_sc)

    _swiglu_step(xm_ref[0], wg_ref, wu_ref, wd_ref, acc_sc)

    @pl.when(j == pl.num_programs(2) - 1)
    def _():
        half_gate = 0.5 * gt_ref[0]
        for rows in _row_chunks(x_ref.shape[1]):
            h = x_ref[0, rows, :] + _rows_of(half_gate, rows) * acc_sc[rows, :]
            out_ref[0, rows, :] = _rms(h) * ngain_ref[...]


FFN_TM = 512
FFN_TF = 512


def _mod_spec(mod3, R, tm, k):
    if mod3.shape[1] == R:
        return pl.BlockSpec((1, tm, D_MODEL), lambda g, i, *_: (g, i, k))
    return pl.BlockSpec((1, 1, D_MODEL), lambda g, i, *_: (g, 0, k))


def _ffn_specs(R):
    tm = min(FFN_TM, R)
    nff = D_FF // FFN_TF
    rows = pl.BlockSpec((1, tm, D_MODEL), lambda g, i, j: (g, i, 0))
    vec = pl.BlockSpec((1, D_MODEL), lambda g, i, j: (0, 0))
    weights = [pl.BlockSpec((D_MODEL, FFN_TF), lambda g, i, j: (0, j)),
               pl.BlockSpec((D_MODEL, FFN_TF), lambda g, i, j: (0, j + nff)),
               pl.BlockSpec((FFN_TF, D_MODEL), lambda g, i, j: (j, 0))]
    return tm, nff, rows, vec, weights


CAST_BLOCK = 128


def _cast_jobs(arrays, step_of, total_steps):
    in_specs, out_specs, out_shapes = [], [], []
    start = 0
    for arr, axis, extent, slab in arrays:
        assert extent % slab == 0 and slab % CAST_BLOCK == 0
        nblk = extent // slab
        shape = tuple(slab if d == axis else n for d, n in enumerate(arr.shape))

        def index(*grid, start=start, nblk=nblk, axis=axis):
            blk = jnp.clip(step_of(*grid) - start, 0, nblk - 1)
            return (blk, 0) if axis == 0 else (0, blk)

        in_specs.append(pl.BlockSpec(shape, index))
        out_specs.append(pl.BlockSpec(shape, index))
        out_shapes.append(jax.ShapeDtypeStruct(
            tuple(extent if d == axis else n for d, n in enumerate(arr.shape)), BF16))
        start += nblk
    assert start <= total_steps, (start, total_steps)
    return in_specs, out_specs, out_shapes


def _cast_blocks(cast_in, cast_out):
    for src, dst in zip(cast_in, cast_out):
        dst[...] = src[...].astype(BF16)


def _ffn_first(x3, mod3, gain, w_up, w_down, next_gain, cast=()):
    G, R, _ = x3.shape
    tm, nff, rows, vec, weights = _ffn_specs(R)
    ms = [_mod_spec(mod3, R, tm, k) for k in range(5)]
    c_in, c_out, c_shapes = _cast_jobs(cast, lambda g, i, j: (g * (R // tm) + i) * nff + j, G * (R // tm) * nff)
    sem = ("arbitrary",) * 3 if cast else ("parallel", "parallel", "arbitrary")
    return pl.pallas_call(
        functools.partial(_ffn_first_kernel, n_cast=len(cast)),
        out_shape=[jax.ShapeDtypeStruct((G, R, D_MODEL), F32), jax.ShapeDtypeStruct((G, R, D_MODEL), BF16)]
        + c_shapes,
        grid=(G, R // tm, nff),
        in_specs=[rows, ms[0], ms[1], ms[2], vec] + weights + [ms[3], ms[4], vec] + c_in,
        out_specs=[rows, rows] + c_out,
        scratch_shapes=[pltpu.VMEM((tm, D_MODEL), BF16), pltpu.VMEM((tm, D_MODEL), F32)],
        compiler_params=_params(*sem),
        name="ffn_first",
    )(x3, mod3, mod3, mod3, gain.reshape(1, D_MODEL), w_up, w_up, w_down, mod3, mod3,
      next_gain.reshape(1, D_MODEL), *[a for a, *_ in cast])


def _ffn_last(x3, xm3, mod3, w_up, w_down, final_gain):
    G, R, _ = x3.shape
    tm, nff, rows, vec, weights = _ffn_specs(R)
    return pl.pallas_call(
        _ffn_last_kernel,
        out_shape=jax.ShapeDtypeStruct((G, R, D_MODEL), F32),
        grid=(G, R // tm, nff),
        in_specs=[rows, rows, _mod_spec(mod3, R, tm, 8)] + weights + [vec],
        out_specs=rows,
        scratch_shapes=[pltpu.VMEM((tm, D_MODEL), F32)],
        compiler_params=_params("parallel", "parallel", "arbitrary"),
        name="ffn_last",
    )(x3, xm3, mod3, w_up, w_up, w_down, final_gain.reshape(1, D_MODEL))


def _rope_tile(x, cosf, sinf):
    return x * cosf + pltpu.roll(x, DWA_DH // 2, axis=1) * sinf


TAIL_ROWS = 8


def _proj_kernel(a_ref, wt_ref, *rest, tail):
    acc = _dot_nt(a_ref[0], wt_ref[...])
    rest[0][0] = acc.astype(rest[0].dtype)
    if tail:
        rest[1][0] = acc[acc.shape[0] - TAIL_ROWS:]


PROJ_TM = 1024


def _proj(xm3, wt, out_dtype, *, n=None, tn=512, tail=False, name="proj"):
    G, R, _ = xm3.shape
    n = wt.shape[0] if n is None else n
    tm = min(PROJ_TM, R)
    out_shape = jax.ShapeDtypeStruct((G, R, n), out_dtype)
    out_specs = pl.BlockSpec((1, tm, tn), lambda g, i, j: (g, i, j))
    if tail:
        out_shape = [out_shape, jax.ShapeDtypeStruct((G, R // tm * TAIL_ROWS, n), F32)]
        out_specs = [out_specs, pl.BlockSpec((1, TAIL_ROWS, tn), lambda g, i, j: (g, i, j))]
    res = pl.pallas_call(
        functools.partial(_proj_kernel, tail=tail),
        out_shape=out_shape,
        grid=(G, R // tm, n // tn),
        in_specs=[pl.BlockSpec((1, tm, D_MODEL), lambda g, i, j: (g, i, 0)),
                  pl.BlockSpec((tn, D_MODEL), lambda g, i, j: (j, 0))],
        out_specs=out_specs,
        compiler_params=_params("parallel", "parallel", "arbitrary"),
        name=name,
    )(xm3, wt)
    if tail:
        return res[0], res[1][:, -TAIL_ROWS:]
    return res


ATTN_TILES = tuple((2 * g, 2 * g + 2) for g in range(len(DWA_GROUPS))) + ((2 * len(DWA_GROUPS), 3 * len(DWA_GROUPS)),)


def _attn_proj_kernel(a_ref, wt_ref, cos_ref, sin_ref, *o_refs):
    j = pl.program_id(2)

    def tile(o_ref, rotate):
        a = a_ref[0]
        for c in range(0, DWA_GW, MXU_COLS):
            acc = _dot_nt(a, wt_ref[c:c + MXU_COLS, :])
            for h in range(0, MXU_COLS, DWA_DH):
                x = acc[:, h:h + DWA_DH]
                o_ref[0, :, c + h:c + h + DWA_DH] = _rope_tile(x, cos_ref[...], sin_ref[...]) if rotate else x

    for o_ref, (lo, hi) in zip(o_refs[:-1], ATTN_TILES[:-1]):
        pl.when(j == lo)(functools.partial(tile, o_ref, True))
        pl.when(j == lo + 1)(functools.partial(tile, o_ref, False))
    pl.when(j >= ATTN_TILES[-1][0])(functools.partial(tile, o_refs[-1], True))


def _attn_proj(xm3, wt, tables):
    G, R, _ = xm3.shape
    tm = min(PROJ_TM, R)
    tab = pl.BlockSpec((tm, DWA_DH), lambda g, i, j: (i, 0))
    out_shape, out_specs = [], []
    for lo, hi in ATTN_TILES:
        out_shape.append(jax.ShapeDtypeStruct((G, R, (hi - lo) * DWA_GW), F32))
        out_specs.append(pl.BlockSpec((1, tm, DWA_GW),
                                      lambda g, i, j, lo=lo, hi=hi: (g, i, jnp.clip(j - lo, 0, hi - lo - 1))))
    res = pl.pallas_call(
        _attn_proj_kernel,
        out_shape=out_shape,
        grid=(G, R // tm, ATTN_TILES[-1][1]),
        in_specs=[pl.BlockSpec((1, tm, D_MODEL), lambda g, i, j: (g, i, 0)),
                  pl.BlockSpec((DWA_GW, D_MODEL), lambda g, i, j: (j, 0)), tab, tab],
        out_specs=out_specs,
        compiler_params=_params("parallel", "arbitrary", "arbitrary"),
        name="proj_attn",
    )(xm3, wt, *tables)
    return res[:-1], res[-1]


def _rope_tables(pos):
    half = DWA_DH // 2
    inv = jnp.power(ROPE_THETA, -jnp.arange(half, dtype=F32) / half)
    ang = pos.astype(F32)[:, None] * inv[None, :]
    cos, sin = jnp.cos(ang), jnp.sin(ang)
    return jnp.concatenate([cos, cos], axis=-1), jnp.concatenate([-sin, sin], axis=-1)


def _gdn_gates(ba, par):
    beta = _sigmoid(ba)
    x = ba + par[1:2]
    softplus = jnp.maximum(x, 0.0) + jnp.log1p(jnp.exp(-jnp.abs(x)))
    return beta, -jnp.exp(par[0:1]) * softplus


def _gated_out(o, z, gn):
    return (_rms(o) * gn) * _silu(z.astype(F32))


def _l2norm(t):
    return t * lax.rsqrt(jnp.sum(t * t, axis=-1, keepdims=True) + NORM_EPS)


GDN_HIST = 16
GDN_STEP_CHUNKS = 4


def _gdn_frontend(hist, raw, cw_ref, buf_sc, x_out):
    C = GDN_CHUNK
    buf_sc[0:GDN_HIST, :] = hist
    buf_sc[GDN_HIST:GDN_HIST + C, :] = raw
    taps = GDN_CONV - 1
    ri = lax.broadcasted_iota(jnp.int32, (taps * C, buf_sc.shape[0]), 0)
    ci = lax.broadcasted_iota(jnp.int32, (taps * C, buf_sc.shape[0]), 1)
    shift = jnp.where(ci == GDN_HIST + ri % C - (ri // C + 1), 1.0, 0.0).astype(BF16)
    back = _dot(shift, buf_sc[...])
    acc = raw.astype(F32) * cw_ref[taps:taps + 1, :]
    for s in range(taps):
        acc = acc + back[s * C:(s + 1) * C] * cw_ref[taps - 1 - s:taps - s, :]
    x = _silu(acc)
    for h in range(2 * GDN_HEADS):
        sl = slice(h * GDN_DK, (h + 1) * GDN_DK)
        x_out[:, sl] = _l2norm(x[:, sl]) * (GDN_DK ** -0.5 if h < GDN_HEADS else 1.0)
    x_out[:, 2 * GDN_QK:] = x[:, 2 * GDN_QK:]


def _gdn_chunk_kernel(cur_ref, nxt_ref, z_ref, ba_ref, cw_ref, par_ref, gn_ref, o_ref, s_ref,
                      buf_sc, xa_sc, xb_sc, S_sc):
    C = GDN_CHUNK
    c = pl.program_id(1)
    x_bufs = (xa_sc, xb_sc)

    @pl.when(c == 0)
    def _():
        S_sc[...] = jnp.zeros_like(S_sc)
        buf_sc[...] = jnp.zeros_like(buf_sc)
        _gdn_frontend(jnp.zeros((GDN_HIST, CONV_CH), BF16), cur_ref[0, 0:C, :], cw_ref, buf_sc, xa_sc)

    refs = (z_ref, ba_ref, par_ref, gn_ref, o_ref, S_sc)
    for k in range(GDN_STEP_CHUNKS):
        last = k == GDN_STEP_CHUNKS - 1
        following = nxt_ref[0] if last else cur_ref[0, (k + 1) * C:(k + 2) * C, :]
        _gdn_frontend(cur_ref[0, (k + 1) * C - GDN_HIST:(k + 1) * C, :], following, cw_ref, buf_sc,
                      x_bufs[(k + 1) % 2])
        _gdn_chain(x_bufs[k % 2], k * C, *refs)

    @pl.when(c == pl.num_programs(1) - 1)
    def _():
        s_ref[0] = S_sc[...]


def _gdn_chain(x_sc, r0, z_ref, ba_ref, par_ref, gn_ref, o_ref, S_sc):
    C = GDN_CHUNK
    beta_t, g_t = _gdn_gates(ba_ref[0, r0:r0 + C, :], par_ref[...])
    row = lax.broadcasted_iota(jnp.int32, (C, LANES), 0)
    G = g_t
    s = 1
    while s < C:
        G = G + jnp.where(row >= s, pltpu.roll(G, s, axis=0), 0.0)
        s *= 2
    shift = LANES - _N_PAIRS
    Gs = jnp.concatenate([G, pltpu.roll(G, shift, axis=1)], axis=0)
    Bs = jnp.concatenate([beta_t, pltpu.roll(beta_t, shift, axis=1)], axis=0)
    GT = Gs.T
    r2 = lax.broadcasted_iota(jnp.int32, (2 * C, LANES), 0)
    Glast = jnp.where(r2 < C, Gs[C - 1:C, :], Gs[2 * C - 1:2 * C, :])
    eG = jnp.exp(Gs)
    eGl = jnp.exp(Glast - Gs)
    egl = jnp.exp(Glast)

    ii = lax.broadcasted_iota(jnp.int32, (2 * C, 2 * C), 0)
    jj = lax.broadcasted_iota(jnp.int32, (2 * C, 2 * C), 1)
    same = (ii // C) == (jj // C)
    strict = same & (ii > jj)
    diag = ii == jj
    blk = (ii // 16) == (jj // 16)
    top = r2 < C
    gn = gn_ref[...]

    P = range(_N_PAIRS)
    C2 = 2 * C

    def pair(p, off):
        a = x_sc[:, off + (2 * p) * GDN_DK: off + (2 * p + 1) * GDN_DK]
        b = x_sc[:, off + (2 * p + 1) * GDN_DK: off + (2 * p + 2) * GDN_DK]
        return jnp.concatenate([a, b], axis=0)

    def col(t, p, base=0):
        return t[:, base + p:base + p + 1]

    q2 = [pair(p, 0) for p in P]
    k2 = [pair(p, GDN_QK) for p in P]
    kb = [k2[p] * col(Bs, p) for p in P]
    vb = [pair(p, 2 * GDN_QK) * col(Bs, p) for p in P]
    dec = [jnp.where(strict, jnp.exp(jnp.where(strict, col(Gs, p, 8) - GT[8 + p:9 + p, :], 0.0)), 0.0) for p in P]
    kq = [_dot_nt(jnp.concatenate([kb[p], q2[p]], axis=0).astype(BF16), k2[p].astype(BF16)) for p in P]
    A = [kq[p][:C2] * dec[p] for p in P]
    qk = [(kq[p][C2:] * (dec[p] + jnp.where(diag, 1.0, 0.0))).astype(BF16) for p in P]

    Dg = [jnp.where(blk, A[p], 0.0) for p in P]
    E = [A[p] - Dg[p] for p in P]
    Q = [-Dg[p] for p in P]
    Dgb = [Dg[p].astype(BF16) for p in P]
    Dp = [_dot(Dgb[p], Dgb[p]) for p in P]
    for _ in range(2):
        Dpb = [Dp[p].astype(BF16) for p in P]
        st = [_dot(jnp.concatenate([Dpb[p], Q[p].astype(BF16)], axis=0), Dpb[p]) for p in P]
        Q = [Q[p] + Dp[p] + st[p][C2:] for p in P]
        Dp = [st[p][:C2] for p in P]
    Q = [Q[p] + Dp[p] + _dot(Q[p].astype(BF16), Dp[p].astype(BF16)) for p in P]
    rhs = [jnp.concatenate([kb[p] * col(eG, p, 8), vb[p]], axis=1) for p in P]
    er = [_dot(Q[p].astype(BF16), jnp.concatenate([E[p], rhs[p]], axis=1).astype(BF16)) for p in P]
    N = [E[p] + er[p][:, :C2] for p in P]
    y = [rhs[p] + er[p][:, C2:] for p in P]
    Nb = [N[p].astype(BF16) for p in P]
    ny = [_dot(Nb[p], jnp.concatenate([N[p], y[p]], axis=1).astype(BF16)) for p in P]
    zz = [y[p] - ny[p][:, C2:] for p in P]
    sol = [zz[p] + _dot(ny[p][:, :C2].astype(BF16), zz[p].astype(BF16)) for p in P]

    wq = [jnp.concatenate([sol[p][:, :GDN_DK], q2[p] * col(eG, p, 8)], axis=0).astype(BF16) for p in P]
    S_old = [S_sc[h] for h in range(GDN_HEADS)]
    rS = [_dot(wq[h // 2], S_old[h].astype(BF16)) for h in range(GDN_HEADS)]
    vn = [(sol[p][:, GDN_DK:] - jnp.where(top, rS[2 * p][:C2], rS[2 * p + 1][:C2])).astype(BF16) for p in P]
    kdT = [(k2[p] * col(eGl, p, 8)).T.astype(BF16) for p in P]
    zero = jnp.zeros((C2, GDN_DK), BF16)
    for p in P:
        o2 = jnp.where(top, rS[2 * p][C2:], rS[2 * p + 1][C2:]) + _dot(qk[p], vn[p])
        S_sc[2 * p] = S_old[2 * p] * egl[0:1, 8 + p:9 + p] + _dot(kdT[p], jnp.where(top, vn[p], zero))
        S_sc[2 * p + 1] = S_old[2 * p + 1] * egl[C:C + 1, 8 + p:9 + p] + _dot(kdT[p], jnp.where(top, zero, vn[p]))
        for e in range(2):
            h = 2 * p + e
            sl = slice(h * GDN_DK, (h + 1) * GDN_DK)
            o_ref[0, r0:r0 + C, sl] = _gated_out(o2[e * C:(e + 1) * C], z_ref[0, r0:r0 + C, sl],
                                                 gn).astype(o_ref.dtype)


def _gdn_prompt(az, ba, conv_w, par, gn):
    B, T, _ = az.shape
    C = GDN_CHUNK
    R = GDN_STEP_CHUNKS * C
    assert GDN_STEP_CHUNKS % 2 == 0 and T % R == 0
    return pl.pallas_call(
        _gdn_chunk_kernel,
        out_shape=[jax.ShapeDtypeStruct((B, T, GDN_QK), BF16),
                   jax.ShapeDtypeStruct((B, GDN_HEADS, GDN_DK, GDN_DK), F32)],
        grid=(B, T // R),
        in_specs=[pl.BlockSpec((1, R, CONV_CH), lambda b, c: (b, c, 0)),
                  pl.BlockSpec((1, C, CONV_CH),
                               lambda b, c: (b, jnp.minimum((c + 1) * GDN_STEP_CHUNKS, T // C - 1), 0)),
                  pl.BlockSpec((1, R, GDN_QK), lambda b, c: (b, c, CONV_CH // GDN_QK)),
                  pl.BlockSpec((1, R, LANES), lambda b, c: (b, c, 0)),
                  pl.BlockSpec((GDN_CONV, CONV_CH), lambda b, c: (0, 0)),
                  pl.BlockSpec((8, LANES), lambda b, c: (0, 0)),
                  pl.BlockSpec((1, GDN_DK), lambda b, c: (0, 0))],
        out_specs=[pl.BlockSpec((1, R, GDN_QK), lambda b, c: (b, c, 0)),
                   pl.BlockSpec((1, GDN_HEADS, GDN_DK, GDN_DK), lambda b, c: (b, 0, 0, 0))],
        scratch_shapes=[pltpu.VMEM((2 * C, CONV_CH), BF16), pltpu.VMEM((C, CONV_CH), F32),
                        pltpu.VMEM((C, CONV_CH), F32), pltpu.VMEM((GDN_HEADS, GDN_DK, GDN_DK), F32)],
        compiler_params=_params("parallel", "arbitrary"),
        name="gdn_chunked",
    )(az, az, az, ba, conv_w, par, gn)


def _gdn_step_kernel(qkv_ref, hist_ref, z_ref, ba_ref, s0_ref, cw_ref, par_ref, gn_ref, o_ref, s_ref):
    acc = qkv_ref[0] * cw_ref[GDN_CONV - 1:GDN_CONV, :]
    for j in range(GDN_CONV - 1):
        acc = acc + hist_ref[0, j:j + 1, :] * cw_ref[j:j + 1, :]
    x = _silu(acc)
    beta_t, g_t = _gdn_gates(ba_ref[0], par_ref[...])
    eg_t = jnp.exp(g_t)

    def head(off, h):
        return x[:, off + h * GDN_DK: off + (h + 1) * GDN_DK]

    rows = []
    for off, scale in ((GDN_QK, 1.0), (0, GDN_DK ** -0.5)):
        for h in range(GDN_HEADS):
            t = head(off, h)
            rows.append(t * lax.rsqrt(jnp.sum(t * t, axis=-1, keepdims=True) + NORM_EPS) * scale)
    ri = lax.broadcasted_iota(jnp.int32, (LANES, LANES), 0)
    tile = jnp.zeros((LANES, LANES), F32)
    for r, t in enumerate(rows):
        tile = jnp.where(ri == r, t, tile)
    cols = tile.T
    gn = gn_ref[...]
    for h in range(GDN_HEADS):
        lane = _HEAD_ORDER.index(h)
        kcol = cols[:, h:h + 1]
        qcol = cols[:, GDN_HEADS + h:GDN_HEADS + h + 1]
        S = s0_ref[0, h] * eg_t[:, 8 + lane:9 + lane]
        kv = jnp.sum(kcol * S, axis=0, keepdims=True)
        delta = beta_t[:, lane:lane + 1] * (head(2 * GDN_QK, h) - kv)
        S = S + kcol * delta
        s_ref[0, h] = S
        o = jnp.sum(qcol * S, axis=0, keepdims=True)
        sl = slice(h * GDN_DK, (h + 1) * GDN_DK)
        o_ref[0, :, sl] = _gated_out(o, z_ref[0, :, sl], gn).astype(o_ref.dtype)


def _gdn_sample(az, ba, hist, s0, conv_w, par, gn):
    N = az.shape[0]
    return pl.pallas_call(
        _gdn_step_kernel,
        out_shape=[jax.ShapeDtypeStruct((N, 1, GDN_QK), BF16),
                   jax.ShapeDtypeStruct((N, GDN_HEADS, GDN_DK, GDN_DK), F32)],
        grid=(N,),
        in_specs=[pl.BlockSpec((1, 1, CONV_CH), lambda b: (b, 0, 0)),
                  pl.BlockSpec((1, GDN_CONV - 1, CONV_CH), lambda b: (b, 0, 0)),
                  pl.BlockSpec((1, 1, GDN_QK), lambda b: (b, 0, CONV_CH // GDN_QK)),
                  pl.BlockSpec((1, 1, LANES), lambda b: (b, 0, 0)),
                  pl.BlockSpec((1, GDN_HEADS, GDN_DK, GDN_DK), lambda b: (b, 0, 0, 0)),
                  pl.BlockSpec((GDN_CONV, CONV_CH), lambda b: (0, 0)),
                  pl.BlockSpec((8, LANES), lambda b: (0, 0)),
                  pl.BlockSpec((1, GDN_DK), lambda b: (0, 0))],
        out_specs=[pl.BlockSpec((1, 1, GDN_QK), lambda b: (b, 0, 0)),
                   pl.BlockSpec((1, GDN_HEADS, GDN_DK, GDN_DK), lambda b: (b, 0, 0, 0))],
        compiler_params=_params("parallel"),
        name="gdn_step",
    )(az, hist, az, ba, s0, conv_w, par, gn)


DWA_BATCH = 8


def _dwa_blocks(refs, blocks, masks, first, scratch):
    q_ref, k_ref, v_ref = refs
    m_sc, l_sc, acc_sc = scratch
    cur_ok, prev_ok = masks
    scale = DWA_DH ** -0.5
    BL = DWA_BLOCK
    qb = [q_ref[0, cur, :].astype(BF16) for cur, _, _ in blocks]
    sc = [jnp.where(cur_ok, _dot_nt(qb[i], k_ref[0, cur, :].astype(BF16)) * scale, -jnp.inf)
          for i, (cur, _, _) in enumerate(blocks)]
    sp = [None if prev is None else
          jnp.where(prev_ok & on, _dot_nt(qb[i], k_ref[0, prev, :].astype(BF16)) * scale, -jnp.inf)
          for i, (_, prev, on) in enumerate(blocks)]
    m, l, pc, pp = [], [], [], []
    for i in range(len(blocks)):
        mi = jnp.max(sc[i], axis=-1, keepdims=True)
        if sp[i] is not None:
            mi = jnp.maximum(mi, jnp.max(sp[i], axis=-1, keepdims=True))
        p = jnp.exp(sc[i] - mi)
        li = jnp.sum(p, axis=-1, keepdims=True)
        pc.append(p.astype(BF16))
        if sp[i] is not None:
            p = jnp.exp(sp[i] - mi)
            li = li + jnp.sum(p, axis=-1, keepdims=True)
            pp.append(p.astype(BF16))
        else:
            pp.append(None)
        m.append(mi)
        l.append(li)
    acc = [_dot(pc[i], v_ref[0, cur, :].astype(BF16)) for i, (cur, _, _) in enumerate(blocks)]
    acc = [a if pp[i] is None else a + _dot(pp[i], v_ref[0, blocks[i][1], :].astype(BF16))
           for i, a in enumerate(acc)]
    for i, (cur, _, _) in enumerate(blocks):
        if first:
            m_sc[cur, :] = jnp.broadcast_to(m[i], (BL, LANES))
            l_sc[cur, :] = jnp.broadcast_to(l[i], (BL, LANES))
            acc_sc[cur, :] = acc[i]
        else:
            m_old = m_sc[cur, :]
            m_new = jnp.maximum(m_old, m[i])
            a = jnp.exp(m_old - m_new)
            b = jnp.exp(m[i] - m_new)
            m_sc[cur, :] = m_new
            l_sc[cur, :] = a * l_sc[cur, :] + b * l[i]
            acc_sc[cur, :] = a * acc_sc[cur, :] + b * acc[i]


def _dwa_kernel(q0_ref, q1_ref, q2_ref, k0_ref, v0_ref, k1_ref, v1_ref, k2_ref, v2_ref, *rest, n_cast):
    o_ref = rest[n_cast]
    m_sc, l_sc, acc_sc = rest[2 * n_cast + 1:]
    _cast_blocks(rest[:n_cast], rest[n_cast + 1:2 * n_cast + 1])
    BL = DWA_BLOCK
    T = o_ref.shape[1]
    ri = lax.broadcasted_iota(jnp.int32, (BL, BL), 0)
    ci = lax.broadcasted_iota(jnp.int32, (BL, BL), 1)
    masks = (ri >= ci, ci >= ri)
    q_refs, k_refs, v_refs = (q0_ref, q1_ref, q2_ref), (k0_ref, k1_ref, k2_ref), (v0_ref, v1_ref, v2_ref)
    scratch = (m_sc, l_sc, acc_sc)

    for gi, (_, dil) in enumerate(DWA_GROUPS):
        refs = (q_refs[gi], k_refs[gi], v_refs[gi])
        nb = T // dil // BL

        def rows(blk, r, dil=dil):
            start = blk * (BL * dil) + r
            if dil == 1:
                return pl.ds(pl.multiple_of(start, BL), BL)
            return pl.ds(start, BL, stride=dil)

        def block(n, r, nb=nb, rows=rows):
            if nb == 1:
                return rows(n, r), None, None
            return rows(n, r), rows(jnp.maximum(n - 1, 0), r), n > 0

        ns = min(dil, DWA_BATCH)
        nc = min(DWA_BATCH // ns, nb)
        assert dil % ns == 0 and nb % nc == 0
        for r0 in range(0, dil, ns):
            def body(i, carry, r0=r0, refs=refs, block=block, first=gi == 0, ns=ns, nc=nc):
                batch = [block(i * nc + u, r0 + t) for t in range(ns) for u in range(nc)]
                _dwa_blocks(refs, batch, masks, first, scratch)
                return carry
            if nb == nc:
                body(0, 0)
            else:
                lax.fori_loop(0, nb // nc, body, 0)

    o_ref[0] = (acc_sc[...] / l_sc[...]).astype(o_ref.dtype)


def _dwa_prompt(q, kvs, cast=()):
    B, T, _ = q.shape
    c_in, c_out, c_shapes = _cast_jobs(cast, lambda b, h: b * DWA_HEADS + h, B * DWA_HEADS)

    def col(c):
        return pl.BlockSpec((1, T, DWA_DH), lambda b, h: (b, 0, c(h)))

    ng = len(DWA_GROUPS)
    in_specs = [col(lambda h, g=g: g * DWA_HEADS + h) for g in range(ng)]
    args = [q] * ng
    for g in range(ng):
        in_specs += [col(lambda h: h), col(lambda h: DWA_HEADS + h)]
        args += [kvs[g], kvs[g]]
    args += [a for a, *_ in cast]
    res = pl.pallas_call(
        functools.partial(_dwa_kernel, n_cast=len(cast)),
        out_shape=[jax.ShapeDtypeStruct((B, T, DWA_GW), BF16)] + c_shapes,
        grid=(B, DWA_HEADS),
        in_specs=in_specs + c_in,
        out_specs=[col(lambda h: h)] + c_out,
        scratch_shapes=[pltpu.VMEM((T, LANES), F32), pltpu.VMEM((T, LANES), F32), pltpu.VMEM((T, DWA_DH), F32)],
        compiler_params=_params(*(("arbitrary",) * 2 if cast else ("parallel",) * 2)),
        name="dwa_prompt",
    )(*args)
    return res[0], res[1:]


def _dwa_step_kernel(q_ref, n0_ref, n1_ref, n2_ref, c0_ref, c1_ref, c2_ref, o_ref):
    new_refs = (n0_ref, n1_ref, n2_ref)
    cache_refs = (c0_ref, c1_ref, c2_ref)
    scale = DWA_DH ** -0.5
    parts = []
    for gi in range(len(DWA_GROUPS)):
        q = q_ref[0, gi * DWA_HEADS:(gi + 1) * DWA_HEADS, :]
        k = cache_refs[gi][0, :, 0, 0]
        v = cache_refs[gi][0, :, 0, 1]
        s = jnp.sum(k * q[None], axis=-1, keepdims=True) * scale
        s_new = jnp.sum(new_refs[gi][0, 0] * q, axis=-1, keepdims=True) * scale
        m = jnp.maximum(jnp.max(s, axis=0), s_new)
        p = jnp.exp(s - m[None])
        p_new = jnp.exp(s_new - m)
        l = jnp.sum(p, axis=0) + p_new
        acc = jnp.sum(p * v, axis=0) + p_new * new_refs[gi][0, 1]
        parts.append((m, l, acc))
    mm = jnp.maximum(jnp.maximum(parts[0][0], parts[1][0]), parts[2][0])
    num = jnp.zeros((DWA_HEADS, DWA_DH), F32)
    den = jnp.zeros((DWA_HEADS, 1), F32)
    for m, l, acc in parts:
        e = jnp.exp(m - mm)
        num = num + e * acc
        den = den + e * l
    o_ref[0] = num / den


def _dwa_sample(q, kv_new, caches):
    N = q.shape[0]
    span = DWA_BLOCK
    in_specs = [pl.BlockSpec((1, len(DWA_GROUPS) * DWA_HEADS, DWA_DH), lambda b: (b, 0, 0))]
    in_specs += [pl.BlockSpec((1, 2, DWA_HEADS, DWA_DH), lambda b: (b, 0, 0, 0))] * 3
    views = []
    for (win, dil), cache in zip(DWA_GROUPS, caches):
        L = cache.shape[1]
        assert L == win and L // dil == span
        views.append(cache.reshape(N, span, dil, 2, DWA_HEADS, DWA_DH))
        in_specs.append(pl.BlockSpec((1, span, 1, 2, DWA_HEADS, DWA_DH), lambda b: (b, 0, 0, 0, 0, 0)))
    return pl.pallas_call(
        _dwa_step_kernel,
        out_shape=jax.ShapeDtypeStruct((N, DWA_HEADS, DWA_DH), F32),
        grid=(N,),
        in_specs=in_specs,
        out_specs=pl.BlockSpec((1, DWA_HEADS, DWA_DH), lambda b: (b, 0, 0)),
        compiler_params=_params("parallel"),
        name="dwa_step",
    )(q, *kv_new, *views)


def _mix_out_kernel(oa_ref, ob_ref, ga_ref, gb_ref, h_ref, gt_ref, nsh_ref, nsc_ref, ngain_ref,
                    wa_ref, wb_ref, wo_ref, *rest, n_cast):
    out_ref, nxt_ref = rest[n_cast:n_cast + 2]
    _cast_blocks(rest[:n_cast], rest[n_cast + 2:])
    oa, ob = oa_ref[0], ob_ref[0]
    y = None
    for c in range(0, D_MODEL, D_MODEL // 2):
        cs = slice(c, c + D_MODEL // 2)
        merged = (_sigmoid(ga_ref[0, :, cs].astype(F32)) * _dot(oa, wa_ref[:, cs])
                  + _sigmoid(gb_ref[0, :, cs].astype(F32)) * _dot(ob, wb_ref[:, cs]))
        part = _dot(merged.astype(BF16), wo_ref[cs, :])
        y = part if y is None else y + part
    gate, nsc, nsh = gt_ref[0], nsc_ref[0], nsh_ref[0]
    for rows in _row_chunks(y.shape[0]):
        h = h_ref[0, rows, :] + _rows_of(gate, rows) * y[rows]
        out_ref[0, rows, :] = h
        nxt_ref[0, rows, :] = _modulated(h, ngain_ref[...], _rows_of(nsc, rows), _rows_of(nsh, rows))


MIX_TM = 512


def _mix_out(oa, ob, gates, h, mod3, wa, wb, wo, next_gain, cast=()):
    G, R, _ = h.shape
    tm = min(MIX_TM, R)
    c_in, c_out, c_shapes = _cast_jobs(cast, lambda g, i: g * (R // tm) + i, G * (R // tm))

    def rows(width, col=0):
        return pl.BlockSpec((1, tm, width), lambda g, i: (g, i, col))

    def whole(shape):
        return pl.BlockSpec(shape, lambda g, i: (0, 0), pipeline_mode=pl.Buffered(1))

    res = pl.pallas_call(
        functools.partial(_mix_out_kernel, n_cast=len(cast)),
        out_shape=[jax.ShapeDtypeStruct((G, R, D_MODEL), F32), jax.ShapeDtypeStruct((G, R, D_MODEL), BF16)]
        + c_shapes,
        grid=(G, R // tm),
        in_specs=[rows(GDN_QK), rows(DWA_GW), rows(D_MODEL, 0), rows(D_MODEL, 1), rows(D_MODEL),
                  _mod_spec(mod3, R, tm, 5), _mod_spec(mod3, R, tm, 6), _mod_spec(mod3, R, tm, 7),
                  pl.BlockSpec((1, D_MODEL), lambda g, i: (0, 0)),
                  whole(wa.shape), whole(wb.shape), whole(wo.shape)] + c_in,
        out_specs=[rows(D_MODEL), rows(D_MODEL)] + c_out,
        compiler_params=_params(*(("arbitrary",) * 2 if cast else ("parallel",) * 2)),
        name="mix_out",
    )(oa, ob, gates, gates, h, mod3, mod3, mod3, next_gain.reshape(1, D_MODEL), wa, wb, wo,
      *[a for a, *_ in cast])
    return res[0], res[1], res[2:]


_W_IN_COLS = _C_GB + D_MODEL
_W_IN_ALIGNED = _W_IN_COLS // CAST_BLOCK * CAST_BLOCK


def _mixer_weights(wt_f32, wt_bf, a_log, dt_bias):
    w = wt_bf
    order = np.array(_HEAD_ORDER)
    ba_rows = np.concatenate([_C_B + order, _C_A + order])
    par = jnp.zeros((8, LANES), F32)
    par = par.at[0, 8:16].set(a_log[0][order]).at[1, 8:16].set(dt_bias[0][order])
    return dict(
        az=w,
        ba=jnp.pad(w[ba_rows], ((0, LANES - 2 * GDN_HEADS), (0, 0))),
        attn=jnp.concatenate(
            [w[c + g * DWA_GW:c + (g + 1) * DWA_GW] for g in range(len(DWA_GROUPS)) for c in (_C_K, _C_V)]
            + [w[_C_Q:_C_K]], axis=0),
        gates=jnp.concatenate([w[_C_GA:], wt_f32[_W_IN_ALIGNED:].astype(BF16)], axis=0),
        par=par,
    )


def _trunk(h1, xm, mod3, pos, W, norms, ffn2, *, sample_state=None):
    G, R, _ = h1.shape
    up2, down2 = ffn2
    tables = _rope_tables(pos)
    ba = _proj(xm, W["ba"], F32, tn=LANES, name="proj_ba")
    gates = _proj(xm, W["gates"], BF16, tn=2048, name="proj_gates")
    kvs, q = _attn_proj(xm, W["attn"], tables)
    if sample_state is None:
        az, conv_rows = _proj(xm, W["az"], BF16, n=_C_B, tn=2048, tail=True, name="proj_az")
        o_a, s_new = _gdn_prompt(az, ba, W["conv"], W["par"], W["gn"])
        o_b, (up2,) = _dwa_prompt(q, kvs, cast=((up2, 1, 2 * D_FF, 4 * CAST_BLOCK),))
        down_cast = ((down2, 0, D_FF, 2 * CAST_BLOCK),)
    else:
        hist, s0, caches = sample_state
        conv_rows = _proj(xm, W["az"], F32, n=_C_B, tn=2048, name="proj_az").reshape(R, 1, -1)
        o_a, s_new = _gdn_sample(conv_rows, ba.reshape(R, 1, LANES), hist, s0, W["conv"], W["par"], W["gn"])
        o_b = _dwa_sample(q.reshape(R, len(DWA_GROUPS) * DWA_HEADS, DWA_DH),
                          [kv.reshape(R, 2, DWA_HEADS, DWA_DH) for kv in kvs], caches)
        o_a = o_a.reshape(1, R, -1)
        o_b = o_b.reshape(1, R, DWA_GW).astype(BF16)
        down_cast = ()
    h2, xm2, cast_out = _mix_out(o_a, o_b, gates, h1, mod3, W["wa"], W["wb"], W["wo"], norms["ffn2"],
                                 cast=down_cast)
    if down_cast:
        (down2,) = cast_out
    y = _ffn_last(h2, xm2, mod3, up2, down2, norms["final"])
    return y, conv_rows[:, :, :CONV_CH], s_new, kvs, (up2, down2)


def kernel(x_prompt, x_sample, state_conv, state_delta, cache_kv_w128, cache_kv_w512, cache_kv_w2048, c_prompt, c_sample, w_ada, b_ada, norm_ffn1, w_ffn1_up, w_ffn1_down, norm_mix, w_in, conv_w, a_log, dt_bias, gdn_norm, w_proj_a, w_proj_b, w_out, norm_ffn2, w_ffn2_up, w_ffn2_down, norm_final):
    B, T, _ = x_prompt.shape
    N, S, _ = x_sample.shape
    assert S == 1 and T % (DWA_BLOCK * DWA_GROUPS[-1][1]) == 0
    norms = dict(ffn1=norm_ffn1[0], mix=norm_mix[0], ffn2=norm_ffn2[0], final=norm_final)

    mod = _ada(jnp.concatenate([c_prompt, c_sample], axis=0), w_ada[0], b_ada)
    mod_p = mod[:B].reshape(B, 1, N_MOD * D_MODEL)
    mod_s = mod[B:].reshape(1, N, N_MOD * D_MODEL)

    up1, down1 = w_ffn1_up[0].astype(BF16), w_ffn1_down[0].astype(BF16)
    w_in_t = jnp.swapaxes(w_in[0], 0, 1)
    h1_p, xm_p, w_in_bf = _ffn_first(x_prompt, mod_p, norms["ffn1"], up1, down1, norms["mix"],
                                     cast=((w_in_t, 0, _W_IN_ALIGNED, CAST_BLOCK),))
    W = _mixer_weights(w_in_t, w_in_bf, a_log, dt_bias)
    W.update(conv=conv_w[0], gn=gdn_norm,
             wa=w_proj_a[0].astype(BF16), wb=w_proj_b[0].astype(BF16), wo=w_out[0].astype(BF16))
    h1_s, xm_s = _ffn_first(x_sample.reshape(1, N, D_MODEL), mod_s, norms["ffn1"], up1, down1, norms["mix"])

    y_p, rows_p, s_p, kv_p, ffn2 = _trunk(h1_p, xm_p, mod_p, jnp.arange(T, dtype=jnp.int32), W, norms,
                                          (w_ffn2_up[0], w_ffn2_down[0]))
    caches = (cache_kv_w128[0], cache_kv_w512[0], cache_kv_w2048[0])
    y_s, rows_s, s_s, kv_s, _ = _trunk(h1_s, xm_s, mod_s, jnp.full((N,), PAST_LEN, dtype=jnp.int32), W, norms,
                                       ffn2, sample_state=(state_conv[0], state_delta[0], caches))

    keep = GDN_CONV - 1
    conv_p = rows_p[:, TAIL_ROWS - keep:][None]
    conv_s = jnp.concatenate([state_conv[0], rows_s], axis=1)[:, -keep:][None]
    kv_out_p = []
    for (win, _), kv in zip(DWA_GROUPS, kv_p):
        k = min(win, T)
        kv_out_p.append(kv[:, T - k:].reshape(1, B, k, 2, DWA_HEADS, DWA_DH))
    kv_out_s = [kv.reshape(1, N, 1, 2, DWA_HEADS, DWA_DH) for kv in kv_s]
    return (y_p, y_s.reshape(N, 1, D_MODEL), conv_p, s_p[None], *kv_out_p,
            conv_s, s_s[None], *kv_out_s)
```
